```python
import math
import jax, jax.numpy as jnp
from jax import lax
import numpy as np

D_MODEL = 1024
BATCH = 16
SEQ = 256
DEPTH = 4
DEC_BATCH = 8
DEC_SEQ = 4096
PAST_LEN = 256

GRID_W = 64
N_EVEN = (DEPTH + 1) // 2
N_ODD = DEPTH // 2
SSD_INNER = D_MODEL
SSD_HEAD_DIM = 64
SSD_HEADS = SSD_INNER // SSD_HEAD_DIM
SSD_GROUPS = 4
SSD_STATE = 128
SSD_CONV = 4
SSD_CHUNK = 128
SSD_CONV_DIM = SSD_INNER + 2 * SSD_GROUPS * SSD_STATE
SCONV_WIDTH = D_MODEL
SCONV_K = 3
EVEN_IN = SSD_INNER + SSD_CONV_DIM + 2 * SSD_HEADS + 3 * SCONV_WIDTH
EVEN_OUT = SSD_INNER + SCONV_WIDTH
HEAD_DIM = 64
N_HEADS = D_MODEL // HEAD_DIM
KV_HEADS = N_HEADS // 4
Q_DIM = N_HEADS * HEAD_DIM
KV_DIM = KV_HEADS * HEAD_DIM
Q_BLOCK = 128
ROPE_THETA = 10000.0
ROPE_AXIS = HEAD_DIM // 2
ATTN_SCALE = HEAD_DIM ** -0.5
CONF_WIDTH = D_MODEL
CONF_K = 31
ODD_IN = Q_DIM + 2 * KV_DIM + 2 * CONF_WIDTH
ODD_OUT = Q_DIM + CONF_WIDTH
N_EXPERTS = 32
TOP_K = 4
D_FF = D_MODEL
SWIGLU_LIMIT = 7.0
SWIGLU_ALPHA = 1.702
MOE_BLOCK = 128
NORM_EPS = 1e-6

kernel_name = "hybrid_ssd_conv_gqa_conformer_moe_dit_step"

F32 = jnp.float32


def rmsnorm(x, w):
    xf = x.astype(F32)
    y = xf * lax.rsqrt(jnp.mean(xf * xf, axis=-1, keepdims=True) + NORM_EPS)
    return y.astype(x.dtype) * w


def layernorm(x, w, b):
    xf = x.astype(F32)
    mu = jnp.mean(xf, axis=-1, keepdims=True)
    var = jnp.mean(jnp.square(xf - mu), axis=-1, keepdims=True)
    return ((xf - mu) * lax.rsqrt(var + NORM_EPS)).astype(x.dtype) * w + b


def modulation(cvec, w, b):
    m = jax.nn.silu(cvec) @ w + b
    return [t[:, None, :] for t in jnp.split(m, 6, axis=-1)]


def modulate(h, shift, scale):
    return h * (1.0 + scale) + shift


def dwconv(x, w, b, pad):
    ch = x.shape[-1]
    y = lax.conv_general_dilated(x, w[:, None, :], window_strides=(1,), padding=[pad],
                                 dimension_numbers=('NWC', 'WIO', 'NWC'), feature_group_count=ch)
    return y if b is None else y + b


def _segsum(a):
    t = a.shape[-1]
    cs = jnp.cumsum(a, axis=-1)
    diff = cs[..., :, None] - cs[..., None, :]
    return jnp.where(jnp.tril(jnp.ones((t, t), dtype=bool)), diff, -jnp.inf)


def ssd_scan(x, a, bm, cm, h0):
    b, L, H, P = x.shape
    G, N = bm.shape[2], bm.shape[3]
    R = H // G
    nc = L // SSD_CHUNK
    dt_ = x.dtype
    xc = x.reshape(b, nc, SSD_CHUNK, G, R, P)
    bc = bm.reshape(b, nc, SSD_CHUNK, G, N)
    cc = cm.reshape(b, nc, SSD_CHUNK, G, N)
    ac = a.astype(F32).reshape(b, nc, SSD_CHUNK, G, R).transpose(0, 3, 4, 1, 2)
    a_cum = jnp.cumsum(ac, axis=-1)
    decay_in = jnp.exp(_segsum(ac)).astype(dt_)
    cb = jnp.einsum('bclgn,bcsgn->bgcls', cc, bc)
    y_diag = jnp.einsum('bgrcls,bcsgrp->bclgrp', cb[:, :, None] * decay_in, xc)
    decay_st = jnp.exp(a_cum[..., -1:] - a_cum).astype(dt_).transpose(0, 3, 4, 1, 2)
    states = jnp.einsum('bclgn,bclgrp->bcgrpn', bc, xc * decay_st[..., None])
    states = jnp.concatenate([h0.astype(dt_).reshape(b, 1, G, R, P, N), states], axis=1)
    chunk_tot = jnp.pad(a_cum[..., -1], ((0, 0), (0, 0), (0, 0), (1, 0)))
    decay_chunk = jnp.exp(_segsum(chunk_tot)).astype(dt_)
    new_states = jnp.einsum('bgrzc,bcgrpn->bzgrpn', decay_chunk, states)
    prev_states, final = new_states[:, :-1], new_states[:, -1]
    decay_out = jnp.exp(a_cum).astype(dt_).transpose(0, 3, 4, 1, 2)
    y_off = jnp.einsum('bclgn,bcgrpn->bclgrp', cc, prev_states) * decay_out[..., None]
    y = (y_diag + y_off).reshape(b, L, H, P)
    return y, final.reshape(b, H, P, N)


def even_mixer(h, w_in, conv_w, conv_b, dt_bias, a_log, d_skip, norm_w, sconv_w, w_out, h0):
    b, L, _ = h.shape
    proj = h @ w_in
    z, xbc, dt, bcx = jnp.split(proj, [SSD_INNER, SSD_INNER + SSD_CONV_DIM,
                                       SSD_INNER + SSD_CONV_DIM + 2 * SSD_HEADS], axis=-1)
    xbc = jax.nn.silu(dwconv(xbc, conv_w, conv_b, (SSD_CONV // 2, SSD_CONV - 1 - SSD_CONV // 2)))
    xh, bm, cm = jnp.split(xbc, [SSD_INNER, SSD_INNER + SSD_GROUPS * SSD_STATE], axis=-1)
    xh = xh.reshape(b, L, SSD_HEADS, SSD_HEAD_DIM)
    bm = bm.reshape(b, L, SSD_GROUPS, SSD_STATE)
    cm = cm.reshape(b, L, SSD_GROUPS, SSD_STATE)
    dt = jax.nn.softplus(dt.reshape(b, L, 2, SSD_HEADS) + dt_bias)
    A = -jnp.exp(a_log)
    y_f, s_f = ssd_scan(xh * dt[:, :, 0, :, None], dt[:, :, 0] * A[0], bm, cm, h0[:, 0])
    y_b, s_b = ssd_scan(jnp.flip(xh * dt[:, :, 1, :, None], 1), jnp.flip(dt[:, :, 1] * A[1], 1),
                        jnp.flip(bm, 1), jnp.flip(cm, 1), h0[:, 1])
    y = y_f + jnp.flip(y_b, 1) + xh * d_skip[:, None]
    y = rmsnorm(y.reshape(b, L, SSD_INNER) * jax.nn.silu(z), norm_w)
    g_b, g_c, xc = jnp.split(bcx, 3, axis=-1)
    yc = g_b * dwconv(g_c * xc, sconv_w, None, (SCONV_K // 2, SCONV_K // 2))
    out = jnp.concatenate([y, yc], axis=-1) @ w_out
    return out, jnp.stack([s_f, s_b], axis=1)


def axial_rope_tables(L, dtype):
    rows = L // GRID_W
    row = jnp.repeat(jnp.arange(rows), GRID_W).astype(F32)
    col = jnp.tile(jnp.arange(GRID_W), rows).astype(F32)
    inv = ROPE_THETA ** (-jnp.arange(0, ROPE_AXIS, 2, dtype=F32) / ROPE_AXIS)
    ang = jnp.concatenate([row[:, None] * inv, col[:, None] * inv], axis=-1)
    return jnp.cos(ang).astype(dtype), jnp.sin(ang).astype(dtype)


def apply_rope(x, cos, sin):
    half = HEAD_DIM // 2
    x1, x2 = x[..., :half], x[..., half:]
    cos, sin = cos[None, :, None, :], sin[None, :, None, :]
    return jnp.concatenate([x1 * cos - x2 * sin, x2 * cos + x1 * sin], axis=-1)


def attend_blocks(q, k, v):
    b, Lq = q.shape[0], q.shape[1]
    nb = Lq // Q_BLOCK
    qb = q.reshape(b, nb, Q_BLOCK, KV_HEADS, N_HEADS // KV_HEADS, HEAD_DIM).transpose(1, 0, 2, 3, 4, 5)

    def one(qi):
        s = jnp.einsum('bqgrd,bkgd->bgrqk', qi, k).astype(F32) * ATTN_SCALE
        p = jax.nn.softmax(s, axis=-1).astype(v.dtype)
        return jnp.einsum('bgrqk,bkgd->bqgrd', p, v)

    o = lax.map(one, qb)
    return o.transpose(1, 0, 2, 3, 4, 5).reshape(b, Lq, Q_DIM)


def odd_mixer(h, w_in, q_norm_w, k_norm_w, dw_w, dw_b, ln_w, ln_b, w_out, rope=None, ctx_k=None, ctx_v=None):
    b, L, _ = h.shape
    proj = h @ w_in
    q, k, v, glu = jnp.split(proj, [Q_DIM, Q_DIM + KV_DIM, Q_DIM + 2 * KV_DIM], axis=-1)
    q = rmsnorm(q.reshape(b, L, N_HEADS, HEAD_DIM), q_norm_w)
    k = rmsnorm(k.reshape(b, L, KV_HEADS, HEAD_DIM), k_norm_w)
    v = v.reshape(b, L, KV_HEADS, HEAD_DIM)
    if rope is None:
        keys, vals = k, v
    else:
        cos, sin = rope
        q = apply_rope(q, cos, sin)
        keys = jnp.concatenate([ctx_k, apply_rope(k, cos, sin)], axis=1)
        vals = jnp.concatenate([ctx_v, v], axis=1)
    att = attend_blocks(q, keys, vals)
    a, g = jnp.split(glu, 2, axis=-1)
    u = dwconv(a * jax.nn.sigmoid(g), dw_w, dw_b, (CONF_K // 2, CONF_K // 2))
    u = jax.nn.silu(layernorm(u, ln_w, ln_b))
    out = jnp.concatenate([att, u], axis=-1) @ w_out
    return out, k, v


def moe(x, router_w, router_b, w_gu, b_gu, w_dn, b_dn):
    T, D = x.shape
    logits = (x @ router_w + router_b).astype(F32)
    top_v, top_i = lax.top_k(logits, TOP_K)
    gates = jax.nn.softmax(top_v, axis=-1).astype(x.dtype)
    TK = T * TOP_K
    flat_e = top_i.reshape(TK)
    order = jnp.argsort(flat_e)
    sorted_e = flat_e[order]
    counts = jnp.bincount(flat_e, length=N_EXPERTS)
    padded = (counts + MOE_BLOCK - 1) // MOE_BLOCK * MOE_BLOCK
    pad_end = jnp.cumsum(padded)
    pad_start = pad_end - padded
    start = jnp.cumsum(counts) - counts
    dest = pad_start[sorted_e] + jnp.arange(TK) - start[sorted_e]
    n_blocks = (TK + N_EXPERTS * (MOE_BLOCK - 1) + MOE_BLOCK - 1) // MOE_BLOCK
    slot_tok = jnp.full((n_blocks * MOE_BLOCK,), T, jnp.int32).at[dest].set((order // TOP_K).astype(jnp.int32))
    block_e = jnp.minimum(jnp.searchsorted(pad_end, jnp.arange(n_blocks) * MOE_BLOCK, side='right'),
                          N_EXPERTS - 1)
    xpad = jnp.concatenate([x, jnp.zeros((1, D), x.dtype)], axis=0)
    xb = xpad[slot_tok].reshape(n_blocks, MOE_BLOCK, D)

    def expert_block(args):
        xi, e = args
        hg = xi @ w_gu[e] + b_gu[e]
        g, u = jnp.split(hg, 2, axis=-1)
        g = jnp.minimum(g, SWIGLU_LIMIT)
        u = jnp.clip(u, -SWIGLU_LIMIT, SWIGLU_LIMIT)
        act = g * jax.nn.sigmoid(SWIGLU_ALPHA * g) * (u + 1.0)
        return act @ w_dn[e] + b_dn[e]

    yb = lax.map(expert_block, (xb, block_e)).reshape(n_blocks * MOE_BLOCK, D)
    y_assign = jnp.zeros((TK, D), x.dtype).at[order].set(yb[dest])
    return jnp.einsum('tkd,tk->td', y_assign.reshape(T, TOP_K, D), gates)


def setup_inputs(seed: int = 0) -> dict:
    key = jax.random.key(seed)
    keys = iter(jax.random.split(key, 48))

    def nrm(shape, scale):
        return jax.random.normal(next(keys), shape, F32) * scale

    dt0 = jnp.exp(jax.random.uniform(next(keys), (N_EVEN, 2, SSD_HEADS), F32, math.log(1e-3), math.log(1e-1)))
    return {
        "x_prompt": nrm((BATCH, SEQ, D_MODEL), 1.0),
        "x_sample": nrm((DEC_BATCH, DEC_SEQ, D_MODEL), 1.0),
        "state_ssd": nrm((DEC_BATCH, N_EVEN, 2, SSD_HEADS, SSD_HEAD_DIM, SSD_STATE), 0.1),
        "cache_k": nrm((DEC_BATCH, N_ODD, PAST_LEN, KV_HEADS, HEAD_DIM), 1.0),
        "cache_v": nrm((DEC_BATCH, N_ODD, PAST_LEN, KV_HEADS, HEAD_DIM), 1.0),
        "c": nrm((DEC_BATCH, D_MODEL), 1.0),
        "c_ctx": nrm((D_MODEL,), 1.0),
        "w_mod": nrm((DEPTH, D_MODEL, 6 * D_MODEL), 0.5 * D_MODEL ** -0.5),
        "b_mod": nrm((DEPTH, 6 * D_MODEL), 0.02),
        "norm_mix_w": 1.0 + nrm((DEPTH, D_MODEL), 0.1),
        "norm_ffn_w": 1.0 + nrm((DEPTH, D_MODEL), 0.1),
        "ev_w_in": nrm((N_EVEN, D_MODEL, EVEN_IN), D_MODEL ** -0.5),
        "ev_conv_w": nrm((N_EVEN, SSD_CONV, SSD_CONV_DIM), SSD_CONV ** -0.5),
        "ev_conv_b": nrm((N_EVEN, SSD_CONV_DIM), 0.02),
        "ev_dt_bias": dt0 + jnp.log(-jnp.expm1(-dt0)),
        "ev_a_log": jnp.log(jax.random.uniform(next(keys), (N_EVEN, 2, SSD_HEADS), F32, 1.0, 16.0)),
        "ev_d_skip": 1.0 + nrm((N_EVEN, SSD_HEADS), 0.1),
        "ev_norm_w": 1.0 + nrm((N_EVEN, SSD_INNER), 0.1),
        "ev_sconv_w": nrm((N_EVEN, SCONV_K, SCONV_WIDTH), SCONV_K ** -0.5),
        "ev_w_out": nrm((N_EVEN, EVEN_OUT, D_MODEL), EVEN_OUT ** -0.5),
        "od_w_in": nrm((N_ODD, D_MODEL, ODD_IN), D_MODEL ** -0.5),
        "od_q_norm_w": 1.0 + nrm((N_ODD, HEAD_DIM), 0.1),
        "od_k_norm_w": 1.0 + nrm((N_ODD, HEAD_DIM), 0.1),
        "od_dw_w": nrm((N_ODD, CONF_K, CONF_WIDTH), CONF_K ** -0.5),
        "od_dw_b": nrm((N_ODD, CONF_WIDTH), 0.02),
        "od_ln_w": 1.0 + nrm((N_ODD, CONF_WIDTH), 0.1),
        "od_ln_b": nrm((N_ODD, CONF_WIDTH), 0.02),
        "od_w_out": nrm((N_ODD, ODD_OUT, D_MODEL), ODD_OUT ** -0.5),
        "router_w": nrm((DEPTH, D_MODEL, N_EXPERTS), D_MODEL ** -0.5),
        "router_b": nrm((DEPTH, N_EXPERTS), 0.01),
        "w_gu": nrm((DEPTH, N_EXPERTS, D_MODEL, 2 * D_FF), D_MODEL ** -0.5),
        "b_gu": nrm((DEPTH, N_EXPERTS, 2 * D_FF), 0.02),
        "w_dn": nrm((DEPTH, N_EXPERTS, D_FF, D_MODEL), D_FF ** -0.5),
        "b_dn": nrm((DEPTH, N_EXPERTS, D_MODEL), 0.02),
        "final_norm_w": 1.0 + nrm((D_MODEL,), 0.1),
    }


def reference(x_prompt, x_sample, state_ssd, cache_k, cache_v, c, c_ctx,
              w_mod, b_mod, norm_mix_w, norm_ffn_w,
              ev_w_in, ev_conv_w, ev_conv_b, ev_dt_bias, ev_a_log, ev_d_skip, ev_norm_w, ev_sconv_w, ev_w_out,
              od_w_in, od_q_norm_w, od_k_norm_w, od_dw_w, od_dw_b, od_ln_w, od_ln_b, od_w_out,
              router_w, router_b, w_gu, b_gu, w_dn, b_dn, final_norm_w):
    bp, sp, _ = x_prompt.shape
    bs, ss, _ = x_sample.shape
    n_ctx_tok = bp * sp
    rope = axial_rope_tables(ss, x_sample.dtype)
    h0_ctx = jnp.zeros((bp, 2, SSD_HEADS, SSD_HEAD_DIM, SSD_STATE), x_prompt.dtype)
    xp, xs = x_prompt, x_sample
    ssd_states, ctx_keys, ctx_vals = [], [], []
    for l in range(DEPTH):
        sh1p, sc1p, g1p, sh2p, sc2p, g2p = modulation(c_ctx[None], w_mod[l], b_mod[l])
        sh1s, sc1s, g1s, sh2s, sc2s, g2s = modulation(c, w_mod[l], b_mod[l])
        hp = modulate(rmsnorm(xp, norm_mix_w[l]), sh1p, sc1p)
        hs = modulate(rmsnorm(xs, norm_mix_w[l]), sh1s, sc1s)
        i = l // 2
        if l % 2 == 0:
            ev = (ev_w_in[i], ev_conv_w[i], ev_conv_b[i], ev_dt_bias[i], ev_a_log[i], ev_d_skip[i],
                  ev_norm_w[i], ev_sconv_w[i], ev_w_out[i])
            yp, st = even_mixer(hp, *ev, h0_ctx)
            ys, _ = even_mixer(hs, *ev, state_ssd[:, i])
            ssd_states.append(st)
        else:
            od = (od_w_in[i], od_q_norm_w[i], od_k_norm_w[i], od_dw_w[i], od_dw_b[i],
                  od_ln_w[i], od_ln_b[i], od_w_out[i])
            yp, kp, vp = odd_mixer(hp, *od)
            ys, _, _ = odd_mixer(hs, *od, rope=rope, ctx_k=cache_k[:, i], ctx_v=cache_v[:, i])
            ctx_keys.append(kp)
            ctx_vals.append(vp)
        xp = xp + g1p * yp
        xs = xs + g1s * ys
        hp = modulate(rmsnorm(xp, norm_ffn_w[l]), sh2p, sc2p)
        hs = modulate(rmsnorm(xs, norm_ffn_w[l]), sh2s, sc2s)
        tok = jnp.concatenate([hp.reshape(n_ctx_tok, D_MODEL), hs.reshape(bs * ss, D_MODEL)], axis=0)
        f = moe(tok, router_w[l], router_b[l], w_gu[l], b_gu[l], w_dn[l], b_dn[l])
        xp = xp + g2p * f[:n_ctx_tok].reshape(xp.shape)
        xs = xs + g2s * f[n_ctx_tok:].reshape(xs.shape)
    y_prompt = rmsnorm(xp, final_norm_w)
    y_sample = rmsnorm(xs, final_norm_w)
    new_state_ssd = jnp.stack(ssd_states, axis=1)
    new_cache_k = jnp.stack(ctx_keys, axis=1)
    new_cache_v = jnp.stack(ctx_vals, axis=1)
    return (y_prompt, y_sample, new_state_ssd, new_cache_k, new_cache_v)
```

```python
import functools

import numpy as np
import jax
import jax.numpy as jnp
from jax import lax
from jax.experimental import pallas as pl
from jax.experimental.pallas import tpu as pltpu

F32 = jnp.float32
BF16 = jnp.bfloat16
HIGHEST = lax.Precision.HIGHEST

NORM_EPS = 1e-6
GRID_W = 64
ROPE_THETA = 10000.0
HEAD_DIM = 64
KV_HEADS = 4
Q_PER_KV = 4
SSD_HEAD_DIM = 64
SSD_GROUPS = 4
SSD_HEADS_PER_GROUP = 4
SSD_STATE = 128
SSD_CHUNK = 128
SSD_CONV = 4
SCONV_K = 3
CONF_K = 31
N_EXPERTS = 32
TOP_K = 4
SWIGLU_LIMIT = 7.0
SWIGLU_ALPHA = 1.702

SEG = 256
HALO = 16
LANES = 128
MOE_ROWS = 256
ATT_TK = 512
NEG_BIG = -1e30
VMEM_LIMIT = 56 * 1024 * 1024

NT_DIMS = (((1,), (1,)), ((), ()))
TN_DIMS = (((0,), (0,)), ((), ()))


def _cparams(n_grid):
    return pltpu.CompilerParams(dimension_semantics=("arbitrary",) * n_grid,
                                vmem_limit_bytes=VMEM_LIMIT)


def _resident(shape):
    nd = len(shape)
    return pl.BlockSpec(shape, lambda *_: (0,) * nd, pipeline_mode=pl.Buffered(1))


def _silu(x):
    return x * jax.nn.sigmoid(x)


def _mod_body(c_ref, w_ref, b_ref, o_ref):
    c = c_ref[...]
    s = _silu(c).astype(BF16)
    o_ref[0] = jnp.dot(s, w_ref[0].astype(BF16), preferred_element_type=F32) + b_ref[0]


def _modulation(cvec, w_mod, b_mod):
    depth, d, n = w_mod.shape
    rows = cvec.shape[0]
    tn = 1536
    assert n % tn == 0
    return pl.pallas_call(
        _mod_body,
        grid=(depth, n // tn),
        in_specs=[pl.BlockSpec((rows, d), lambda l, j: (0, 0)),
                  pl.BlockSpec((1, d, tn), lambda l, j: (l, 0, j)),
                  pl.BlockSpec((1, 1, tn), lambda l, j: (l, 0, j))],
        out_specs=pl.BlockSpec((1, rows, tn), lambda l, j: (l, 0, j)),
        out_shape=jax.ShapeDtypeStruct((depth, rows, n), F32),
        compiler_params=_cparams(2),
        name="modulation",
    )(cvec, w_mod, b_mod.reshape(depth, 1, n))


def _normed(x, nw_ref, sh_ref, sc_ref):
    ms = jnp.mean(x * x, axis=-1, keepdims=True)
    h = x * lax.rsqrt(ms + NORM_EPS) * nw_ref[...]
    return h * (1.0 + sc_ref[0]) + sh_ref[0]


def _proj_body(mrow_ref, x_ref, nw_ref, sh_ref, sc_ref, *rest, n_out, col_chunk):
    del mrow_ref
    w_refs, o_refs = rest[:n_out], rest[n_out:]
    hb = _normed(x_ref[...], nw_ref, sh_ref, sc_ref).astype(BF16)
    for w_ref, o_ref in zip(w_refs, o_refs):
        n = w_ref.shape[1]
        for c0 in range(0, n, col_chunk):
            c1 = min(n, c0 + col_chunk)
            o_ref[:, c0:c1] = jnp.dot(hb, w_ref[:, c0:c1],
                                      preferred_element_type=F32).astype(o_ref.dtype)


def _mod_spec(d, chunk):
    return pl.BlockSpec((1, 1, d), lambda i, mrow: (mrow[i], 0, chunk))


def _fused_proj(x, mrow, norm_w, mods, shift_chunk, weights, out_dtypes):
    t, d = x.shape
    nseg = t // SEG
    n_out = len(weights)
    in_specs = [pl.BlockSpec((SEG, d), lambda i, mrow: (i, 0)),
                pl.BlockSpec((1, d), lambda i, mrow: (0, 0)),
                _mod_spec(d, shift_chunk), _mod_spec(d, shift_chunk + 1)]
    in_specs += [_resident(w.shape) for w in weights]
    out_specs = [pl.BlockSpec((SEG, w.shape[1]), lambda i, mrow: (i, 0)) for w in weights]
    out_shape = [jax.ShapeDtypeStruct((t, w.shape[1]), dt) for w, dt in zip(weights, out_dtypes)]
    return pl.pallas_call(
        functools.partial(_proj_body, n_out=n_out, col_chunk=512),
        grid_spec=pltpu.PrefetchScalarGridSpec(
            num_scalar_prefetch=1, grid=(nseg,), in_specs=in_specs, out_specs=out_specs),
        out_shape=out_shape,
        compiler_params=_cparams(1),
        name="norm_mod_proj",
    )(mrow, x, norm_w.reshape(1, d), mods, mods, *weights)


def _fill(scr, prev, cur, nxt, keep_prev, keep_next):
    scr[0:HALO, :] = prev * keep_prev
    scr[HALO:HALO + SEG, :] = cur
    scr[HALO + SEG:HALO + SEG + HALO, :] = nxt * keep_next


def _taps(scr, w_ref, c0, k_taps, left):
    acc = None
    for k in range(k_taps):
        term = scr[pl.ds(HALO - left + k, SEG), c0:c0 + LANES] * w_ref[k:k + 1, c0:c0 + LANES]
        acc = term if acc is None else acc + term
    return acc


def _keep(first_ref, last_ref):
    i = pl.program_id(0)
    return (1 - first_ref[i]).astype(F32), (1 - last_ref[i]).astype(F32)


def _ssdconv_body(first_ref, last_ref, p_ref, c_ref, n_ref, w_ref, b_ref, o_ref, scr):
    kp, kn = _keep(first_ref, last_ref)
    _fill(scr, p_ref[...].astype(F32), c_ref[...].astype(F32), n_ref[...].astype(F32), kp, kn)
    for c0 in range(0, o_ref.shape[1], LANES):
        y = _taps(scr, w_ref, c0, SSD_CONV, SSD_CONV // 2) + b_ref[:, c0:c0 + LANES]
        o_ref[:, c0:c0 + LANES] = _silu(y).astype(o_ref.dtype)


def _sconv_body(first_ref, last_ref, gp_ref, gc_ref, gn_ref, xp_ref, xc_ref, xn_ref, gb_ref,
                w_ref, o_ref, scr):
    kp, kn = _keep(first_ref, last_ref)
    f = lambda a, b: a[...].astype(F32) * b[...].astype(F32)
    _fill(scr, f(gp_ref, xp_ref), f(gc_ref, xc_ref), f(gn_ref, xn_ref), kp, kn)
    for c0 in range(0, o_ref.shape[1], LANES):
        y = _taps(scr, w_ref, c0, SCONV_K, SCONV_K // 2)
        o_ref[:, c0:c0 + LANES] = (gb_ref[:, c0:c0 + LANES].astype(F32) * y).astype(o_ref.dtype)


def _conf_body(first_ref, last_ref, ap_ref, ac_ref, an_ref, gp_ref, gc_ref, gn_ref,
               w_ref, b_ref, lnw_ref, lnb_ref, o_ref, scr, u_scr):
    kp, kn = _keep(first_ref, last_ref)
    f = lambda a, g: a[...].astype(F32) * jax.nn.sigmoid(g[...].astype(F32))
    _fill(scr, f(ap_ref, gp_ref), f(ac_ref, gc_ref), f(an_ref, gn_ref), kp, kn)
    for c0 in range(0, o_ref.shape[1], LANES):
        u_scr[:, c0:c0 + LANES] = _taps(scr, w_ref, c0, CONF_K, CONF_K // 2) + b_ref[:, c0:c0 + LANES]
    u = u_scr[...]
    mu = jnp.mean(u, axis=-1, keepdims=True)
    uc = u - mu
    var = jnp.mean(uc * uc, axis=-1, keepdims=True)
    y = uc * lax.rsqrt(var + NORM_EPS) * lnw_ref[...] + lnb_ref[...]
    o_ref[...] = _silu(y).astype(o_ref.dtype)


def _halo_specs(t, cw, col_off):
    per = SEG // HALO
    last_blk = t // HALO - 1
    return [
        pl.BlockSpec((HALO, cw), lambda i, j, f, l: (jnp.maximum(i * per - 1, 0), col_off + j)),
        pl.BlockSpec((SEG, cw), lambda i, j, f, l: (i, col_off + j)),
        pl.BlockSpec((HALO, cw), lambda i, j, f, l: (jnp.minimum((i + 1) * per, last_blk), col_off + j)),
    ]


def _conv_call(body, name, meta, t, cw, n_col, in_specs, args, out_cols, scratch):
    nseg = t // SEG
    return pl.pallas_call(
        body,
        grid_spec=pltpu.PrefetchScalarGridSpec(
            num_scalar_prefetch=2, grid=(nseg, n_col), in_specs=in_specs,
            out_specs=pl.BlockSpec((SEG, cw), lambda i, j, f, l: (i, j)),
            scratch_shapes=scratch),
        out_shape=jax.ShapeDtypeStruct((t, out_cols), BF16),
        compiler_params=_cparams(2),
        name=name,
    )(meta["first"], meta["last"], *args)


def _colvec_spec(rows, cw):
    return pl.BlockSpec((rows, cw), lambda i, j, f, l: (0, j))


def _ssd_conv(main, meta, conv_w, conv_b, col0, width):
    t = main.shape[0]
    cw = 512
    in_specs = _halo_specs(t, cw, col0 // cw) + [_colvec_spec(SSD_CONV, cw), _colvec_spec(1, cw)]
    return _conv_call(_ssdconv_body, "ssd_conv_silu", meta, t, cw, width // cw, in_specs,
                      (main, main, main, conv_w, conv_b.reshape(1, width)), width,
                      [pltpu.VMEM((SEG + 2 * HALO, cw), F32)])


def _short_conv(main, meta, sconv_w, col_gb, col_gc, col_xc, width):
    t = main.shape[0]
    cw = 512
    in_specs = (_halo_specs(t, cw, col_gc // cw) + _halo_specs(t, cw, col_xc // cw)
                + [pl.BlockSpec((SEG, cw), lambda i, j, f, l: (i, col_gb // cw + j)),
                   _colvec_spec(SCONV_K, cw)])
    return _conv_call(_sconv_body, "short_gated_conv", meta, t, cw, width // cw, in_specs,
                      (main,) * 7 + (sconv_w,), width, [pltpu.VMEM((SEG + 2 * HALO, cw), F32)])


def _conformer_conv(glu, meta, dw_w, dw_b, ln_w, ln_b):
    t, two_w = glu.shape
    w = two_w // 2
    in_specs = (_halo_specs(t, w, 0) + _halo_specs(t, w, 1)
                + [_colvec_spec(CONF_K, w)] + [_colvec_spec(1, w)] * 3)
    return _conv_call(_conf_body, "conformer_conv", meta, t, w, 1, in_specs,
                      (glu,) * 6 + (dw_w, dw_b.reshape(1, w), ln_w.reshape(1, w), ln_b.reshape(1, w)),
                      w, [pltpu.VMEM((SEG + 2 * HALO, w), F32), pltpu.VMEM((SEG, w), F32)])


def _softplus(x):
    return jnp.maximum(x, 0.0) + jnp.log1p(jnp.exp(-jnp.abs(x)))


def _ssd_body(*refs, nc, has_h0, want_state):
    it = iter(refs)
    x_ref, b_ref, c_ref, dt_ref, bias_ref, alog_ref, dsk_ref = (next(it) for _ in range(7))
    h0_ref = next(it) if has_h0 else None
    y_alias = next(it) if not want_state else None
    del y_alias
    y_ref = next(it)
    st_ref = next(it) if want_state else None
    ybuf, sf_scr, sb_scr = next(it), next(it), next(it)

    q = SSD_CHUNK
    hp = SSD_HEADS_PER_GROUP
    width = hp * SSD_HEAD_DIM
    row = lax.broadcasted_iota(jnp.int32, (q, q), 0)
    col = lax.broadcasted_iota(jnp.int32, (q, q), 1)
    lower = row >= col
    upper = col >= row
    lower_f = lower.astype(F32)
    lane_w = lax.broadcasted_iota(jnp.int32, (q, width), 1) // SSD_HEAD_DIM
    sub_w = lax.broadcasted_iota(jnp.int32, (width, 1), 0) // SSD_HEAD_DIM

    def per_head_cols(mat, base):
        out = mat[:, base + hp - 1:base + hp]
        for r in range(hp - 2, -1, -1):
            out = jnp.where(lane_w == r, mat[:, base + r:base + r + 1], out)
        return out

    def per_head_rows(rowvec, base):
        out = rowvec[:, base + hp - 1:base + hp]
        for r in range(hp - 2, -1, -1):
            out = jnp.where(sub_w == r, rowvec[:, base + r:base + r + 1], out)
        return out

    a_row = -jnp.exp(alog_ref[0])
    bias_row = bias_ref[0]

    def chunk_terms(c):
        r0 = pl.multiple_of(c * q, q)
        rows = pl.ds(r0, q)
        dt = _softplus(dt_ref[rows, :] + bias_row)
        a = dt * a_row
        cs = jnp.dot(lower_f, a, precision=HIGHEST, preferred_element_type=F32)
        return rows, dt, a, cs

    if has_h0:
        sf_scr[...] = h0_ref[0].reshape(width, SSD_STATE)
        sb_scr[...] = h0_ref[1].reshape(width, SSD_STATE)
    else:
        sf_scr[...] = jnp.zeros_like(sf_scr)
        sb_scr[...] = jnp.zeros_like(sb_scr)

    def bwd(i, carry):
        rows, dt, a, cs = chunk_terms(nc - 1 - i)
        ex = cs - a
        tot = cs[q - 1:q, :]
        x = x_ref[rows, :].astype(F32)
        s_prev = sb_scr[...]
        y_off = lax.dot_general(c_ref[rows, :], s_prev.astype(BF16), NT_DIMS, preferred_element_type=F32)
        ybuf[rows, :] = y_off * per_head_cols(jnp.exp(tot - ex), hp)
        xs = (x * per_head_cols(jnp.exp(ex) * dt, hp)).astype(BF16)
        contrib = lax.dot_general(xs, b_ref[rows, :], TN_DIMS, preferred_element_type=F32)
        sb_scr[...] = s_prev * per_head_rows(jnp.exp(tot), hp) + contrib
        return carry

    lax.fori_loop(0, nc, bwd, 0)

    dsk = dsk_ref[...]

    def fwd(c, carry):
        rows, dt, a, cs = chunk_terms(c)
        ex = cs - a
        tot = cs[q - 1:q, :]
        xb = x_ref[rows, :]
        x = xb.astype(F32)
        bm = b_ref[rows, :]
        cm = c_ref[rows, :]
        cb = lax.dot_general(cm, bm, NT_DIMS, preferred_element_type=F32)
        cs_t, ex_t, dt_t = cs.T, ex.T, dt.T
        y = ybuf[rows, :] + x * dsk
        for r in range(hp):
            dec_f = jnp.exp(jnp.where(lower, cs[:, r:r + 1] - cs_t[r:r + 1, :], NEG_BIG))
            dec_b = jnp.exp(jnp.where(upper, ex_t[hp + r:hp + r + 1, :] - ex[:, hp + r:hp + r + 1], NEG_BIG))
            wm = cb * (dec_f * dt_t[r:r + 1, :] + dec_b * dt_t[hp + r:hp + r + 1, :])
            yd = jnp.dot(wm.astype(BF16), xb, preferred_element_type=F32)
            y = y + jnp.where(lane_w == r, yd, 0.0)
        s_prev = sf_scr[...]
        y_off = lax.dot_general(cm, s_prev.astype(BF16), NT_DIMS, preferred_element_type=F32)
        y = y + y_off * per_head_cols(jnp.exp(cs), 0)
        y_ref[rows, :] = y.astype(y_ref.dtype)
        xs = (x * per_head_cols(jnp.exp(tot - cs) * dt, 0)).astype(BF16)
        contrib = lax.dot_general(xs, bm, TN_DIMS, preferred_element_type=F32)
        sf_scr[...] = s_prev * per_head_rows(jnp.exp(tot), 0) + contrib
        return carry

    lax.fori_loop(0, nc, fwd, 0)

    if want_state:
        st_ref[0] = sf_scr[...].reshape(hp, SSD_HEAD_DIM, SSD_STATE)
        st_ref[1] = sb_scr[...].reshape(hp, SSD_HEAD_DIM, SSD_STATE)


def _ssd_call(xbc, dtp, bias, alog, dsk, seq_len, n_seq, row_blk0, h0, layer_idx, y_prev):
    t = xbc.shape[0]
    inner = SSD_GROUPS * SSD_HEADS_PER_GROUP * SSD_HEAD_DIM
    width = SSD_HEADS_PER_GROUP * SSD_HEAD_DIM
    nb = inner // SSD_STATE
    want_state = h0 is None
    in_specs = [
        pl.BlockSpec((seq_len, width), lambda s, g: (row_blk0 + s, g)),
        pl.BlockSpec((seq_len, SSD_STATE), lambda s, g: (row_blk0 + s, nb + g)),
        pl.BlockSpec((seq_len, SSD_STATE), lambda s, g: (row_blk0 + s, nb + SSD_GROUPS + g)),
        pl.BlockSpec((seq_len, LANES), lambda s, g: (row_blk0 + s, g)),
        pl.BlockSpec((1, 1, LANES), lambda s, g: (g, 0, 0)),
        pl.BlockSpec((1, 1, LANES), lambda s, g: (g, 0, 0)),
        pl.BlockSpec((1, width), lambda s, g: (0, g)),
    ]
    args = [xbc, xbc, xbc, dtp, bias, alog, dsk]
    aliases = {}
    if not want_state:
        in_specs.append(pl.BlockSpec((None, None, 2, SSD_HEADS_PER_GROUP, SSD_HEAD_DIM, SSD_STATE),
                                     lambda s, g: (s, layer_idx, 0, g, 0, 0)))
        args.append(h0)
        in_specs.append(pl.BlockSpec(memory_space=pl.ANY))
        args.append(y_prev)
        aliases = {len(args) - 1: 0}
    out_specs = [pl.BlockSpec((seq_len, width), lambda s, g: (row_blk0 + s, g))]
    out_shape = [jax.ShapeDtypeStruct((t, inner), BF16)]
    if want_state:
        out_specs.append(pl.BlockSpec((None, 2, SSD_HEADS_PER_GROUP, SSD_HEAD_DIM, SSD_STATE),
                                      lambda s, g: (s, 0, g, 0, 0)))
        out_shape.append(jax.ShapeDtypeStruct(
            (n_seq, 2, SSD_GROUPS * SSD_HEADS_PER_GROUP, SSD_HEAD_DIM, SSD_STATE), F32))
    return pl.pallas_call(
        functools.partial(_ssd_body, nc=seq_len // SSD_CHUNK, has_h0=not want_state, want_state=want_state),
        grid=(n_seq, SSD_GROUPS),
        in_specs=in_specs, out_specs=out_specs, out_shape=out_shape,
        scratch_shapes=[pltpu.VMEM((seq_len, width), F32),
                        pltpu.VMEM((width, SSD_STATE), F32), pltpu.VMEM((width, SSD_STATE), F32)],
        input_output_aliases=aliases,
        compiler_params=_cparams(2),
        name="ssd_scan_ctx" if want_state else "ssd_scan_latent",
    )(*args)


def _outproj_body(mrow_ref, *refs, gated_norm):
    del mrow_ref
    if gated_norm:
        y_ref, z_ref, nw_ref, b_ref, w_ref, x_ref, g_ref, o_ref = refs
        y = y_ref[...].astype(F32) * _silu(z_ref[...].astype(F32))
        ms = jnp.mean(y * y, axis=-1, keepdims=True)
        a = (y * lax.rsqrt(ms + NORM_EPS) * nw_ref[...]).astype(BF16)
    else:
        a_ref, b_ref, w_ref, x_ref, g_ref, o_ref = refs
        a = a_ref[...]
    d = a.shape[1]
    out = jnp.dot(a, w_ref[0:d, :], preferred_element_type=F32)
    out = out + jnp.dot(b_ref[...], w_ref[d:, :], preferred_element_type=F32)
    o_ref[...] = x_ref[...] + g_ref[0] * out


def _out_proj(a, b, w, x, mrow, mods, gate_chunk, z_src=None, norm_w=None):
    t, d = x.shape
    nseg = t // SEG
    row = lambda i, mrow: (i, 0)
    in_specs = [pl.BlockSpec((SEG, d), row)]
    args = [a]
    if z_src is not None:
        in_specs += [pl.BlockSpec((SEG, d), row), pl.BlockSpec((1, d), lambda i, mrow: (0, 0))]
        args += [z_src, norm_w.reshape(1, d)]
    in_specs += [pl.BlockSpec((SEG, d), row), _resident(w.shape), pl.BlockSpec((SEG, d), row),
                 _mod_spec(d, gate_chunk)]
    args += [b, w, x, mods]
    return pl.pallas_call(
        functools.partial(_outproj_body, gated_norm=z_src is not None),
        grid_spec=pltpu.PrefetchScalarGridSpec(
            num_scalar_prefetch=1, grid=(nseg,), in_specs=in_specs,
            out_specs=pl.BlockSpec((SEG, d), row)),
        out_shape=jax.ShapeDtypeStruct((t, d), F32),
        compiler_params=_cparams(1),
        name="mixer_out_proj",
    )(mrow, *args)


def _qkprep_body(lat_ref, pos_ref, q_ref, kv_ref, qw_ref, kw_ref, cos_ref, sin_ref,
                 qo_ref, ko_ref, vo_ref, kn_ref, vn_ref):
    del pos_ref
    is_lat = lat_ref[pl.program_id(0)] > 0
    r = lax.broadcasted_iota(jnp.int32, (LANES, LANES), 0) // HEAD_DIM
    c = lax.broadcasted_iota(jnp.int32, (LANES, LANES), 1) // HEAD_DIM
    head_mean = jnp.where(r == c, 1.0 / HEAD_DIM, 0.0).astype(F32)
    lane = lax.broadcasted_iota(jnp.int32, (SEG, LANES), 1)
    first_half = (lane % HEAD_DIM) < (HEAD_DIM // 2)
    cos = cos_ref[...]
    sin = sin_ref[...]

    def norm_rope(xs, w):
        ms = jnp.dot(xs * xs, head_mean, precision=HIGHEST, preferred_element_type=F32)
        xn = xs * lax.rsqrt(ms + NORM_EPS) * w
        rot = jnp.where(first_half, pltpu.roll(xn, LANES - HEAD_DIM // 2, 1),
                        pltpu.roll(xn, HEAD_DIM // 2, 1))
        return xn, jnp.where(is_lat, xn * cos + rot * sin, xn)

    scale = HEAD_DIM ** -0.5
    for c0 in range(0, q_ref.shape[1], LANES):
        _, qr = norm_rope(q_ref[:, c0:c0 + LANES].astype(F32), qw_ref[...])
        qo_ref[:, c0:c0 + LANES] = (qr * scale).astype(qo_ref.dtype)
    kvw = ko_ref.shape[1]
    for c0 in range(0, kvw, LANES):
        kn, kr = norm_rope(kv_ref[:, c0:c0 + LANES], kw_ref[...])
        kn_ref[:, c0:c0 + LANES] = kn
        ko_ref[:, c0:c0 + LANES] = kr.astype(ko_ref.dtype)
    v = kv_ref[:, kvw:]
    vn_ref[...] = v
    vo_ref[...] = v.astype(vo_ref.dtype)


def _qk_prep(q, kv, meta, q_norm_w, k_norm_w, cos_t, sin_t):
    t, qd = q.shape
    kvd = kv.shape[1] // 2
    nseg = t // SEG
    row = lambda i, lat, pos: (i, 0)
    tab = lambda i, lat, pos: (pos[i] * lat[i], 0)
    tile2 = lambda w: jnp.tile(w, LANES // HEAD_DIM).reshape(1, LANES)
    return pl.pallas_call(
        _qkprep_body,
        grid_spec=pltpu.PrefetchScalarGridSpec(
            num_scalar_prefetch=2, grid=(nseg,),
            in_specs=[pl.BlockSpec((SEG, qd), row), pl.BlockSpec((SEG, 2 * kvd), row),
                      pl.BlockSpec((1, LANES), lambda i, lat, pos: (0, 0)),
                      pl.BlockSpec((1, LANES), lambda i, lat, pos: (0, 0)),
                      pl.BlockSpec((SEG, LANES), tab), pl.BlockSpec((SEG, LANES), tab)],
            out_specs=[pl.BlockSpec((SEG, qd), row), pl.BlockSpec((SEG, kvd), row),
                       pl.BlockSpec((SEG, kvd), row), pl.BlockSpec((SEG, kvd), row),
                       pl.BlockSpec((SEG, kvd), row)]),
        out_shape=[jax.ShapeDtypeStruct((t, qd), BF16), jax.ShapeDtypeStruct((t, kvd), BF16),
                   jax.ShapeDtypeStruct((t, kvd), BF16), jax.ShapeDtypeStruct((t, kvd), F32),
                   jax.ShapeDtypeStruct((t, kvd), F32)],
        compiler_params=_cparams(1),
        name="qk_norm_rope",
    )(meta["lat"], meta["pos"], q, kv, tile2(q_norm_w), tile2(k_norm_w), cos_t, sin_t)


def _rope_tables(seq_len):
    rows = seq_len // GRID_W
    rowp = jnp.repeat(jnp.arange(rows), GRID_W).astype(F32)
    colp = jnp.tile(jnp.arange(GRID_W), rows).astype(F32)
    axis = HEAD_DIM // 2
    inv = ROPE_THETA ** (-jnp.arange(0, axis, 2, dtype=F32) / axis)
    ang = jnp.concatenate([rowp[:, None] * inv, colp[:, None] * inv], axis=-1)
    cos, sin = jnp.cos(ang), jnp.sin(ang)
    cos_h = jnp.concatenate([cos, cos], axis=-1)
    sin_h = jnp.concatenate([-sin, sin], axis=-1)
    rep = LANES // HEAD_DIM
    return jnp.tile(cos_h, (1, rep)), jnp.tile(sin_h, (1, rep))


def _attn_body(*refs, src_rows, tq, tk, aliased):
    n_src = len(src_rows)
    q_ref = refs[0]
    kv_refs = refs[1:1 + 2 * n_src]
    pos = 1 + 2 * n_src + (1 if aliased else 0)
    o_ref = refs[pos]
    qs_scr, acc_scr, m_scr, l_scr = refs[pos + 1:]
    kvw = KV_HEADS * HEAD_DIM
    lane_g = lax.broadcasted_iota(jnp.int32, (tq, kvw), 1) // HEAD_DIM

    for g in range(KV_HEADS):
        for r in range(Q_PER_KV):
            qr = q_ref[:, kvw * r:kvw * (r + 1)]
            qs_scr[g, r * tq:(r + 1) * tq, :] = jnp.where(lane_g == g, qr, jnp.zeros_like(qr))
    m_scr[...] = jnp.full_like(m_scr, NEG_BIG)
    l_scr[...] = jnp.zeros_like(l_scr)
    acc_scr[...] = jnp.zeros_like(acc_scr)

    def step(kc, vc):
        for g in range(KV_HEADS):
            s = lax.dot_general(qs_scr[g], kc, NT_DIMS, preferred_element_type=F32)
            m_prev = m_scr[g]
            m_new = jnp.maximum(m_prev, jnp.max(s, axis=1, keepdims=True))
            p = jnp.exp(s - m_new)
            alpha = jnp.exp(m_prev - m_new)
            l_scr[g] = alpha * l_scr[g] + jnp.sum(p, axis=1, keepdims=True)
            m_scr[g] = m_new
            acc_scr[g] = acc_scr[g] * alpha + jnp.dot(p.astype(BF16), vc, preferred_element_type=F32)

    for si, n_rows in enumerate(src_rows):
        k_ref, v_ref = kv_refs[2 * si], kv_refs[2 * si + 1]
        if n_rows <= tk:
            step(k_ref[...], v_ref[...])
        else:
            def body(j, carry, k_ref=k_ref, v_ref=v_ref):
                rows = pl.ds(pl.multiple_of(j * tk, tk), tk)
                step(k_ref[rows, :], v_ref[rows, :])
                return carry
            lax.fori_loop(0, n_rows // tk, body, 0)

    for r in range(Q_PER_KV):
        out = None
        for g in range(KV_HEADS):
            rows = slice(r * tq, (r + 1) * tq)
            a = acc_scr[g, rows, :] / l_scr[g, rows, :]
            out = a if out is None else jnp.where(lane_g == g, a, out)
        o_ref[:, kvw * r:kvw * (r + 1)] = out.astype(o_ref.dtype)


def _attn_call(q, srcs, q_blk0, n_seq, q_len, att_prev, name):
    t, qd = q.shape
    tq = SEG
    kvw = KV_HEADS * HEAD_DIM
    nq = q_len // tq
    qmap = lambda s, j: (q_blk0 + s * nq + j, 0)
    in_specs = [pl.BlockSpec((tq, qd), qmap)]
    args = [q]
    for k, v, rows, blk0 in srcs:
        spec = pl.BlockSpec((rows, kvw), lambda s, j, blk0=blk0: (blk0 + s, 0))
        in_specs += [spec, spec]
        args += [k, v]
    aliases = {}
    if att_prev is not None:
        in_specs.append(pl.BlockSpec(memory_space=pl.ANY))
        args.append(att_prev)
        aliases = {len(args) - 1: 0}
    rows4 = Q_PER_KV * tq
    return pl.pallas_call(
        functools.partial(_attn_body, src_rows=tuple(s[2] for s in srcs), tq=tq, tk=ATT_TK,
                          aliased=att_prev is not None),
        grid=(n_seq, nq),
        in_specs=in_specs,
        out_specs=pl.BlockSpec((tq, qd), qmap),
        out_shape=jax.ShapeDtypeStruct((t, qd), BF16),
        scratch_shapes=[pltpu.VMEM((KV_HEADS, rows4, kvw), BF16),
                        pltpu.VMEM((KV_HEADS, rows4, kvw), F32),
                        pltpu.VMEM((KV_HEADS, rows4, 1), F32),
                        pltpu.VMEM((KV_HEADS, rows4, 1), F32)],
        input_output_aliases=aliases,
        compiler_params=_cparams(2),
        name=name,
    )(*args)


def _ffnpre_body(mrow_ref, x_ref, nw_ref, sh_ref, sc_ref, rw_ref, rb_ref, h_ref, ti_ref, gt_ref):
    del mrow_ref
    h = _normed(x_ref[...], nw_ref, sh_ref, sc_ref)
    hb = h.astype(BF16)
    h_ref[...] = hb
    logits = jnp.dot(hb, rw_ref[...], preferred_element_type=F32) + rb_ref[...]
    lane = lax.broadcasted_iota(jnp.int32, logits.shape, 1)
    work = logits
    vals, idxs = [], []
    for _ in range(TOP_K):
        m = jnp.max(work, axis=1, keepdims=True)
        idx = jnp.min(jnp.where(work == m, lane, LANES), axis=1, keepdims=True)
        vals.append(m)
        idxs.append(idx)
        work = jnp.where(lane == idx, NEG_BIG, work)
    es = [jnp.exp(v - vals[0]) for v in vals]
    den = es[0]
    for e in es[1:]:
        den = den + e
    ti = jnp.zeros(logits.shape, jnp.int32)
    gt = jnp.zeros(logits.shape, F32)
    for k in range(TOP_K):
        ti = jnp.where(lane == k, idxs[k], ti)
        gt = jnp.where(lane == k, es[k] / den, gt)
    ti_ref[...] = ti
    gt_ref[...] = gt


def _ffn_pre(x, mrow, norm_w, mods, router_w, router_b):
    t, d = x.shape
    nseg = t // SEG
    ne = router_w.shape[1]
    rw = jnp.zeros((d, LANES), F32).at[:, :ne].set(router_w).astype(BF16)
    rb = jnp.full((1, LANES), NEG_BIG, F32).at[0, :ne].set(router_b)
    row = lambda i, mrow: (i, 0)
    const = lambda i, mrow: (0, 0)
    return pl.pallas_call(
        _ffnpre_body,
        grid_spec=pltpu.PrefetchScalarGridSpec(
            num_scalar_prefetch=1, grid=(nseg,),
            in_specs=[pl.BlockSpec((SEG, d), row), pl.BlockSpec((1, d), const),
                      _mod_spec(d, 3), _mod_spec(d, 4),
                      pl.BlockSpec((d, LANES), const), pl.BlockSpec((1, LANES), const)],
            out_specs=[pl.BlockSpec((SEG, d), row), pl.BlockSpec((SEG, LANES), row),
                       pl.BlockSpec((SEG, LANES), row)]),
        out_shape=[jax.ShapeDtypeStruct((t, d), BF16), jax.ShapeDtypeStruct((t, LANES), jnp.int32),
                   jax.ShapeDtypeStruct((t, LANES), F32)],
        compiler_params=_cparams(1),
        name="ffn_norm_router_topk",
    )(mrow, x, norm_w.reshape(1, d), mods, mods, rw, rb)


def _experts_body(be_ref, nu_ref, x_ref, wgu_ref, bgu_ref, wdn_ref, bdn_ref, o_ref):
    del be_ref
    i = pl.program_id(0)

    @pl.when(i < nu_ref[0])
    def _():
        hg = jnp.dot(x_ref[...], wgu_ref[...], preferred_element_type=F32) + bgu_ref[...]
        dff = hg.shape[1] // 2
        g = jnp.minimum(hg[:, :dff], SWIGLU_LIMIT)
        u = jnp.clip(hg[:, dff:], -SWIGLU_LIMIT, SWIGLU_LIMIT)
        act = g * jax.nn.sigmoid(SWIGLU_ALPHA * g) * (u + 1.0)
        y = jnp.dot(act.astype(BF16), wdn_ref[...], preferred_element_type=F32) + bdn_ref[...]
        o_ref[...] = y.astype(o_ref.dtype)

    @pl.when(i >= nu_ref[0])
    def _():
        o_ref[...] = jnp.zeros_like(o_ref)


def _experts(xg, block_e, n_used, w_gu, b_gu, w_dn, b_dn):
    n_slots, d = xg.shape
    n_blocks = n_slots // MOE_ROWS
    ne, _, two_f = w_gu.shape
    dff = two_f // 2
    return pl.pallas_call(
        _experts_body,
        grid_spec=pltpu.PrefetchScalarGridSpec(
            num_scalar_prefetch=2, grid=(n_blocks,),
            in_specs=[pl.BlockSpec((MOE_ROWS, d), lambda i, be, nu: (i, 0)),
                      pl.BlockSpec((None, d, two_f), lambda i, be, nu: (be[i], 0, 0)),
                      pl.BlockSpec((None, 1, two_f), lambda i, be, nu: (be[i], 0, 0)),
                      pl.BlockSpec((None, dff, d), lambda i, be, nu: (be[i], 0, 0)),
                      pl.BlockSpec((None, 1, d), lambda i, be, nu: (be[i], 0, 0))],
            out_specs=pl.BlockSpec((MOE_ROWS, d), lambda i, be, nu: (i, 0))),
        out_shape=jax.ShapeDtypeStruct((n_slots, d), BF16),
        compiler_params=_cparams(1),
        name="expert_swiglu",
    )(block_e, n_used, xg, w_gu, b_gu.reshape(ne, 1, two_f), w_dn, b_dn.reshape(ne, 1, d))


def _moe(x, h2, top_i, gates, g2_rows, w_gu, b_gu, w_dn, b_dn):
    t, d = x.shape
    tk = t * TOP_K
    bm = MOE_ROWS
    flat_e = top_i.reshape(tk)
    order = jnp.argsort(flat_e).astype(jnp.int32)
    sorted_e = flat_e[order]
    counts = jnp.bincount(flat_e, length=N_EXPERTS).astype(jnp.int32)
    padded = (counts + bm - 1) // bm * bm
    pad_end = jnp.cumsum(padded)
    pad_start = pad_end - padded
    start = jnp.cumsum(counts) - counts
    dest = pad_start[sorted_e] + jnp.arange(tk, dtype=jnp.int32) - start[sorted_e]
    n_blocks = (tk + N_EXPERTS * (bm - 1) + bm - 1) // bm
    slot_tok = jnp.zeros((n_blocks * bm,), jnp.int32).at[dest].set(order // TOP_K)
    block_e = jnp.minimum(jnp.searchsorted(pad_end, jnp.arange(n_blocks, dtype=jnp.int32) * bm,
                                           side="right"), N_EXPERTS - 1).astype(jnp.int32)
    n_used = (pad_end[-1:] // bm).astype(jnp.int32)
    xg = h2[slot_tok]
    yb = _experts(xg, block_e, n_used, w_gu, b_gu, w_dn, b_dn)
    dest_a = jnp.zeros((tk,), jnp.int32).at[order].set(dest).reshape(t, TOP_K)
    f = jnp.einsum("tkd,tk->td", yb[dest_a].astype(F32), gates)
    return x + g2_rows * f


def _final_body(x_ref, w_ref, o_ref):
    x = x_ref[...]
    ms = jnp.mean(x * x, axis=-1, keepdims=True)
    o_ref[...] = x * lax.rsqrt(ms + NORM_EPS) * w_ref[...]


def _final_norm(x, w):
    t, d = x.shape
    return pl.pallas_call(
        _final_body,
        grid=(t // SEG,),
        in_specs=[pl.BlockSpec((SEG, d), lambda i: (i, 0)), pl.BlockSpec((1, d), lambda i: (0, 0))],
        out_specs=pl.BlockSpec((SEG, d), lambda i: (i, 0)),
        out_shape=jax.ShapeDtypeStruct((t, d), F32),
        compiler_params=_cparams(1),
        name="final_rmsnorm",
    )(x, w.reshape(1, d))


def _segment_meta(n_ctx, sp, n_lat, ss):
    mrow, first, last, pos, lat = [], [], [], [], []
    for n_seq, length, is_lat in ((n_ctx, sp, 0), (n_lat, ss, 1)):
        per = length // SEG
        for s in range(n_seq):
            for j in range(per):
                mrow.append(1 + s if is_lat else 0)
                first.append(int(j == 0))
                last.append(int(j == per - 1))
                pos.append(j)
                lat.append(is_lat)
    as_arr = lambda v: jnp.asarray(np.asarray(v, np.int32))
    return dict(mrow=as_arr(mrow), first=as_arr(first), last=as_arr(last), pos=as_arr(pos),
                lat=as_arr(lat), mrow_np=np.asarray(mrow, np.int32))


def kernel(x_prompt, x_sample, state_ssd, cache_k, cache_v, c, c_ctx, w_mod, b_mod, norm_mix_w, norm_ffn_w, ev_w_in, ev_conv_w, ev_conv_b, ev_dt_bias, ev_a_log, ev_d_skip, ev_norm_w, ev_sconv_w, ev_w_out, od_w_in, od_q_norm_w, od_k_norm_w, od_dw_w, od_dw_b, od_ln_w, od_ln_b, od_w_out, router_w, router_b, w_gu, b_gu, w_dn, b_dn, final_norm_w):
    bp, sp, d = x_prompt.shape
    bs, ss, _ = x_sample.shape
    depth = w_mod.shape[0]
    n_ctx_tok = bp * sp
    t = n_ctx_tok + bs * ss
    past = cache_k.shape[2]
    kvw = KV_HEADS * HEAD_DIM
    assert sp % SEG == 0 and ss % SEG == 0 and n_ctx_tok % ss == 0 and past % SEG == 0
    assert sp % SSD_CHUNK == 0 and ss % ATT_TK == 0 and d == SSD_GROUPS * SSD_HEADS_PER_GROUP * SSD_HEAD_DIM

    meta = _segment_meta(bp, sp, bs, ss)
    mrow = meta["mrow"]
    x = jnp.concatenate([x_prompt.reshape(n_ctx_tok, d), x_sample.reshape(bs * ss, d)], axis=0)

    n_rows = 16
    cvec = jnp.zeros((n_rows, d), F32).at[0].set(c_ctx).at[1:1 + bs].set(c)
    mods_all = _modulation(cvec, w_mod, b_mod)
    cos_t, sin_t = _rope_tables(ss)

    n_heads = SSD_GROUPS * SSD_HEADS_PER_GROUP
    conv_dim = d + 2 * SSD_GROUPS * SSD_STATE
    states, ctx_k, ctx_v = [], [], []
    for l in range(depth):
        mods = mods_all[l].reshape(n_rows, 1, 6 * d)
        i = l // 2
        if l % 2 == 0:
            w_in = ev_w_in[i]
            o_dt = d + conv_dim
            w_main = jnp.concatenate([w_in[:, :o_dt], w_in[:, o_dt + 2 * n_heads:]], axis=1).astype(BF16)
            regroup = lambda v: v.reshape(v.shape[:-1] + (2, SSD_GROUPS, SSD_HEADS_PER_GROUP)).swapaxes(-3, -2)
            pad_lanes = lambda v: jnp.zeros(v.shape[:-3] + (SSD_GROUPS, LANES), F32).at[..., :2 * SSD_HEADS_PER_GROUP].set(
                v.reshape(v.shape[:-3] + (SSD_GROUPS, 2 * SSD_HEADS_PER_GROUP)))
            w_dt = pad_lanes(regroup(w_in[:, o_dt:o_dt + 2 * n_heads])).reshape(d, SSD_GROUPS * LANES).astype(BF16)
            bias = pad_lanes(regroup(ev_dt_bias[i].reshape(2 * n_heads))).reshape(SSD_GROUPS, 1, LANES)
            alog = pad_lanes(regroup(ev_a_log[i].reshape(2 * n_heads))).reshape(SSD_GROUPS, 1, LANES)
            dsk = jnp.repeat(ev_d_skip[i], SSD_HEAD_DIM).reshape(1, d)

            main, dtp = _fused_proj(x, mrow, norm_mix_w[l], mods, 0, [w_main, w_dt], [BF16, F32])
            xbc = _ssd_conv(main, meta, ev_conv_w[i], ev_conv_b[i], d, conv_dim)
            y, st = _ssd_call(xbc, dtp, bias, alog, dsk, sp, bp, 0, None, i, None)
            (y,) = _ssd_call(xbc, dtp, bias, alog, dsk, ss, bs, n_ctx_tok // ss, state_ssd, i, y)
            states.append(st)
            yc = _short_conv(main, meta, ev_sconv_w[i], o_dt, o_dt + d, o_dt + 2 * d, d)
            x = _out_proj(y, yc, ev_w_out[i].astype(BF16), x, mrow, mods, 2, z_src=main, norm_w=ev_norm_w[i])
        else:
            w_in = od_w_in[i]
            wq = w_in[:, :d].reshape(d, KV_HEADS, Q_PER_KV, HEAD_DIM).swapaxes(1, 2).reshape(d, d).astype(BF16)
            wkv = w_in[:, d:d + 2 * kvw].astype(BF16)
            wglu = w_in[:, d + 2 * kvw:].astype(BF16)
            w_out = od_w_out[i]
            w_att = w_out[:d].reshape(KV_HEADS, Q_PER_KV, HEAD_DIM, d).swapaxes(0, 1).reshape(d, d)
            w_out_p = jnp.concatenate([w_att, w_out[d:]], axis=0).astype(BF16)

            q, kv, glu = _fused_proj(x, mrow, norm_mix_w[l], mods, 0, [wq, wkv, wglu], [BF16, F32, BF16])
            qn, k_att, v_att, k_n, v_n = _qk_prep(q, kv, meta, od_q_norm_w[i], od_k_norm_w[i], cos_t, sin_t)
            ctx_k.append(k_n[:n_ctx_tok].reshape(bp, sp, KV_HEADS, HEAD_DIM))
            ctx_v.append(v_n[:n_ctx_tok].reshape(bp, sp, KV_HEADS, HEAD_DIM))
            ck = cache_k[:, i].reshape(bs * past, kvw).astype(BF16)
            cv = cache_v[:, i].reshape(bs * past, kvw).astype(BF16)
            att = _attn_call(qn, [(k_att, v_att, sp, 0)], 0, bp, sp, None, "attention_ctx")
            att = _attn_call(qn, [(ck, cv, past, 0), (k_att, v_att, ss, n_ctx_tok // ss)],
                             n_ctx_tok // SEG, bs, ss, att, "attention_latent")
            u = _conformer_conv(glu, meta, od_dw_w[i], od_dw_b[i], od_ln_w[i], od_ln_b[i])
            x = _out_proj(att, u, w_out_p, x, mrow, mods, 2)

        h2, ti, gt = _ffn_pre(x, mrow, norm_ffn_w[l], mods, router_w[l], router_b[l])
        g2_rows = jnp.repeat(mods_all[l][meta["mrow_np"], 5 * d:], SEG, axis=0)
        x = _moe(x, h2, ti[:, :TOP_K], gt[:, :TOP_K], g2_rows,
                 w_gu[l].astype(BF16), b_gu[l], w_dn[l].astype(BF16), b_dn[l])

    y = _final_norm(x, final_norm_w)
    y_prompt = y[:n_ctx_tok].reshape(bp, sp, d)
    y_sample = y[n_ctx_tok:].reshape(bs, ss, d)
    return (y_prompt, y_sample, jnp.stack(states, axis=1), jnp.stack(ctx_k, axis=1), jnp.stack(ctx_v, axis=1))
```

```python
import functools

import numpy as np
import jax
import jax.numpy as jnp
from jax import lax
from jax.experimental import pallas as pl
from jax.experimental.pallas import tpu as pltpu

F32 = jnp.float32
BF16 = jnp.bfloat16
HIGHEST = lax.Precision.HIGHEST

NORM_EPS = 1e-6
GRID_W = 64
ROPE_THETA = 10000.0
HEAD_DIM = 64
KV_HEADS = 4
Q_PER_KV = 4
SSD_HEAD_DIM = 64
SSD_GROUPS = 4
SSD_HEADS_PER_GROUP = 4
SSD_STATE = 128
SSD_CHUNK = 128
SSD_CONV = 4
SCONV_K = 3
CONF_K = 31
N_EXPERTS = 32
TOP_K = 4
SWIGLU_LIMIT = 7.0
SWIGLU_ALPHA = 1.702

SEG = 256
HALO = 16
LANES = 128
MOE_ROWS = 256
ATT_TK = 512
NEG_BIG = -1e30
LOG2_E = 1.4426950408889634
VMEM_LIMIT = 56 * 1024 * 1024

NT_DIMS = (((1,), (1,)), ((), ()))
TN_DIMS = (((0,), (0,)), ((), ()))


def _cparams(n_grid):
    return pltpu.CompilerParams(dimension_semantics=("arbitrary",) * n_grid,
                                vmem_limit_bytes=VMEM_LIMIT)


def _resident(shape):
    nd = len(shape)
    return pl.BlockSpec(shape, lambda *_: (0,) * nd, pipeline_mode=pl.Buffered(1))


def _silu(x):
    return x * jax.nn.sigmoid(x)


def _mod_body(c_ref, w_ref, b_ref, o_ref):
    c = c_ref[...]
    s = _silu(c).astype(BF16)
    o_ref[0] = jnp.dot(s, w_ref[0].astype(BF16), preferred_element_type=F32) + b_ref[0]


def _modulation(cvec, w_mod, b_mod):
    depth, d, n = w_mod.shape
    rows = cvec.shape[0]
    tn = 1536
    assert n % tn == 0
    return pl.pallas_call(
        _mod_body,
        grid=(depth, n // tn),
        in_specs=[pl.BlockSpec((rows, d), lambda l, j: (0, 0)),
                  pl.BlockSpec((1, d, tn), lambda l, j: (l, 0, j)),
                  pl.BlockSpec((1, 1, tn), lambda l, j: (l, 0, j))],
        out_specs=pl.BlockSpec((1, rows, tn), lambda l, j: (l, 0, j)),
        out_shape=jax.ShapeDtypeStruct((depth, rows, n), F32),
        compiler_params=_cparams(2),
        name="modulation",
    )(cvec, w_mod, b_mod.reshape(depth, 1, n))


def _normed(x, nw_ref, sh_ref, sc_ref):
    ms = jnp.mean(x * x, axis=-1, keepdims=True)
    h = x * lax.rsqrt(ms + NORM_EPS) * nw_ref[...]
    return h * (1.0 + sc_ref[0]) + sh_ref[0]


def _proj_body(mrow_ref, x_ref, nw_ref, sh_ref, sc_ref, *rest, n_out, col_chunk):
    del mrow_ref
    w_refs, o_refs = rest[:n_out], rest[n_out:]
    hb = _normed(x_ref[...], nw_ref, sh_ref, sc_ref).astype(BF16)
    for w_ref, o_ref in zip(w_refs, o_refs):
        n = w_ref.shape[1]
        for c0 in range(0, n, col_chunk):
            c1 = min(n, c0 + col_chunk)
            o_ref[:, c0:c1] = jnp.dot(hb, w_ref[:, c0:c1],
                                      preferred_element_type=F32).astype(o_ref.dtype)


def _mod_spec(d, chunk):
    return pl.BlockSpec((1, 1, d), lambda i, mrow: (mrow[i], 0, chunk))


def _fused_proj(x, mrow, norm_w, mods, shift_chunk, weights, out_dtypes):
    t, d = x.shape
    nseg = t // SEG
    n_out = len(weights)
    in_specs = [pl.BlockSpec((SEG, d), lambda i, mrow: (i, 0)),
                pl.BlockSpec((1, d), lambda i, mrow: (0, 0)),
                _mod_spec(d, shift_chunk), _mod_spec(d, shift_chunk + 1)]
    in_specs += [_resident(w.shape) for w in weights]
    out_specs = [pl.BlockSpec((SEG, w.shape[1]), lambda i, mrow: (i, 0)) for w in weights]
    out_shape = [jax.ShapeDtypeStruct((t, w.shape[1]), dt) for w, dt in zip(weights, out_dtypes)]
    return pl.pallas_call(
        functools.partial(_proj_body, n_out=n_out, col_chunk=512),
        grid_spec=pltpu.PrefetchScalarGridSpec(
            num_scalar_prefetch=1, grid=(nseg,), in_specs=in_specs, out_specs=out_specs),
        out_shape=out_shape,
        compiler_params=_cparams(1),
        name="norm_mod_proj",
    )(mrow, x, norm_w.reshape(1, d), mods, mods, *weights)


def _fill(scr, prev, cur, nxt, keep_prev, keep_next):
    scr[0:HALO, :] = prev * keep_prev
    scr[HALO:HALO + SEG, :] = cur
    scr[HALO + SEG:HALO + SEG + HALO, :] = nxt * keep_next


def _taps(scr, w_ref, c0, k_taps, left):
    acc = None
    for k in range(k_taps):
        term = scr[pl.ds(HALO - left + k, SEG), c0:c0 + LANES] * w_ref[k:k + 1, c0:c0 + LANES]
        acc = term if acc is None else acc + term
    return acc


def _keep(first_ref, last_ref):
    i = pl.program_id(0)
    return (1 - first_ref[i]).astype(F32), (1 - last_ref[i]).astype(F32)


def _ssdconv_body(first_ref, last_ref, p_ref, c_ref, n_ref, w_ref, b_ref, o_ref, scr):
    kp, kn = _keep(first_ref, last_ref)
    _fill(scr, p_ref[...].astype(F32), c_ref[...].astype(F32), n_ref[...].astype(F32), kp, kn)
    for c0 in range(0, o_ref.shape[1], LANES):
        y = _taps(scr, w_ref, c0, SSD_CONV, SSD_CONV // 2) + b_ref[:, c0:c0 + LANES]
        o_ref[:, c0:c0 + LANES] = _silu(y).astype(o_ref.dtype)


def _sconv_body(first_ref, last_ref, gp_ref, gc_ref, gn_ref, xp_ref, xc_ref, xn_ref, gb_ref,
                w_ref, o_ref, scr):
    kp, kn = _keep(first_ref, last_ref)
    f = lambda a, b: a[...].astype(F32) * b[...].astype(F32)
    _fill(scr, f(gp_ref, xp_ref), f(gc_ref, xc_ref), f(gn_ref, xn_ref), kp, kn)
    for c0 in range(0, o_ref.shape[1], LANES):
        y = _taps(scr, w_ref, c0, SCONV_K, SCONV_K // 2)
        o_ref[:, c0:c0 + LANES] = (gb_ref[:, c0:c0 + LANES].astype(F32) * y).astype(o_ref.dtype)


def _conf_body(first_ref, last_ref, ap_ref, ac_ref, an_ref, gp_ref, gc_ref, gn_ref,
               w_ref, b_ref, lnw_ref, lnb_ref, o_ref, scr, u_scr):
    kp, kn = _keep(first_ref, last_ref)
    f = lambda a, g: a[...].astype(F32) * jax.nn.sigmoid(g[...].astype(F32))
    _fill(scr, f(ap_ref, gp_ref), f(ac_ref, gc_ref), f(an_ref, gn_ref), kp, kn)
    for c0 in range(0, o_ref.shape[1], LANES):
        u_scr[:, c0:c0 + LANES] = _taps(scr, w_ref, c0, CONF_K, CONF_K // 2) + b_ref[:, c0:c0 + LANES]
    u = u_scr[...]
    mu = jnp.mean(u, axis=-1, keepdims=True)
    uc = u - mu
    var = jnp.mean(uc * uc, axis=-1, keepdims=True)
    y = uc * lax.rsqrt(var + NORM_EPS) * lnw_ref[...] + lnb_ref[...]
    o_ref[...] = _silu(y).astype(o_ref.dtype)


def _halo_specs(t, cw, col_off):
    per = SEG // HALO
    last_blk = t // HALO - 1
    return [
        pl.BlockSpec((HALO, cw), lambda i, j, f, l: (jnp.maximum(i * per - 1, 0), col_off + j)),
        pl.BlockSpec((SEG, cw), lambda i, j, f, l: (i, col_off + j)),
        pl.BlockSpec((HALO, cw), lambda i, j, f, l: (jnp.minimum((i + 1) * per, last_blk), col_off + j)),
    ]


def _conv_call(body, name, meta, t, cw, n_col, in_specs, args, out_cols, scratch):
    nseg = t // SEG
    return pl.pallas_call(
        body,
        grid_spec=pltpu.PrefetchScalarGridSpec(
            num_scalar_prefetch=2, grid=(nseg, n_col), in_specs=in_specs,
            out_specs=pl.BlockSpec((SEG, cw), lambda i, j, f, l: (i, j)),
            scratch_shapes=scratch),
        out_shape=jax.ShapeDtypeStruct((t, out_cols), BF16),
        compiler_params=_cparams(2),
        name=name,
    )(meta["first"], meta["last"], *args)


def _colvec_spec(rows, cw):
    return pl.BlockSpec((rows, cw), lambda i, j, f, l: (0, j))


def _ssd_conv(main, meta, conv_w, conv_b, col0, width):
    t = main.shape[0]
    cw = 512
    in_specs = _halo_specs(t, cw, col0 // cw) + [_colvec_spec(SSD_CONV, cw), _colvec_spec(1, cw)]
    return _conv_call(_ssdconv_body, "ssd_conv_silu", meta, t, cw, width // cw, in_specs,
                      (main, main, main, conv_w, conv_b.reshape(1, width)), width,
                      [pltpu.VMEM((SEG + 2 * HALO, cw), F32)])


def _short_conv(main, meta, sconv_w, col_gb, col_gc, col_xc, width):
    t = main.shape[0]
    cw = 512
    in_specs = (_halo_specs(t, cw, col_gc // cw) + _halo_specs(t, cw, col_xc // cw)
                + [pl.BlockSpec((SEG, cw), lambda i, j, f, l: (i, col_gb // cw + j)),
                   _colvec_spec(SCONV_K, cw)])
    return _conv_call(_sconv_body, "short_gated_conv", meta, t, cw, width // cw, in_specs,
                      (main,) * 7 + (sconv_w,), width, [pltpu.VMEM((SEG + 2 * HALO, cw), F32)])


def _conformer_conv(glu, meta, dw_w, dw_b, ln_w, ln_b):
    t, two_w = glu.shape
    w = two_w // 2
    in_specs = (_halo_specs(t, w, 0) + _halo_specs(t, w, 1)
                + [_colvec_spec(CONF_K, w)] + [_colvec_spec(1, w)] * 3)
    return _conv_call(_conf_body, "conformer_conv", meta, t, w, 1, in_specs,
                      (glu,) * 6 + (dw_w, dw_b.reshape(1, w), ln_w.reshape(1, w), ln_b.reshape(1, w)),
                      w, [pltpu.VMEM((SEG + 2 * HALO, w), F32), pltpu.VMEM((SEG, w), F32)])


def _softplus(x):
    return jnp.maximum(x, 0.0) + jnp.log1p(jnp.exp(-jnp.abs(x)))


def _ssd_body(*refs, nc, has_h0, want_state):
    it = iter(refs)
    x_ref, b_ref, c_ref, dt_ref, bias_ref, alog_ref, dsk_ref = (next(it) for _ in range(7))
    h0_ref = next(it) if has_h0 else None
    y_alias = next(it) if not want_state else None
    del y_alias
    y_ref = next(it)
    st_ref = next(it) if want_state else None
    ybuf, sf_scr, sb_scr = next(it), next(it), next(it)

    q = SSD_CHUNK
    hp = SSD_HEADS_PER_GROUP
    width = hp * SSD_HEAD_DIM
    row = lax.broadcasted_iota(jnp.int32, (q, q), 0)
    col = lax.broadcasted_iota(jnp.int32, (q, q), 1)
    lower = row >= col
    upper = col >= row
    lower_f = lower.astype(F32)
    lane_w = lax.broadcasted_iota(jnp.int32, (q, width), 1) // SSD_HEAD_DIM
    sub_w = lax.broadcasted_iota(jnp.int32, (width, 1), 0) // SSD_HEAD_DIM

    def per_head_cols(mat, base):
        out = mat[:, base + hp - 1:base + hp]
        for r in range(hp - 2, -1, -1):
            out = jnp.where(lane_w == r, mat[:, base + r:base + r + 1], out)
        return out

    def per_head_rows(rowvec, base):
        out = rowvec[:, base + hp - 1:base + hp]
        for r in range(hp - 2, -1, -1):
            out = jnp.where(sub_w == r, rowvec[:, base + r:base + r + 1], out)
        return out

    a_row = -jnp.exp(alog_ref[0])
    bias_row = bias_ref[0]

    def chunk_terms(c):
        r0 = pl.multiple_of(c * q, q)
        rows = pl.ds(r0, q)
        dt = _softplus(dt_ref[rows, :] + bias_row)
        a = dt * a_row
        cs = jnp.dot(lower_f, a, precision=HIGHEST, preferred_element_type=F32)
        return rows, dt, a, cs

    if has_h0:
        sf_scr[...] = h0_ref[0].reshape(width, SSD_STATE)
        sb_scr[...] = h0_ref[1].reshape(width, SSD_STATE)
    else:
        sf_scr[...] = jnp.zeros_like(sf_scr)
        sb_scr[...] = jnp.zeros_like(sb_scr)

    def bwd(i, carry):
        rows, dt, a, cs = chunk_terms(nc - 1 - i)
        ex = cs - a
        tot = cs[q - 1:q, :]
        x = x_ref[rows, :].astype(F32)
        s_prev = sb_scr[...]
        y_off = lax.dot_general(c_ref[rows, :], s_prev.astype(BF16), NT_DIMS, preferred_element_type=F32)
        ybuf[rows, :] = y_off * per_head_cols(jnp.exp(tot - ex), hp)
        xs = (x * per_head_cols(jnp.exp(ex) * dt, hp)).astype(BF16)
        contrib = lax.dot_general(xs, b_ref[rows, :], TN_DIMS, preferred_element_type=F32)
        sb_scr[...] = s_prev * per_head_rows(jnp.exp(tot), hp) + contrib
        return carry

    lax.fori_loop(0, nc, bwd, 0)

    dsk = dsk_ref[...]

    def fwd(c, carry):
        rows, dt, a, cs = chunk_terms(c)
        ex = cs - a
        tot = cs[q - 1:q, :]
        xb = x_ref[rows, :]
        x = xb.astype(F32)
        bm = b_ref[rows, :]
        cm = c_ref[rows, :]
        cb = lax.dot_general(cm, bm, NT_DIMS, preferred_element_type=F32)
        cs_t, ex_t, dt_t = cs.T, ex.T, dt.T
        y = ybuf[rows, :] + x * dsk
        for r in range(hp):
            dec_f = jnp.exp(jnp.where(lower, cs[:, r:r + 1] - cs_t[r:r + 1, :], NEG_BIG))
            dec_b = jnp.exp(jnp.where(upper, ex_t[hp + r:hp + r + 1, :] - ex[:, hp + r:hp + r + 1], NEG_BIG))
            wm = cb * (dec_f * dt_t[r:r + 1, :] + dec_b * dt_t[hp + r:hp + r + 1, :])
            yd = jnp.dot(wm.astype(BF16), xb, preferred_element_type=F32)
            y = y + jnp.where(lane_w == r, yd, 0.0)
        s_prev = sf_scr[...]
        y_off = lax.dot_general(cm, s_prev.astype(BF16), NT_DIMS, preferred_element_type=F32)
        y = y + y_off * per_head_cols(jnp.exp(cs), 0)
        y_ref[rows, :] = y.astype(y_ref.dtype)
        xs = (x * per_head_cols(jnp.exp(tot - cs) * dt, 0)).astype(BF16)
        contrib = lax.dot_general(xs, bm, TN_DIMS, preferred_element_type=F32)
        sf_scr[...] = s_prev * per_head_rows(jnp.exp(tot), 0) + contrib
        return carry

    lax.fori_loop(0, nc, fwd, 0)

    if want_state:
        st_ref[0] = sf_scr[...].reshape(hp, SSD_HEAD_DIM, SSD_STATE)
        st_ref[1] = sb_scr[...].reshape(hp, SSD_HEAD_DIM, SSD_STATE)


def _ssd_call(xbc, dtp, bias, alog, dsk, seq_len, n_seq, row_blk0, h0, layer_idx, y_prev):
    t = xbc.shape[0]
    inner = SSD_GROUPS * SSD_HEADS_PER_GROUP * SSD_HEAD_DIM
    width = SSD_HEADS_PER_GROUP * SSD_HEAD_DIM
    nb = inner // SSD_STATE
    want_state = h0 is None
    in_specs = [
        pl.BlockSpec((seq_len, width), lambda s, g: (row_blk0 + s, g)),
        pl.BlockSpec((seq_len, SSD_STATE), lambda s, g: (row_blk0 + s, nb + g)),
        pl.BlockSpec((seq_len, SSD_STATE), lambda s, g: (row_blk0 + s, nb + SSD_GROUPS + g)),
        pl.BlockSpec((seq_len, LANES), lambda s, g: (row_blk0 + s, g)),
        pl.BlockSpec((1, 1, LANES), lambda s, g: (g, 0, 0)),
        pl.BlockSpec((1, 1, LANES), lambda s, g: (g, 0, 0)),
        pl.BlockSpec((1, width), lambda s, g: (0, g)),
    ]
    args = [xbc, xbc, xbc, dtp, bias, alog, dsk]
    aliases = {}
    if not want_state:
        in_specs.append(pl.BlockSpec((None, None, 2, SSD_HEADS_PER_GROUP, SSD_HEAD_DIM, SSD_STATE),
                                     lambda s, g: (s, layer_idx, 0, g, 0, 0)))
        args.append(h0)
        in_specs.append(pl.BlockSpec(memory_space=pl.ANY))
        args.append(y_prev)
        aliases = {len(args) - 1: 0}
    out_specs = [pl.BlockSpec((seq_len, width), lambda s, g: (row_blk0 + s, g))]
    out_shape = [jax.ShapeDtypeStruct((t, inner), BF16)]
    if want_state:
        out_specs.append(pl.BlockSpec((None, 2, SSD_HEADS_PER_GROUP, SSD_HEAD_DIM, SSD_STATE),
                                      lambda s, g: (s, 0, g, 0, 0)))
        out_shape.append(jax.ShapeDtypeStruct(
            (n_seq, 2, SSD_GROUPS * SSD_HEADS_PER_GROUP, SSD_HEAD_DIM, SSD_STATE), F32))
    return pl.pallas_call(
        functools.partial(_ssd_body, nc=seq_len // SSD_CHUNK, has_h0=not want_state, want_state=want_state),
        grid=(n_seq, SSD_GROUPS),
        in_specs=in_specs, out_specs=out_specs, out_shape=out_shape,
        scratch_shapes=[pltpu.VMEM((seq_len, width), F32),
                        pltpu.VMEM((width, SSD_STATE), F32), pltpu.VMEM((width, SSD_STATE), F32)],
        input_output_aliases=aliases,
        compiler_params=_cparams(2),
        name="ssd_scan_ctx" if want_state else "ssd_scan_latent",
    )(*args)


def _outproj_body(mrow_ref, *refs, gated_norm):
    del mrow_ref
    if gated_norm:
        y_ref, z_ref, nw_ref, b_ref, w_ref, x_ref, g_ref, o_ref = refs
        y = y_ref[...].astype(F32) * _silu(z_ref[...].astype(F32))
        ms = jnp.mean(y * y, axis=-1, keepdims=True)
        a = (y * lax.rsqrt(ms + NORM_EPS) * nw_ref[...]).astype(BF16)
    else:
        a_ref, b_ref, w_ref, x_ref, g_ref, o_ref = refs
        a = a_ref[...]
    d = a.shape[1]
    out = jnp.dot(a, w_ref[0:d, :], preferred_element_type=F32)
    out = out + jnp.dot(b_ref[...], w_ref[d:, :], preferred_element_type=F32)
    o_ref[...] = x_ref[...] + g_ref[0] * out


def _out_proj(a, b, w, x, mrow, mods, gate_chunk, z_src=None, norm_w=None):
    t, d = x.shape
    nseg = t // SEG
    row = lambda i, mrow: (i, 0)
    in_specs = [pl.BlockSpec((SEG, d), row)]
    args = [a]
    if z_src is not None:
        in_specs += [pl.BlockSpec((SEG, d), row), pl.BlockSpec((1, d), lambda i, mrow: (0, 0))]
        args += [z_src, norm_w.reshape(1, d)]
    in_specs += [pl.BlockSpec((SEG, d), row), _resident(w.shape), pl.BlockSpec((SEG, d), row),
                 _mod_spec(d, gate_chunk)]
    args += [b, w, x, mods]
    return pl.pallas_call(
        functools.partial(_outproj_body, gated_norm=z_src is not None),
        grid_spec=pltpu.PrefetchScalarGridSpec(
            num_scalar_prefetch=1, grid=(nseg,), in_specs=in_specs,
            out_specs=pl.BlockSpec((SEG, d), row)),
        out_shape=jax.ShapeDtypeStruct((t, d), F32),
        compiler_params=_cparams(1),
        name="mixer_out_proj",
    )(mrow, *args)


def _qkprep_body(lat_ref, pos_ref, q_ref, kv_ref, qw_ref, kw_ref, cos_ref, sin_ref,
                 qo_ref, ko_ref, vo_ref, kn_ref, vn_ref):
    del pos_ref
    is_lat = lat_ref[pl.program_id(0)] > 0
    r = lax.broadcasted_iota(jnp.int32, (LANES, LANES), 0) // HEAD_DIM
    c = lax.broadcasted_iota(jnp.int32, (LANES, LANES), 1) // HEAD_DIM
    head_mean = jnp.where(r == c, 1.0 / HEAD_DIM, 0.0).astype(F32)
    lane = lax.broadcasted_iota(jnp.int32, (SEG, LANES), 1)
    first_half = (lane % HEAD_DIM) < (HEAD_DIM // 2)
    cos = cos_ref[...]
    sin = sin_ref[...]

    def norm_rope(xs, w):
        ms = jnp.dot(xs * xs, head_mean, precision=HIGHEST, preferred_element_type=F32)
        xn = xs * lax.rsqrt(ms + NORM_EPS) * w
        rot = jnp.where(first_half, pltpu.roll(xn, LANES - HEAD_DIM // 2, 1),
                        pltpu.roll(xn, HEAD_DIM // 2, 1))
        return xn, jnp.where(is_lat, xn * cos + rot * sin, xn)

    scale = HEAD_DIM ** -0.5 * LOG2_E
    for c0 in range(0, q_ref.shape[1], LANES):
        _, qr = norm_rope(q_ref[:, c0:c0 + LANES].astype(F32), qw_ref[...])
        qo_ref[:, c0:c0 + LANES] = (qr * scale).astype(qo_ref.dtype)
    kvw = ko_ref.shape[1]
    for c0 in range(0, kvw, LANES):
        kn, kr = norm_rope(kv_ref[:, c0:c0 + LANES], kw_ref[...])
        kn_ref[:, c0:c0 + LANES] = kn
        ko_ref[:, c0:c0 + LANES] = kr.astype(ko_ref.dtype)
    v = kv_ref[:, kvw:]
    vn_ref[...] = v
    vo_ref[...] = v.T.astype(vo_ref.dtype)


def _qk_prep(q, kv, meta, q_norm_w, k_norm_w, cos_t, sin_t):
    t, qd = q.shape
    kvd = kv.shape[1] // 2
    nseg = t // SEG
    row = lambda i, lat, pos: (i, 0)
    tab = lambda i, lat, pos: (pos[i] * lat[i], 0)
    tile2 = lambda w: jnp.tile(w, LANES // HEAD_DIM).reshape(1, LANES)
    per_chunk = ATT_TK // SEG
    return pl.pallas_call(
        _qkprep_body,
        grid_spec=pltpu.PrefetchScalarGridSpec(
            num_scalar_prefetch=2, grid=(nseg,),
            in_specs=[pl.BlockSpec((SEG, qd), row), pl.BlockSpec((SEG, 2 * kvd), row),
                      pl.BlockSpec((1, LANES), lambda i, lat, pos: (0, 0)),
                      pl.BlockSpec((1, LANES), lambda i, lat, pos: (0, 0)),
                      pl.BlockSpec((SEG, LANES), tab), pl.BlockSpec((SEG, LANES), tab)],
            out_specs=[pl.BlockSpec((SEG, qd), row), pl.BlockSpec((SEG, kvd), row),
                       pl.BlockSpec((None, kvd, SEG), lambda i, lat, pos: (i // per_chunk, 0, i % per_chunk)),
                       pl.BlockSpec((SEG, kvd), row), pl.BlockSpec((SEG, kvd), row)]),
        out_shape=[jax.ShapeDtypeStruct((t, qd), BF16), jax.ShapeDtypeStruct((t, kvd), BF16),
                   jax.ShapeDtypeStruct((t // ATT_TK, kvd, ATT_TK), BF16),
                   jax.ShapeDtypeStruct((t, kvd), F32), jax.ShapeDtypeStruct((t, kvd), F32)],
        compiler_params=_cparams(1),
        name="qk_norm_rope",
    )(meta["lat"], meta["pos"], q, kv, tile2(q_norm_w), tile2(k_norm_w), cos_t, sin_t)


def _rope_tables(seq_len):
    rows = seq_len // GRID_W
    rowp = jnp.repeat(jnp.arange(rows), GRID_W).astype(F32)
    colp = jnp.tile(jnp.arange(GRID_W), rows).astype(F32)
    axis = HEAD_DIM // 2
    inv = ROPE_THETA ** (-jnp.arange(0, axis, 2, dtype=F32) / axis)
    ang = jnp.concatenate([rowp[:, None] * inv, colp[:, None] * inv], axis=-1)
    cos, sin = jnp.cos(ang), jnp.sin(ang)
    cos_h = jnp.concatenate([cos, cos], axis=-1)
    sin_h = jnp.concatenate([-sin, sin], axis=-1)
    rep = LANES // HEAD_DIM
    return jnp.tile(cos_h, (1, rep)), jnp.tile(sin_h, (1, rep))


ONES_ROWS = 16


def _attn_body(*refs, chunked, tq, aliased):
    n_src = len(chunked)
    q_ref = refs[0]
    kv_refs = refs[1:1 + 2 * n_src]
    pos = 1 + 2 * n_src + (1 if aliased else 0)
    o_ref = refs[pos]
    qs_scr, acc_scr, m_scr = refs[pos + 1:]
    kvw = KV_HEADS * HEAD_DIM
    lane_g = lax.broadcasted_iota(jnp.int32, (tq, kvw), 1) // HEAD_DIM

    for g in range(KV_HEADS):
        for r in range(Q_PER_KV):
            qr = q_ref[:, kvw * r:kvw * (r + 1)]
            qs_scr[g, r * tq:(r + 1) * tq, :] = jnp.where(lane_g == g, qr, jnp.zeros_like(qr))
    m_scr[...] = jnp.full_like(m_scr, NEG_BIG)
    acc_scr[...] = jnp.zeros_like(acc_scr)

    def step(kc, vt):
        ones = jnp.ones((ONES_ROWS, kc.shape[0]), BF16)
        for g in range(KV_HEADS):
            s = lax.dot_general(kc, qs_scr[g], NT_DIMS, preferred_element_type=F32)
            m_prev = m_scr[g]
            m_new = jnp.maximum(m_prev, jnp.max(s, axis=0, keepdims=True))
            p = jnp.exp2(s - m_new).astype(BF16)
            lhs = jnp.concatenate([vt[HEAD_DIM * g:HEAD_DIM * (g + 1), :], ones], axis=0)
            pv = jnp.dot(lhs, p, preferred_element_type=F32)
            acc_scr[g] = acc_scr[g] * jnp.exp2(m_prev - m_new) + pv
            m_scr[g] = m_new

    for si, is_chunked in enumerate(chunked):
        k_ref, vt_ref = kv_refs[2 * si], kv_refs[2 * si + 1]
        if not is_chunked:
            step(k_ref[...], vt_ref[...])
        else:
            tk = vt_ref.shape[2]

            def body(j, carry, k_ref=k_ref, vt_ref=vt_ref, tk=tk):
                rows = pl.ds(pl.multiple_of(j * tk, tk), tk)
                step(k_ref[rows, :], vt_ref[j])
                return carry
            lax.fori_loop(0, vt_ref.shape[0], body, 0)

    for r in range(Q_PER_KV):
        cols = slice(r * tq, (r + 1) * tq)
        parts = [acc_scr[g, 0:HEAD_DIM, cols] / acc_scr[g, HEAD_DIM:HEAD_DIM + 1, cols]
                 for g in range(KV_HEADS)]
        o_ref[:, kvw * r:kvw * (r + 1)] = jnp.concatenate(parts, axis=0).T.astype(o_ref.dtype)


def _attn_call(q, srcs, q_blk0, n_seq, q_len, att_prev, name):
    t, qd = q.shape
    tq = SEG
    kvw = KV_HEADS * HEAD_DIM
    nq = q_len // tq
    qmap = lambda s, j: (q_blk0 + s * nq + j, 0)
    in_specs = [pl.BlockSpec((tq, qd), qmap)]
    args = [q]
    for k, k_spec, vt, vt_spec, _ in srcs:
        in_specs += [k_spec, vt_spec]
        args += [k, vt]
    aliases = {}
    if att_prev is not None:
        in_specs.append(pl.BlockSpec(memory_space=pl.ANY))
        args.append(att_prev)
        aliases = {len(args) - 1: 0}
    rows4 = Q_PER_KV * tq
    return pl.pallas_call(
        functools.partial(_attn_body, chunked=tuple(s[4] for s in srcs), tq=tq,
                          aliased=att_prev is not None),
        grid=(n_seq, nq),
        in_specs=in_specs,
        out_specs=pl.BlockSpec((tq, qd), qmap),
        out_shape=jax.ShapeDtypeStruct((t, qd), BF16),
        scratch_shapes=[pltpu.VMEM((KV_HEADS, rows4, kvw), BF16),
                        pltpu.VMEM((KV_HEADS, HEAD_DIM + ONES_ROWS, rows4), F32),
                        pltpu.VMEM((KV_HEADS, 1, rows4), F32)],
        input_output_aliases=aliases,
        compiler_params=_cparams(2),
        name=name,
    )(*args)


def _ffnpre_body(mrow_ref, x_ref, nw_ref, sh_ref, sc_ref, rw_ref, rb_ref,
                 h_ref, ti_ref, gt_ref, rk_ref, cnt_ref, cnt_scr):
    del mrow_ref

    @pl.when(pl.program_id(0) == 0)
    def _():
        cnt_scr[...] = jnp.zeros_like(cnt_scr)

    h = _normed(x_ref[...], nw_ref, sh_ref, sc_ref)
    hb = h.astype(BF16)
    h_ref[...] = hb
    logits = jnp.dot(hb, rw_ref[...], preferred_element_type=F32) + rb_ref[...]
    lane = lax.broadcasted_iota(jnp.int32, logits.shape, 1)
    work = logits
    vals, idxs = [], []
    for _ in range(TOP_K):
        m = jnp.max(work, axis=1, keepdims=True)
        idx = jnp.min(jnp.where(work == m, lane, LANES), axis=1, keepdims=True)
        vals.append(m)
        idxs.append(idx)
        work = jnp.where(lane == idx, NEG_BIG, work)
    es = [jnp.exp(v - vals[0]) for v in vals]
    den = es[0]
    for e in es[1:]:
        den = den + e
    ti = jnp.zeros(logits.shape, jnp.int32)
    gt = jnp.zeros(logits.shape, F32)
    for k in range(TOP_K):
        ti = jnp.where(lane == k, idxs[k], ti)
        gt = jnp.where(lane == k, es[k] / den, gt)
    ti_ref[...] = ti
    gt_ref[...] = gt

    onehot = jnp.zeros(logits.shape, F32)
    for k in range(TOP_K):
        onehot = onehot + (lane == idxs[k]).astype(F32)
    n = logits.shape[0]
    earlier = (lax.broadcasted_iota(jnp.int32, (n, n), 0) > lax.broadcasted_iota(jnp.int32, (n, n), 1))
    before = jnp.dot(earlier.astype(BF16), onehot.astype(BF16), preferred_element_type=F32) + cnt_scr[...]
    rk = jnp.zeros(logits.shape, jnp.int32)
    for k in range(TOP_K):
        r_k = jnp.sum(jnp.where(lane == idxs[k], before, 0.0), axis=1, keepdims=True)
        rk = jnp.where(lane == k, r_k.astype(jnp.int32), rk)
    rk_ref[...] = rk
    cnt_scr[...] = cnt_scr[...] + jnp.sum(onehot, axis=0, keepdims=True)
    cnt_ref[...] = cnt_scr[...]


def _ffn_pre(x, mrow, norm_w, mods, router_w, router_b):
    t, d = x.shape
    nseg = t // SEG
    ne = router_w.shape[1]
    rw = jnp.zeros((d, LANES), F32).at[:, :ne].set(router_w).astype(BF16)
    rb = jnp.full((1, LANES), NEG_BIG, F32).at[0, :ne].set(router_b)
    row = lambda i, mrow: (i, 0)
    const = lambda i, mrow: (0, 0)
    return pl.pallas_call(
        _ffnpre_body,
        grid_spec=pltpu.PrefetchScalarGridSpec(
            num_scalar_prefetch=1, grid=(nseg,),
            in_specs=[pl.BlockSpec((SEG, d), row), pl.BlockSpec((1, d), const),
                      _mod_spec(d, 3), _mod_spec(d, 4),
                      pl.BlockSpec((d, LANES), const), pl.BlockSpec((1, LANES), const)],
            out_specs=[pl.BlockSpec((SEG, d), row), pl.BlockSpec((SEG, LANES), row),
                       pl.BlockSpec((SEG, LANES), row), pl.BlockSpec((SEG, LANES), row),
                       pl.BlockSpec((1, LANES), const)],
            scratch_shapes=[pltpu.VMEM((1, LANES), F32)]),
        out_shape=[jax.ShapeDtypeStruct((t, d), BF16), jax.ShapeDtypeStruct((t, LANES), jnp.int32),
                   jax.ShapeDtypeStruct((t, LANES), F32), jax.ShapeDtypeStruct((t, LANES), jnp.int32),
                   jax.ShapeDtypeStruct((1, LANES), F32)],
        compiler_params=_cparams(1),
        name="ffn_norm_router_topk",
    )(mrow, x, norm_w.reshape(1, d), mods, mods, rw, rb)


def _experts_body(be_ref, nu_ref, x_ref, wgu_ref, bgu_ref, wdn_ref, bdn_ref, o_ref):
    del be_ref
    i = pl.program_id(0)

    @pl.when(i < nu_ref[0])
    def _():
        hg = jnp.dot(x_ref[...], wgu_ref[...], preferred_element_type=F32) + bgu_ref[...]
        dff = hg.shape[1] // 2
        g = jnp.minimum(hg[:, :dff], SWIGLU_LIMIT)
        u = jnp.clip(hg[:, dff:], -SWIGLU_LIMIT, SWIGLU_LIMIT)
        act = g * jax.nn.sigmoid(SWIGLU_ALPHA * g) * (u + 1.0)
        y = jnp.dot(act.astype(BF16), wdn_ref[...], preferred_element_type=F32) + bdn_ref[...]
        o_ref[...] = y.astype(o_ref.dtype)

    @pl.when(i >= nu_ref[0])
    def _():
        o_ref[...] = jnp.zeros_like(o_ref)


def _experts(xg, block_e, n_used, w_gu, b_gu, w_dn, b_dn):
    n_slots, d = xg.shape
    n_blocks = n_slots // MOE_ROWS
    ne, _, two_f = w_gu.shape
    dff = two_f // 2
    return pl.pallas_call(
        _experts_body,
        grid_spec=pltpu.PrefetchScalarGridSpec(
            num_scalar_prefetch=2, grid=(n_blocks,),
            in_specs=[pl.BlockSpec((MOE_ROWS, d), lambda i, be, nu: (i, 0)),
                      pl.BlockSpec((None, d, two_f), lambda i, be, nu: (be[i], 0, 0)),
                      pl.BlockSpec((None, 1, two_f), lambda i, be, nu: (be[i], 0, 0)),
                      pl.BlockSpec((None, dff, d), lambda i, be, nu: (be[i], 0, 0)),
                      pl.BlockSpec((None, 1, d), lambda i, be, nu: (be[i], 0, 0))],
            out_specs=pl.BlockSpec((MOE_ROWS, d), lambda i, be, nu: (i, 0))),
        out_shape=jax.ShapeDtypeStruct((n_slots, d), BF16),
        compiler_params=_cparams(1),
        name="expert_swiglu",
    )(block_e, n_used, xg, w_gu, b_gu.reshape(ne, 1, two_f), w_dn, b_dn.reshape(ne, 1, d))


def _moe(x, h2, top_i, gates, rank, counts, g2_rows, w_gu, b_gu, w_dn, b_dn):
    t, d = x.shape
    tk = t * TOP_K
    bm = MOE_ROWS
    counts = counts.astype(jnp.int32)
    padded = (counts + bm - 1) // bm * bm
    pad_end = jnp.cumsum(padded)
    pad_start = pad_end - padded
    experts = jnp.arange(N_EXPERTS, dtype=jnp.int32)
    dest = rank + jnp.sum(jnp.where(top_i[:, :, None] == experts, pad_start, 0), axis=-1)
    n_blocks = (tk + N_EXPERTS * (bm - 1) + bm - 1) // bm
    blk_start = jnp.arange(n_blocks, dtype=jnp.int32) * bm
    block_e = jnp.minimum(jnp.sum(pad_end[None, :] <= blk_start[:, None], axis=1),
                          N_EXPERTS - 1).astype(jnp.int32)
    n_used = (pad_end[-1:] // bm).astype(jnp.int32)
    slot_tok = jnp.zeros((n_blocks * bm,), jnp.int32).at[dest.reshape(tk)].set(
        jnp.arange(tk, dtype=jnp.int32) // TOP_K)
    xg = h2[slot_tok]
    yb = _experts(xg, block_e, n_used, w_gu, b_gu, w_dn, b_dn)
    f = jnp.einsum("tkd,tk->td", yb[dest].astype(F32), gates)
    return x + g2_rows * f


def _final_body(x_ref, w_ref, o_ref):
    x = x_ref[...]
    ms = jnp.mean(x * x, axis=-1, keepdims=True)
    o_ref[...] = x * lax.rsqrt(ms + NORM_EPS) * w_ref[...]


def _final_norm(x, w):
    t, d = x.shape
    return pl.pallas_call(
        _final_body,
        grid=(t // SEG,),
        in_specs=[pl.BlockSpec((SEG, d), lambda i: (i, 0)), pl.BlockSpec((1, d), lambda i: (0, 0))],
        out_specs=pl.BlockSpec((SEG, d), lambda i: (i, 0)),
        out_shape=jax.ShapeDtypeStruct((t, d), F32),
        compiler_params=_cparams(1),
        name="final_rmsnorm",
    )(x, w.reshape(1, d))


def _segment_meta(n_ctx, sp, n_lat, ss):
    mrow, first, last, pos, lat = [], [], [], [], []
    for n_seq, length, is_lat in ((n_ctx, sp, 0), (n_lat, ss, 1)):
        per = length // SEG
        for s in range(n_seq):
            for j in range(per):
                mrow.append(1 + s if is_lat else 0)
                first.append(int(j == 0))
                last.append(int(j == per - 1))
                pos.append(j)
                lat.append(is_lat)
    as_arr = lambda v: jnp.asarray(np.asarray(v, np.int32))
    return dict(mrow=as_arr(mrow), first=as_arr(first), last=as_arr(last), pos=as_arr(pos),
                lat=as_arr(lat), mrow_np=np.asarray(mrow, np.int32))


def kernel(x_prompt, x_sample, state_ssd, cache_k, cache_v, c, c_ctx, w_mod, b_mod, norm_mix_w, norm_ffn_w, ev_w_in, ev_conv_w, ev_conv_b, ev_dt_bias, ev_a_log, ev_d_skip, ev_norm_w, ev_sconv_w, ev_w_out, od_w_in, od_q_norm_w, od_k_norm_w, od_dw_w, od_dw_b, od_ln_w, od_ln_b, od_w_out, router_w, router_b, w_gu, b_gu, w_dn, b_dn, final_norm_w):
    bp, sp, d = x_prompt.shape
    bs, ss, _ = x_sample.shape
    depth = w_mod.shape[0]
    n_ctx_tok = bp * sp
    t = n_ctx_tok + bs * ss
    past = cache_k.shape[2]
    kvw = KV_HEADS * HEAD_DIM
    assert sp % SEG == 0 and ss % SEG == 0 and n_ctx_tok % ss == 0 and past % SEG == 0
    assert sp % SSD_CHUNK == 0 and ss % ATT_TK == 0 and ATT_TK % sp == 0 and d ==SSD_GROUPS * SSD_HEADS_PER_GROUP * SSD_HEAD_DIM

    meta = _segment_meta(bp, sp, bs, ss)
    mrow = meta["mrow"]
    x = jnp.concatenate([x_prompt.reshape(n_ctx_tok, d), x_sample.reshape(bs * ss, d)], axis=0)

    n_rows = 16
    cvec = jnp.zeros((n_rows, d), F32).at[0].set(c_ctx).at[1:1 + bs].set(c)
    mods_all = _modulation(cvec, w_mod, b_mod)
    cos_t, sin_t = _rope_tables(ss)

    n_heads = SSD_GROUPS * SSD_HEADS_PER_GROUP
    conv_dim = d + 2 * SSD_GROUPS * SSD_STATE
    states, ctx_k, ctx_v = [], [], []
    for l in range(depth):
        mods = mods_all[l].reshape(n_rows, 1, 6 * d)
        i = l // 2
        if l % 2 == 0:
            w_in = ev_w_in[i]
            o_dt = d + conv_dim
            w_main = jnp.concatenate([w_in[:, :o_dt], w_in[:, o_dt + 2 * n_heads:]], axis=1).astype(BF16)
            regroup = lambda v: v.reshape(v.shape[:-1] + (2, SSD_GROUPS, SSD_HEADS_PER_GROUP)).swapaxes(-3, -2)
            pad_lanes = lambda v: jnp.zeros(v.shape[:-3] + (SSD_GROUPS, LANES), F32).at[..., :2 * SSD_HEADS_PER_GROUP].set(
                v.reshape(v.shape[:-3] + (SSD_GROUPS, 2 * SSD_HEADS_PER_GROUP)))
            w_dt = pad_lanes(regroup(w_in[:, o_dt:o_dt + 2 * n_heads])).reshape(d, SSD_GROUPS * LANES).astype(BF16)
            bias = pad_lanes(regroup(ev_dt_bias[i].reshape(2 * n_heads))).reshape(SSD_GROUPS, 1, LANES)
            alog = pad_lanes(regroup(ev_a_log[i].reshape(2 * n_heads))).reshape(SSD_GROUPS, 1, LANES)
            dsk = jnp.repeat(ev_d_skip[i], SSD_HEAD_DIM).reshape(1, d)

            main, dtp = _fused_proj(x, mrow, norm_mix_w[l], mods, 0, [w_main, w_dt], [BF16, F32])
            xbc = _ssd_conv(main, meta, ev_conv_w[i], ev_conv_b[i], d, conv_dim)
            y, st = _ssd_call(xbc, dtp, bias, alog, dsk, sp, bp, 0, None, i, None)
            (y,) = _ssd_call(xbc, dtp, bias, alog, dsk, ss, bs, n_ctx_tok // ss, state_ssd, i, y)
            states.append(st)
            yc = _short_conv(main, meta, ev_sconv_w[i], o_dt, o_dt + d, o_dt + 2 * d, d)
            x = _out_proj(y, yc, ev_w_out[i].astype(BF16), x, mrow, mods, 2, z_src=main, norm_w=ev_norm_w[i])
        else:
            w_in = od_w_in[i]
            wq = w_in[:, :d].reshape(d, KV_HEADS, Q_PER_KV, HEAD_DIM).swapaxes(1, 2).reshape(d, d).astype(BF16)
            wkv = w_in[:, d:d + 2 * kvw].astype(BF16)
            wglu = w_in[:, d + 2 * kvw:].astype(BF16)
            w_out = od_w_out[i]
            w_att = w_out[:d].reshape(KV_HEADS, Q_PER_KV, HEAD_DIM, d).swapaxes(0, 1).reshape(d, d)
            w_out_p = jnp.concatenate([w_att, w_out[d:]], axis=0).astype(BF16)

            q, kv, glu = _fused_proj(x, mrow, norm_mix_w[l], mods, 0, [wq, wkv, wglu], [BF16, F32, BF16])
            qn, k_att, vt_att, k_n, v_n = _qk_prep(q, kv, meta, od_q_norm_w[i], od_k_norm_w[i], cos_t, sin_t)
            ctx_k.append(k_n[:n_ctx_tok].reshape(bp, sp, KV_HEADS, HEAD_DIM))
            ctx_v.append(v_n[:n_ctx_tok].reshape(bp, sp, KV_HEADS, HEAD_DIM))
            ck = cache_k[:, i].reshape(bs * past, kvw).astype(BF16)
            cvt = cache_v[:, i].reshape(bs, past, kvw).swapaxes(1, 2).astype(BF16)
            per = ATT_TK // sp
            ctx_src = (k_att, pl.BlockSpec((sp, kvw), lambda s, j: (s, 0)),
                       vt_att, pl.BlockSpec((None, kvw, sp), lambda s, j: (s // per, 0, s % per)), False)
            cache_src = (ck, pl.BlockSpec((past, kvw), lambda s, j: (s, 0)),
                         cvt, pl.BlockSpec((None, kvw, past), lambda s, j: (s, 0, 0)), False)
            lat_src = (k_att, pl.BlockSpec((ss, kvw), lambda s, j: (n_ctx_tok // ss + s, 0)),
                       vt_att, pl.BlockSpec((ss // ATT_TK, kvw, ATT_TK),
                                            lambda s, j: (n_ctx_tok // ss + s, 0, 0)), True)
            att = _attn_call(qn, [ctx_src], 0, bp, sp, None, "attention_ctx")
            att = _attn_call(qn, [cache_src, lat_src], n_ctx_tok // SEG, bs, ss, att, "attention_latent")
            u = _conformer_conv(glu, meta, od_dw_w[i], od_dw_b[i], od_ln_w[i], od_ln_b[i])
            x = _out_proj(att, u, w_out_p, x, mrow, mods, 2)

        h2, ti, gt, rk, cnt = _ffn_pre(x, mrow, norm_ffn_w[l], mods, router_w[l], router_b[l])
        g2_rows = jnp.repeat(mods_all[l][meta["mrow_np"], 5 * d:], SEG, axis=0)
        x = _moe(x, h2, ti[:, :TOP_K], gt[:, :TOP_K], rk[:, :TOP_K], cnt[0, :N_EXPERTS], g2_rows,
                 w_gu[l].astype(BF16), b_gu[l], w_dn[l].astype(BF16), b_dn[l])

    y = _final_norm(x, final_norm_w)
    y_prompt = y[:n_ctx_tok].reshape(bp, sp, d)
    y_sample = y[n_ctx_tok:].reshape(bs, ss, d)
    return (y_prompt, y_sample, jnp.stack(states, axis=1), jnp.stack(ctx_k, axis=1), jnp.stack(ctx_v, axis=1))
```

```python
import functools

import numpy as np
import jax
import jax.numpy as jnp
from jax import lax
from jax.experimental import pallas as pl
from jax.experimental.pallas import tpu as pltpu

F32 = jnp.float32
BF16 = jnp.bfloat16
HIGHEST = lax.Precision.HIGHEST

NORM_EPS = 1e-6
GRID_W = 64
ROPE_THETA = 10000.0
HEAD_DIM = 64
KV_HEADS = 4
Q_PER_KV = 4
SSD_HEAD_DIM = 64
SSD_GROUPS = 4
SSD_HEADS_PER_GROUP = 4
SSD_STATE = 128
SSD_CHUNK = 128
SSD_CONV = 4
SCONV_K = 3
CONF_K = 31
N_EXPERTS = 32
TOP_K = 4
SWIGLU_LIMIT = 7.0
SWIGLU_ALPHA = 1.702

SEG = 256
HALO = 16
LANES = 128
MOE_ROWS = 256
ATT_TK = 512
NEG_BIG = -1e30
LOG2_E = 1.4426950408889634
VMEM_LIMIT = 56 * 1024 * 1024

NT_DIMS = (((1,), (1,)), ((), ()))
TN_DIMS = (((0,), (0,)), ((), ()))


def _cparams(n_grid):
    return pltpu.CompilerParams(dimension_semantics=("arbitrary",) * n_grid,
                                vmem_limit_bytes=VMEM_LIMIT)


def _resident(shape):
    nd = len(shape)
    return pl.BlockSpec(shape, lambda *_: (0,) * nd, pipeline_mode=pl.Buffered(1))


def _silu(x):
    return x * jax.nn.sigmoid(x)


def _mod_body(c_ref, w_ref, b_ref, o_ref):
    c = c_ref[...]
    s = _silu(c).astype(BF16)
    o_ref[0] = jnp.dot(s, w_ref[0].astype(BF16), preferred_element_type=F32) + b_ref[0]


def _modulation(cvec, w_mod, b_mod):
    depth, d, n = w_mod.shape
    rows = cvec.shape[0]
    tn = 1536
    assert n % tn == 0
    return pl.pallas_call(
        _mod_body,
        grid=(depth, n // tn),
        in_specs=[pl.BlockSpec((rows, d), lambda l, j: (0, 0)),
                  pl.BlockSpec((1, d, tn), lambda l, j: (l, 0, j)),
                  pl.BlockSpec((1, 1, tn), lambda l, j: (l, 0, j))],
        out_specs=pl.BlockSpec((1, rows, tn), lambda l, j: (l, 0, j)),
        out_shape=jax.ShapeDtypeStruct((depth, rows, n), F32),
        compiler_params=_cparams(2),
        name="modulation",
    )(cvec, w_mod, b_mod.reshape(depth, 1, n))


def _normed(x, nw_ref, sh_ref, sc_ref):
    ms = jnp.mean(x * x, axis=-1, keepdims=True)
    h = x * lax.rsqrt(ms + NORM_EPS) * nw_ref[...]
    return h * (1.0 + sc_ref[0]) + sh_ref[0]


def _proj_body(mrow_ref, x_ref, nw_ref, sh_ref, sc_ref, *rest, n_out, col_chunk):
    del mrow_ref
    w_refs, o_refs = rest[:n_out], rest[n_out:]
    hb = _normed(x_ref[...], nw_ref, sh_ref, sc_ref).astype(BF16)
    for w_ref, o_ref in zip(w_refs, o_refs):
        n = w_ref.shape[1]
        for c0 in range(0, n, col_chunk):
            c1 = min(n, c0 + col_chunk)
            o_ref[:, c0:c1] = jnp.dot(hb, w_ref[:, c0:c1],
                                      preferred_element_type=F32).astype(o_ref.dtype)


def _mod_spec(d, chunk):
    return pl.BlockSpec((1, 1, d), lambda i, mrow: (mrow[i], 0, chunk))


def _fused_proj(x, mrow, norm_w, mods, shift_chunk, weights, out_dtypes):
    t, d = x.shape
    nseg = t // SEG
    n_out = len(weights)
    in_specs = [pl.BlockSpec((SEG, d), lambda i, mrow: (i, 0)),
                pl.BlockSpec((1, d), lambda i, mrow: (0, 0)),
                _mod_spec(d, shift_chunk), _mod_spec(d, shift_chunk + 1)]
    in_specs += [_resident(w.shape) for w in weights]
    out_specs = [pl.BlockSpec((SEG, w.shape[1]), lambda i, mrow: (i, 0)) for w in weights]
    out_shape = [jax.ShapeDtypeStruct((t, w.shape[1]), dt) for w, dt in zip(weights, out_dtypes)]
    return pl.pallas_call(
        functools.partial(_proj_body, n_out=n_out, col_chunk=512),
        grid_spec=pltpu.PrefetchScalarGridSpec(
            num_scalar_prefetch=1, grid=(nseg,), in_specs=in_specs, out_specs=out_specs),
        out_shape=out_shape,
        compiler_params=_cparams(1),
        name="norm_mod_proj",
    )(mrow, x, norm_w.reshape(1, d), mods, mods, *weights)


def _fill(scr, prev, cur, nxt, keep_prev, keep_next):
    scr[0:HALO, :] = prev * keep_prev
    scr[HALO:HALO + SEG, :] = cur
    scr[HALO + SEG:HALO + SEG + HALO, :] = nxt * keep_next


def _taps(scr, w_ref, c0, k_taps, left):
    acc = None
    for k in range(k_taps):
        term = scr[pl.ds(HALO - left + k, SEG), c0:c0 + LANES] * w_ref[k:k + 1, c0:c0 + LANES]
        acc = term if acc is None else acc + term
    return acc


def _keep(first_ref, last_ref):
    i = pl.program_id(0)
    return (1 - first_ref[i]).astype(F32), (1 - last_ref[i]).astype(F32)


def _ssdconv_body(first_ref, last_ref, p_ref, c_ref, n_ref, w_ref, b_ref, o_ref, scr):
    kp, kn = _keep(first_ref, last_ref)
    _fill(scr, p_ref[...].astype(F32), c_ref[...].astype(F32), n_ref[...].astype(F32), kp, kn)
    for c0 in range(0, o_ref.shape[1], LANES):
        y = _taps(scr, w_ref, c0, SSD_CONV, SSD_CONV // 2) + b_ref[:, c0:c0 + LANES]
        o_ref[:, c0:c0 + LANES] = _silu(y).astype(o_ref.dtype)


def _sconv_body(first_ref, last_ref, gp_ref, gc_ref, gn_ref, xp_ref, xc_ref, xn_ref, gb_ref,
                w_ref, o_ref, scr):
    kp, kn = _keep(first_ref, last_ref)
    f = lambda a, b: a[...].astype(F32) * b[...].astype(F32)
    _fill(scr, f(gp_ref, xp_ref), f(gc_ref, xc_ref), f(gn_ref, xn_ref), kp, kn)
    for c0 in range(0, o_ref.shape[1], LANES):
        y = _taps(scr, w_ref, c0, SCONV_K, SCONV_K // 2)
        o_ref[:, c0:c0 + LANES] = (gb_ref[:, c0:c0 + LANES].astype(F32) * y).astype(o_ref.dtype)


def _conf_body(first_ref, last_ref, ap_ref, ac_ref, an_ref, gp_ref, gc_ref, gn_ref,
               w_ref, b_ref, lnw_ref, lnb_ref, o_ref, scr, u_scr):
    kp, kn = _keep(first_ref, last_ref)
    f = lambda a, g: a[...].astype(F32) * jax.nn.sigmoid(g[...].astype(F32))
    _fill(scr, f(ap_ref, gp_ref), f(ac_ref, gc_ref), f(an_ref, gn_ref), kp, kn)
    for c0 in range(0, o_ref.shape[1], LANES):
        u_scr[:, c0:c0 + LANES] = _taps(scr, w_ref, c0, CONF_K, CONF_K // 2) + b_ref[:, c0:c0 + LANES]
    u = u_scr[...]
    mu = jnp.mean(u, axis=-1, keepdims=True)
    uc = u - mu
    var = jnp.mean(uc * uc, axis=-1, keepdims=True)
    y = uc * lax.rsqrt(var + NORM_EPS) * lnw_ref[...] + lnb_ref[...]
    o_ref[...] = _silu(y).astype(o_ref.dtype)


def _halo_specs(t, cw, col_off):
    per = SEG // HALO
    last_blk = t // HALO - 1
    return [
        pl.BlockSpec((HALO, cw), lambda i, j, f, l: (jnp.maximum(i * per - 1, 0), col_off + j)),
        pl.BlockSpec((SEG, cw), lambda i, j, f, l: (i, col_off + j)),
        pl.BlockSpec((HALO, cw), lambda i, j, f, l: (jnp.minimum((i + 1) * per, last_blk), col_off + j)),
    ]


def _conv_call(body, name, meta, t, cw, n_col, in_specs, args, out_cols, scratch):
    nseg = t // SEG
    return pl.pallas_call(
        body,
        grid_spec=pltpu.PrefetchScalarGridSpec(
            num_scalar_prefetch=2, grid=(nseg, n_col), in_specs=in_specs,
            out_specs=pl.BlockSpec((SEG, cw), lambda i, j, f, l: (i, j)),
            scratch_shapes=scratch),
        out_shape=jax.ShapeDtypeStruct((t, out_cols), BF16),
        compiler_params=_cparams(2),
        name=name,
    )(meta["first"], meta["last"], *args)


def _colvec_spec(rows, cw):
    return pl.BlockSpec((rows, cw), lambda i, j, f, l: (0, j))


def _ssd_conv(main, meta, conv_w, conv_b, col0, width):
    t = main.shape[0]
    cw = 512
    in_specs = _halo_specs(t, cw, col0 // cw) + [_colvec_spec(SSD_CONV, cw), _colvec_spec(1, cw)]
    return _conv_call(_ssdconv_body, "ssd_conv_silu", meta, t, cw, width // cw, in_specs,
                      (main, main, main, conv_w, conv_b.reshape(1, width)), width,
                      [pltpu.VMEM((SEG + 2 * HALO, cw), F32)])


def _short_conv(main, meta, sconv_w, col_gb, col_gc, col_xc, width):
    t = main.shape[0]
    cw = 512
    in_specs = (_halo_specs(t, cw, col_gc // cw) + _halo_specs(t, cw, col_xc // cw)
                + [pl.BlockSpec((SEG, cw), lambda i, j, f, l: (i, col_gb // cw + j)),
                   _colvec_spec(SCONV_K, cw)])
    return _conv_call(_sconv_body, "short_gated_conv", meta, t, cw, width // cw, in_specs,
                      (main,) * 7 + (sconv_w,), width, [pltpu.VMEM((SEG + 2 * HALO, cw), F32)])


def _conformer_conv(glu, meta, dw_w, dw_b, ln_w, ln_b):
    t, two_w = glu.shape
    w = two_w // 2
    in_specs = (_halo_specs(t, w, 0) + _halo_specs(t, w, 1)
                + [_colvec_spec(CONF_K, w)] + [_colvec_spec(1, w)] * 3)
    return _conv_call(_conf_body, "conformer_conv", meta, t, w, 1, in_specs,
                      (glu,) * 6 + (dw_w, dw_b.reshape(1, w), ln_w.reshape(1, w), ln_b.reshape(1, w)),
                      w, [pltpu.VMEM((SEG + 2 * HALO, w), F32), pltpu.VMEM((SEG, w), F32)])


def _softplus(x):
    return jnp.maximum(x, 0.0) + jnp.log1p(jnp.exp(-jnp.abs(x)))


def _ssd_body(*refs, nc, has_h0, want_state):
    it = iter(refs)
    x_ref, b_ref, c_ref, dt_ref, bias_ref, alog_ref, dsk_ref = (next(it) for _ in range(7))
    h0_ref = next(it) if has_h0 else None
    y_alias = next(it) if not want_state else None
    del y_alias
    y_ref = next(it)
    st_ref = next(it) if want_state else None
    ybuf, sf_scr, sb_scr = next(it), next(it), next(it)

    q = SSD_CHUNK
    hp = SSD_HEADS_PER_GROUP
    width = hp * SSD_HEAD_DIM
    row = lax.broadcasted_iota(jnp.int32, (q, q), 0)
    col = lax.broadcasted_iota(jnp.int32, (q, q), 1)
    lower = row >= col
    upper = col >= row
    lower_f = lower.astype(F32)
    lane_w = lax.broadcasted_iota(jnp.int32, (q, width), 1) // SSD_HEAD_DIM
    sub_w = lax.broadcasted_iota(jnp.int32, (width, 1), 0) // SSD_HEAD_DIM

    def per_head_cols(mat, base):
        out = mat[:, base + hp - 1:base + hp]
        for r in range(hp - 2, -1, -1):
            out = jnp.where(lane_w == r, mat[:, base + r:base + r + 1], out)
        return out

    def per_head_rows(rowvec, base):
        out = rowvec[:, base + hp - 1:base + hp]
        for r in range(hp - 2, -1, -1):
            out = jnp.where(sub_w == r, rowvec[:, base + r:base + r + 1], out)
        return out

    a_row = -jnp.exp(alog_ref[0])
    bias_row = bias_ref[0]

    def chunk_terms(c):
        r0 = pl.multiple_of(c * q, q)
        rows = pl.ds(r0, q)
        dt = _softplus(dt_ref[rows, :] + bias_row)
        a = dt * a_row
        cs = jnp.dot(lower_f, a, precision=HIGHEST, preferred_element_type=F32)
        return rows, dt, a, cs

    if has_h0:
        sf_scr[...] = h0_ref[0].reshape(width, SSD_STATE)
        sb_scr[...] = h0_ref[1].reshape(width, SSD_STATE)
    else:
        sf_scr[...] = jnp.zeros_like(sf_scr)
        sb_scr[...] = jnp.zeros_like(sb_scr)

    def bwd(i, carry):
        rows, dt, a, cs = chunk_terms(nc - 1 - i)
        ex = cs - a
        tot = cs[q - 1:q, :]
        x = x_ref[rows, :].astype(F32)
        s_prev = sb_scr[...]
        y_off = lax.dot_general(c_ref[rows, :], s_prev.astype(BF16), NT_DIMS, preferred_element_type=F32)
        ybuf[rows, :] = y_off * per_head_cols(jnp.exp(tot - ex), hp)
        xs = (x * per_head_cols(jnp.exp(ex) * dt, hp)).astype(BF16)
        contrib = lax.dot_general(xs, b_ref[rows, :], TN_DIMS, preferred_element_type=F32)
        sb_scr[...] = s_prev * per_head_rows(jnp.exp(tot), hp) + contrib
        return carry

    lax.fori_loop(0, nc, bwd, 0, unroll=2)

    dsk = dsk_ref[...]

    def fwd(c, carry):
        rows, dt, a, cs = chunk_terms(c)
        ex = cs - a
        tot = cs[q - 1:q, :]
        xb = x_ref[rows, :]
        x = xb.astype(F32)
        bm = b_ref[rows, :]
        cm = c_ref[rows, :]
        cb = lax.dot_general(cm, bm, NT_DIMS, preferred_element_type=F32)
        cs_t, ex_t, dt_t = cs.T, ex.T, dt.T
        y = ybuf[rows, :] + x * dsk
        for r in range(hp):
            dec_f = jnp.exp(jnp.where(lower, cs[:, r:r + 1] - cs_t[r:r + 1, :], NEG_BIG))
            dec_b = jnp.exp(jnp.where(upper, ex_t[hp + r:hp + r + 1, :] - ex[:, hp + r:hp + r + 1], NEG_BIG))
            wm = cb * (dec_f * dt_t[r:r + 1, :] + dec_b * dt_t[hp + r:hp + r + 1, :])
            yd = jnp.dot(wm.astype(BF16), xb, preferred_element_type=F32)
            y = y + jnp.where(lane_w == r, yd, 0.0)
        s_prev = sf_scr[...]
        y_off = lax.dot_general(cm, s_prev.astype(BF16), NT_DIMS, preferred_element_type=F32)
        y = y + y_off * per_head_cols(jnp.exp(cs), 0)
        y_ref[rows, :] = y.astype(y_ref.dtype)
        xs = (x * per_head_cols(jnp.exp(tot - cs) * dt, 0)).astype(BF16)
        contrib = lax.dot_general(xs, bm, TN_DIMS, preferred_element_type=F32)
        sf_scr[...] = s_prev * per_head_rows(jnp.exp(tot), 0) + contrib
        return carry

    lax.fori_loop(0, nc, fwd, 0, unroll=2)

    if want_state:
        st_ref[0] = sf_scr[...].reshape(hp, SSD_HEAD_DIM, SSD_STATE)
        st_ref[1] = sb_scr[...].reshape(hp, SSD_HEAD_DIM, SSD_STATE)


def _ssd_call(xbc, dtp, bias, alog, dsk, seq_len, n_seq, row_blk0, h0, layer_idx, y_prev):
    t = xbc.shape[0]
    inner = SSD_GROUPS * SSD_HEADS_PER_GROUP * SSD_HEAD_DIM
    width = SSD_HEADS_PER_GROUP * SSD_HEAD_DIM
    nb = inner // SSD_STATE
    want_state = h0 is None
    in_specs = [
        pl.BlockSpec((seq_len, width), lambda s, g: (row_blk0 + s, g)),
        pl.BlockSpec((seq_len, SSD_STATE), lambda s, g: (row_blk0 + s, nb + g)),
        pl.BlockSpec((seq_len, SSD_STATE), lambda s, g: (row_blk0 + s, nb + SSD_GROUPS + g)),
        pl.BlockSpec((seq_len, LANES), lambda s, g: (row_blk0 + s, g)),
        pl.BlockSpec((1, 1, LANES), lambda s, g: (g, 0, 0)),
        pl.BlockSpec((1, 1, LANES), lambda s, g: (g, 0, 0)),
        pl.BlockSpec((1, width), lambda s, g: (0, g)),
    ]
    args = [xbc, xbc, xbc, dtp, bias, alog, dsk]
    aliases = {}
    if not want_state:
        in_specs.append(pl.BlockSpec((None, None, 2, SSD_HEADS_PER_GROUP, SSD_HEAD_DIM, SSD_STATE),
                                     lambda s, g: (s, layer_idx, 0, g, 0, 0)))
        args.append(h0)
        in_specs.append(pl.BlockSpec(memory_space=pl.ANY))
        args.append(y_prev)
        aliases = {len(args) - 1: 0}
    out_specs = [pl.BlockSpec((seq_len, width), lambda s, g: (row_blk0 + s, g))]
    out_shape = [jax.ShapeDtypeStruct((t, inner), BF16)]
    if want_state:
        out_specs.append(pl.BlockSpec((None, 2, SSD_HEADS_PER_GROUP, SSD_HEAD_DIM, SSD_STATE),
                                      lambda s, g: (s, 0, g, 0, 0)))
        out_shape.append(jax.ShapeDtypeStruct(
            (n_seq, 2, SSD_GROUPS * SSD_HEADS_PER_GROUP, SSD_HEAD_DIM, SSD_STATE), F32))
    return pl.pallas_call(
        functools.partial(_ssd_body, nc=seq_len // SSD_CHUNK, has_h0=not want_state, want_state=want_state),
        grid=(n_seq, SSD_GROUPS),
        in_specs=in_specs, out_specs=out_specs, out_shape=out_shape,
        scratch_shapes=[pltpu.VMEM((seq_len, width), F32),
                        pltpu.VMEM((width, SSD_STATE), F32), pltpu.VMEM((width, SSD_STATE), F32)],
        input_output_aliases=aliases,
        compiler_params=_cparams(2),
        name="ssd_scan_ctx" if want_state else "ssd_scan_latent",
    )(*args)


def _outproj_body(mrow_ref, *refs, gated_norm):
    del mrow_ref
    if gated_norm:
        y_ref, z_ref, nw_ref, b_ref, w_ref, x_ref, g_ref, o_ref = refs
        y = y_ref[...].astype(F32) * _silu(z_ref[...].astype(F32))
        ms = jnp.mean(y * y, axis=-1, keepdims=True)
        a = (y * lax.rsqrt(ms + NORM_EPS) * nw_ref[...]).astype(BF16)
    else:
        a_ref, b_ref, w_ref, x_ref, g_ref, o_ref = refs
        a = a_ref[...]
    d = a.shape[1]
    out = jnp.dot(a, w_ref[0:d, :], preferred_element_type=F32)
    out = out + jnp.dot(b_ref[...], w_ref[d:, :], preferred_element_type=F32)
    o_ref[...] = x_ref[...] + g_ref[0] * out


def _out_proj(a, b, w, x, mrow, mods, gate_chunk, z_src=None, norm_w=None):
    t, d = x.shape
    nseg = t // SEG
    row = lambda i, mrow: (i, 0)
    in_specs = [pl.BlockSpec((SEG, d), row)]
    args = [a]
    if z_src is not None:
        in_specs += [pl.BlockSpec((SEG, d), row), pl.BlockSpec((1, d), lambda i, mrow: (0, 0))]
        args += [z_src, norm_w.reshape(1, d)]
    in_specs += [pl.BlockSpec((SEG, d), row), _resident(w.shape), pl.BlockSpec((SEG, d), row),
                 _mod_spec(d, gate_chunk)]
    args += [b, w, x, mods]
    return pl.pallas_call(
        functools.partial(_outproj_body, gated_norm=z_src is not None),
        grid_spec=pltpu.PrefetchScalarGridSpec(
            num_scalar_prefetch=1, grid=(nseg,), in_specs=in_specs,
            out_specs=pl.BlockSpec((SEG, d), row)),
        out_shape=jax.ShapeDtypeStruct((t, d), F32),
        compiler_params=_cparams(1),
        name="mixer_out_proj",
    )(mrow, *args)


def _qkprep_body(lat_ref, pos_ref, q_ref, kv_ref, qw_ref, kw_ref, cos_ref, sin_ref,
                 qo_ref, ko_ref, vo_ref, kn_ref, vn_ref):
    del pos_ref
    is_lat = lat_ref[pl.program_id(0)] > 0
    r = lax.broadcasted_iota(jnp.int32, (LANES, LANES), 0) // HEAD_DIM
    c = lax.broadcasted_iota(jnp.int32, (LANES, LANES), 1) // HEAD_DIM
    head_mean = jnp.where(r == c, 1.0 / HEAD_DIM, 0.0).astype(F32)
    lane = lax.broadcasted_iota(jnp.int32, (SEG, LANES), 1)
    first_half = (lane % HEAD_DIM) < (HEAD_DIM // 2)
    cos = cos_ref[...]
    sin = sin_ref[...]

    def norm_rope(xs, w):
        ms = jnp.dot(xs * xs, head_mean, precision=HIGHEST, preferred_element_type=F32)
        xn = xs * lax.rsqrt(ms + NORM_EPS) * w
        rot = jnp.where(first_half, pltpu.roll(xn, LANES - HEAD_DIM // 2, 1),
                        pltpu.roll(xn, HEAD_DIM // 2, 1))
        return xn, jnp.where(is_lat, xn * cos + rot * sin, xn)

    scale = HEAD_DIM ** -0.5 * LOG2_E
    for c0 in range(0, q_ref.shape[1], LANES):
        _, qr = norm_rope(q_ref[:, c0:c0 + LANES].astype(F32), qw_ref[...])
        qo_ref[:, c0:c0 + LANES] = (qr * scale).astype(qo_ref.dtype)
    kvw = ko_ref.shape[1]
    for c0 in range(0, kvw, LANES):
        kn, kr = norm_rope(kv_ref[:, c0:c0 + LANES], kw_ref[...])
        kn_ref[:, c0:c0 + LANES] = kn
        ko_ref[:, c0:c0 + LANES] = kr.astype(ko_ref.dtype)
    v = kv_ref[:, kvw:]
    vn_ref[...] = v
    vo_ref[...] = v.T.astype(vo_ref.dtype)


def _qk_prep(q, kv, meta, q_norm_w, k_norm_w, cos_t, sin_t):
    t, qd = q.shape
    kvd = kv.shape[1] // 2
    nseg = t // SEG
    row = lambda i, lat, pos: (i, 0)
    tab = lambda i, lat, pos: (pos[i] * lat[i], 0)
    tile2 = lambda w: jnp.tile(w, LANES // HEAD_DIM).reshape(1, LANES)
    per_chunk = ATT_TK // SEG
    return pl.pallas_call(
        _qkprep_body,
        grid_spec=pltpu.PrefetchScalarGridSpec(
            num_scalar_prefetch=2, grid=(nseg,),
            in_specs=[pl.BlockSpec((SEG, qd), row), pl.BlockSpec((SEG, 2 * kvd), row),
                      pl.BlockSpec((1, LANES), lambda i, lat, pos: (0, 0)),
                      pl.BlockSpec((1, LANES), lambda i, lat, pos: (0, 0)),
                      pl.BlockSpec((SEG, LANES), tab), pl.BlockSpec((SEG, LANES), tab)],
            out_specs=[pl.BlockSpec((SEG, qd), row), pl.BlockSpec((SEG, kvd), row),
                       pl.BlockSpec((None, kvd, SEG), lambda i, lat, pos: (i // per_chunk, 0, i % per_chunk)),
                       pl.BlockSpec((SEG, kvd), row), pl.BlockSpec((SEG, kvd), row)]),
        out_shape=[jax.ShapeDtypeStruct((t, qd), BF16), jax.ShapeDtypeStruct((t, kvd), BF16),
                   jax.ShapeDtypeStruct((t // ATT_TK, kvd, ATT_TK), BF16),
                   jax.ShapeDtypeStruct((t, kvd), F32), jax.ShapeDtypeStruct((t, kvd), F32)],
        compiler_params=_cparams(1),
        name="qk_norm_rope",
    )(meta["lat"], meta["pos"], q, kv, tile2(q_norm_w), tile2(k_norm_w), cos_t, sin_t)


def _rope_tables(seq_len):
    rows = seq_len // GRID_W
    rowp = jnp.repeat(jnp.arange(rows), GRID_W).astype(F32)
    colp = jnp.tile(jnp.arange(GRID_W), rows).astype(F32)
    axis = HEAD_DIM // 2
    inv = ROPE_THETA ** (-jnp.arange(0, axis, 2, dtype=F32) / axis)
    ang = jnp.concatenate([rowp[:, None] * inv, colp[:, None] * inv], axis=-1)
    cos, sin = jnp.cos(ang), jnp.sin(ang)
    cos_h = jnp.concatenate([cos, cos], axis=-1)
    sin_h = jnp.concatenate([-sin, sin], axis=-1)
    rep = LANES // HEAD_DIM
    return jnp.tile(cos_h, (1, rep)), jnp.tile(sin_h, (1, rep))


ONES_ROWS = 16


def _attn_body(*refs, chunked, tq, aliased):
    n_src = len(chunked)
    q_ref = refs[0]
    kv_refs = refs[1:1 + 2 * n_src]
    pos = 1 + 2 * n_src + (1 if aliased else 0)
    o_ref = refs[pos]
    qs_scr, acc_scr, m_scr = refs[pos + 1:]
    kvw = KV_HEADS * HEAD_DIM
    lane_g = lax.broadcasted_iota(jnp.int32, (tq, kvw), 1) // HEAD_DIM

    for g in range(KV_HEADS):
        for r in range(Q_PER_KV):
            qr = q_ref[:, kvw * r:kvw * (r + 1)]
            qs_scr[g, r * tq:(r + 1) * tq, :] = jnp.where(lane_g == g, qr, jnp.zeros_like(qr))
    m_scr[...] = jnp.full_like(m_scr, NEG_BIG)
    acc_scr[...] = jnp.zeros_like(acc_scr)

    def step(kc, vt):
        ones = jnp.ones((ONES_ROWS, kc.shape[0]), BF16)
        for g in range(KV_HEADS):
            s = lax.dot_general(kc, qs_scr[g], NT_DIMS, preferred_element_type=F32)
            m_prev = m_scr[g]
            m_new = jnp.maximum(m_prev, jnp.max(s, axis=0, keepdims=True))
            p = jnp.exp2(s - m_new).astype(BF16)
            lhs = jnp.concatenate([vt[HEAD_DIM * g:HEAD_DIM * (g + 1), :], ones], axis=0)
            pv = jnp.dot(lhs, p, preferred_element_type=F32)
            acc_scr[g] = acc_scr[g] * jnp.exp2(m_prev - m_new) + pv
            m_scr[g] = m_new

    for si, is_chunked in enumerate(chunked):
        k_ref, vt_ref = kv_refs[2 * si], kv_refs[2 * si + 1]
        if not is_chunked:
            step(k_ref[...], vt_ref[...])
        else:
            tk = vt_ref.shape[2]

            def body(j, carry, k_ref=k_ref, vt_ref=vt_ref, tk=tk):
                rows = pl.ds(pl.multiple_of(j * tk, tk), tk)
                step(k_ref[rows, :], vt_ref[j])
                return carry
            lax.fori_loop(0, vt_ref.shape[0], body, 0)

    for r in range(Q_PER_KV):
        cols = slice(r * tq, (r + 1) * tq)
        parts = [acc_scr[g, 0:HEAD_DIM, cols] / acc_scr[g, HEAD_DIM:HEAD_DIM + 1, cols]
                 for g in range(KV_HEADS)]
        o_ref[:, kvw * r:kvw * (r + 1)] = jnp.concatenate(parts, axis=0).T.astype(o_ref.dtype)


def _attn_call(q, srcs, q_blk0, n_seq, q_len, att_prev, name):
    t, qd = q.shape
    tq = SEG
    kvw = KV_HEADS * HEAD_DIM
    nq = q_len // tq
    qmap = lambda s, j: (q_blk0 + s * nq + j, 0)
    in_specs = [pl.BlockSpec((tq, qd), qmap)]
    args = [q]
    for k, k_spec, vt, vt_spec, _ in srcs:
        in_specs += [k_spec, vt_spec]
        args += [k, vt]
    aliases = {}
    if att_prev is not None:
        in_specs.append(pl.BlockSpec(memory_space=pl.ANY))
        args.append(att_prev)
        aliases = {len(args) - 1: 0}
    rows4 = Q_PER_KV * tq
    return pl.pallas_call(
        functools.partial(_attn_body, chunked=tuple(s[4] for s in srcs), tq=tq,
                          aliased=att_prev is not None),
        grid=(n_seq, nq),
        in_specs=in_specs,
        out_specs=pl.BlockSpec((tq, qd), qmap),
        out_shape=jax.ShapeDtypeStruct((t, qd), BF16),
        scratch_shapes=[pltpu.VMEM((KV_HEADS, rows4, kvw), BF16),
                        pltpu.VMEM((KV_HEADS, HEAD_DIM + ONES_ROWS, rows4), F32),
                        pltpu.VMEM((KV_HEADS, 1, rows4), F32)],
        input_output_aliases=aliases,
        compiler_params=_cparams(2),
        name=name,
    )(*args)


def _ffnpre_body(mrow_ref, x_ref, nw_ref, sh_ref, sc_ref, rw_ref, rb_ref,
                 h_ref, ti_ref, gt_ref, rk_ref, cnt_ref, cnt_scr):
    del mrow_ref

    @pl.when(pl.program_id(0) == 0)
    def _():
        cnt_scr[...] = jnp.zeros_like(cnt_scr)

    h = _normed(x_ref[...], nw_ref, sh_ref, sc_ref)
    hb = h.astype(BF16)
    h_ref[...] = hb
    logits = jnp.dot(hb, rw_ref[...], preferred_element_type=F32) + rb_ref[...]
    lane = lax.broadcasted_iota(jnp.int32, logits.shape, 1)
    work = logits
    vals, idxs = [], []
    for _ in range(TOP_K):
        m = jnp.max(work, axis=1, keepdims=True)
        idx = jnp.min(jnp.where(work == m, lane, LANES), axis=1, keepdims=True)
        vals.append(m)
        idxs.append(idx)
        work = jnp.where(lane == idx, NEG_BIG, work)
    es = [jnp.exp(v - vals[0]) for v in vals]
    den = es[0]
    for e in es[1:]:
        den = den + e
    ti = jnp.zeros(logits.shape, jnp.int32)
    gt = jnp.zeros(logits.shape, F32)
    for k in range(TOP_K):
        ti = jnp.where(lane == k, idxs[k], ti)
        gt = jnp.where(lane == k, es[k] / den, gt)
    ti_ref[...] = ti
    gt_ref[...] = gt

    onehot = jnp.zeros(logits.shape, F32)
    for k in range(TOP_K):
        onehot = onehot + (lane == idxs[k]).astype(F32)
    n = logits.shape[0]
    earlier = (lax.broadcasted_iota(jnp.int32, (n, n), 0) > lax.broadcasted_iota(jnp.int32, (n, n), 1))
    before = jnp.dot(earlier.astype(BF16), onehot.astype(BF16), preferred_element_type=F32) + cnt_scr[...]
    rk = jnp.zeros(logits.shape, jnp.int32)
    for k in range(TOP_K):
        r_k = jnp.sum(jnp.where(lane == idxs[k], before, 0.0), axis=1, keepdims=True)
        rk = jnp.where(lane == k, r_k.astype(jnp.int32), rk)
    rk_ref[...] = rk
    cnt_scr[...] = cnt_scr[...] + jnp.sum(onehot, axis=0, keepdims=True)
    cnt_ref[...] = cnt_scr[...]


def _ffn_pre(x, mrow, norm_w, mods, router_w, router_b):
    t, d = x.shape
    nseg = t // SEG
    ne = router_w.shape[1]
    rw = jnp.zeros((d, LANES), F32).at[:, :ne].set(router_w).astype(BF16)
    rb = jnp.full((1, LANES), NEG_BIG, F32).at[0, :ne].set(router_b)
    row = lambda i, mrow: (i, 0)
    const = lambda i, mrow: (0, 0)
    return pl.pallas_call(
        _ffnpre_body,
        grid_spec=pltpu.PrefetchScalarGridSpec(
            num_scalar_prefetch=1, grid=(nseg,),
            in_specs=[pl.BlockSpec((SEG, d), row), pl.BlockSpec((1, d), const),
                      _mod_spec(d, 3), _mod_spec(d, 4),
                      pl.BlockSpec((d, LANES), const), pl.BlockSpec((1, LANES), const)],
            out_specs=[pl.BlockSpec((SEG, d), row), pl.BlockSpec((SEG, LANES), row),
                       pl.BlockSpec((SEG, LANES), row), pl.BlockSpec((SEG, LANES), row),
                       pl.BlockSpec((1, LANES), const)],
            scratch_shapes=[pltpu.VMEM((1, LANES), F32)]),
        out_shape=[jax.ShapeDtypeStruct((t, d), BF16), jax.ShapeDtypeStruct((t, LANES), jnp.int32),
                   jax.ShapeDtypeStruct((t, LANES), F32), jax.ShapeDtypeStruct((t, LANES), jnp.int32),
                   jax.ShapeDtypeStruct((1, LANES), F32)],
        compiler_params=_cparams(1),
        name="ffn_norm_router_topk",
    )(mrow, x, norm_w.reshape(1, d), mods, mods, rw, rb)


def _experts_body(be_ref, nu_ref, x_ref, wgu_ref, bgu_ref, wdn_ref, bdn_ref, o_ref, wgu_bf, wdn_bf):
    i = pl.program_id(0)
    live = i < nu_ref[0]
    new_expert = jnp.logical_or(i == 0, be_ref[i] != be_ref[jnp.maximum(i - 1, 0)])

    @pl.when(jnp.logical_and(live, new_expert))
    def _():
        wgu_bf[...] = wgu_ref[...].astype(BF16)
        wdn_bf[...] = wdn_ref[...].astype(BF16)

    @pl.when(live)
    def _():
        hg = jnp.dot(x_ref[...], wgu_bf[...], preferred_element_type=F32) + bgu_ref[...]
        dff = hg.shape[1] // 2
        g = jnp.minimum(hg[:, :dff], SWIGLU_LIMIT)
        u = jnp.clip(hg[:, dff:], -SWIGLU_LIMIT, SWIGLU_LIMIT)
        act = g * jax.nn.sigmoid(SWIGLU_ALPHA * g) * (u + 1.0)
        y = jnp.dot(act.astype(BF16), wdn_bf[...], preferred_element_type=F32) + bdn_ref[...]
        o_ref[...] = y.astype(o_ref.dtype)

    @pl.when(i >= nu_ref[0])
    def _():
        o_ref[...] = jnp.zeros_like(o_ref)


def _experts(xg, block_e, n_used, w_gu, b_gu, w_dn, b_dn):
    n_slots, d = xg.shape
    n_blocks = n_slots // MOE_ROWS
    ne, _, two_f = w_gu.shape
    dff = two_f // 2
    return pl.pallas_call(
        _experts_body,
        grid_spec=pltpu.PrefetchScalarGridSpec(
            num_scalar_prefetch=2, grid=(n_blocks,),
            in_specs=[pl.BlockSpec((MOE_ROWS, d), lambda i, be, nu: (i, 0)),
                      pl.BlockSpec((None, d, two_f), lambda i, be, nu: (be[i], 0, 0)),
                      pl.BlockSpec((None, 1, two_f), lambda i, be, nu: (be[i], 0, 0)),
                      pl.BlockSpec((None, dff, d), lambda i, be, nu: (be[i], 0, 0)),
                      pl.BlockSpec((None, 1, d), lambda i, be, nu: (be[i], 0, 0))],
            out_specs=pl.BlockSpec((MOE_ROWS, d), lambda i, be, nu: (i, 0)),
            scratch_shapes=[pltpu.VMEM((d, two_f), BF16), pltpu.VMEM((dff, d), BF16)]),
        out_shape=jax.ShapeDtypeStruct((n_slots, d), BF16),
        compiler_params=_cparams(1),
        name="expert_swiglu",
    )(block_e, n_used, xg, w_gu, b_gu.reshape(ne, 1, two_f), w_dn, b_dn.reshape(ne, 1, d))


def _combine_body(mrow_ref, x_ref, y0_ref, y1_ref, y2_ref, y3_ref, gt_ref, g2_ref, o_ref):
    del mrow_ref
    gt = gt_ref[...]
    f = None
    for k, y_ref in enumerate((y0_ref, y1_ref, y2_ref, y3_ref)):
        term = y_ref[...].astype(F32) * gt[:, k:k + 1]
        f = term if f is None else f + term
    o_ref[...] = x_ref[...] + g2_ref[0] * f


def _combine(x, yg, gates, mrow, mods):
    t, d = x.shape
    nseg = t // SEG
    row = lambda i, mrow: (i, 0)
    assert TOP_K == 4
    in_specs = [pl.BlockSpec((SEG, d), row)]
    in_specs += [pl.BlockSpec((SEG, d), lambda i, mrow, k=k: (k * nseg + i, 0)) for k in range(TOP_K)]
    in_specs += [pl.BlockSpec((SEG, LANES), row), _mod_spec(d, 5)]
    return pl.pallas_call(
        _combine_body,
        grid_spec=pltpu.PrefetchScalarGridSpec(
            num_scalar_prefetch=1, grid=(nseg,), in_specs=in_specs,
            out_specs=pl.BlockSpec((SEG, d), row)),
        out_shape=jax.ShapeDtypeStruct((t, d), F32),
        compiler_params=_cparams(1),
        name="moe_combine_residual",
    )(mrow, x, yg, yg, yg, yg, gates, mods)


def _moe(x, h2, top_i, gates, rank, counts, mrow, mods, w_gu, b_gu, w_dn, b_dn):
    t, d = x.shape
    tk = t * TOP_K
    bm = MOE_ROWS
    counts = counts.astype(jnp.int32)
    padded = (counts + bm - 1) // bm * bm
    pad_end = jnp.cumsum(padded)
    pad_start = pad_end - padded
    start = jnp.cumsum(counts) - counts
    experts = jnp.arange(N_EXPERTS, dtype=jnp.int32)
    ti4 = top_i[:, :TOP_K]
    dest = rank[:, :TOP_K] + jnp.sum(jnp.where(ti4[:, :, None] == experts, pad_start, 0), axis=-1)
    n_blocks = (tk + N_EXPERTS * (bm - 1) + bm - 1) // bm
    blk_start = jnp.arange(n_blocks, dtype=jnp.int32) * bm
    block_e = jnp.minimum(jnp.sum(pad_end[None, :] <= blk_start[:, None], axis=1),
                          N_EXPERTS - 1).astype(jnp.int32)
    n_used = (pad_end[-1:] // bm).astype(jnp.int32)
    order = jnp.argsort(ti4.reshape(tk), stable=True).astype(jnp.int32)
    off = (blk_start - pad_start[block_e])[:, None] + jnp.arange(bm, dtype=jnp.int32)
    src = jnp.where(off < counts[block_e][:, None], start[block_e][:, None] + off, 0)
    slot_tok = order[src.reshape(n_blocks * bm)] // TOP_K
    xg = h2[slot_tok]
    yb = _experts(xg, block_e, n_used, w_gu, b_gu, w_dn, b_dn)
    yg = yb[dest.T.reshape(tk)]
    return _combine(x, yg, gates, mrow, mods)


def _final_body(x_ref, w_ref, o_ref):
    x = x_ref[...]
    ms = jnp.mean(x * x, axis=-1, keepdims=True)
    o_ref[...] = x * lax.rsqrt(ms + NORM_EPS) * w_ref[...]


def _final_norm(x, w):
    t, d = x.shape
    return pl.pallas_call(
        _final_body,
        grid=(t // SEG,),
        in_specs=[pl.BlockSpec((SEG, d), lambda i: (i, 0)), pl.BlockSpec((1, d), lambda i: (0, 0))],
        out_specs=pl.BlockSpec((SEG, d), lambda i: (i, 0)),
        out_shape=jax.ShapeDtypeStruct((t, d), F32),
        compiler_params=_cparams(1),
        name="final_rmsnorm",
    )(x, w.reshape(1, d))


def _segment_meta(n_ctx, sp, n_lat, ss):
    mrow, first, last, pos, lat = [], [], [], [], []
    for n_seq, length, is_lat in ((n_ctx, sp, 0), (n_lat, ss, 1)):
        per = length // SEG
        for s in range(n_seq):
            for j in range(per):
                mrow.append(1 + s if is_lat else 0)
                first.append(int(j == 0))
                last.append(int(j == per - 1))
                pos.append(j)
                lat.append(is_lat)
    as_arr = lambda v: jnp.asarray(np.asarray(v, np.int32))
    return dict(mrow=as_arr(mrow), first=as_arr(first), last=as_arr(last), pos=as_arr(pos),
                lat=as_arr(lat), mrow_np=np.asarray(mrow, np.int32))


def kernel(x_prompt, x_sample, state_ssd, cache_k, cache_v, c, c_ctx, w_mod, b_mod, norm_mix_w, norm_ffn_w, ev_w_in, ev_conv_w, ev_conv_b, ev_dt_bias, ev_a_log, ev_d_skip, ev_norm_w, ev_sconv_w, ev_w_out, od_w_in, od_q_norm_w, od_k_norm_w, od_dw_w, od_dw_b, od_ln_w, od_ln_b, od_w_out, router_w, router_b, w_gu, b_gu, w_dn, b_dn, final_norm_w):
    bp, sp, d = x_prompt.shape
    bs, ss, _ = x_sample.shape
    depth = w_mod.shape[0]
    n_ctx_tok = bp * sp
    t = n_ctx_tok + bs * ss
    past = cache_k.shape[2]
    kvw = KV_HEADS * HEAD_DIM
    assert sp % SEG == 0 and ss % SEG == 0 and n_ctx_tok % ss == 0 and past % SEG == 0
    assert sp % SSD_CHUNK == 0 and ss % ATT_TK == 0 and ATT_TK % sp == 0 and d ==SSD_GROUPS * SSD_HEADS_PER_GROUP * SSD_HEAD_DIM

    meta = _segment_meta(bp, sp, bs, ss)
    mrow = meta["mrow"]
    x = jnp.concatenate([x_prompt.reshape(n_ctx_tok, d), x_sample.reshape(bs * ss, d)], axis=0)

    n_rows = 16
    cvec = jnp.zeros((n_rows, d), F32).at[0].set(c_ctx).at[1:1 + bs].set(c)
    mods_all = _modulation(cvec, w_mod, b_mod)
    cos_t, sin_t = _rope_tables(ss)

    n_heads = SSD_GROUPS * SSD_HEADS_PER_GROUP
    conv_dim = d + 2 * SSD_GROUPS * SSD_STATE
    states, ctx_k, ctx_v = [], [], []
    for l in range(depth):
        mods = mods_all[l].reshape(n_rows, 1, 6 * d)
        i = l // 2
        if l % 2 == 0:
            w_in = ev_w_in[i]
            o_dt = d + conv_dim
            w_main = jnp.concatenate([w_in[:, :o_dt], w_in[:, o_dt + 2 * n_heads:]], axis=1).astype(BF16)
            regroup = lambda v: v.reshape(v.shape[:-1] + (2, SSD_GROUPS, SSD_HEADS_PER_GROUP)).swapaxes(-3, -2)
            pad_lanes = lambda v: jnp.zeros(v.shape[:-3] + (SSD_GROUPS, LANES), F32).at[..., :2 * SSD_HEADS_PER_GROUP].set(
                v.reshape(v.shape[:-3] + (SSD_GROUPS, 2 * SSD_HEADS_PER_GROUP)))
            w_dt = pad_lanes(regroup(w_in[:, o_dt:o_dt + 2 * n_heads])).reshape(d, SSD_GROUPS * LANES).astype(BF16)
            bias = pad_lanes(regroup(ev_dt_bias[i].reshape(2 * n_heads))).reshape(SSD_GROUPS, 1, LANES)
            alog = pad_lanes(regroup(ev_a_log[i].reshape(2 * n_heads))).reshape(SSD_GROUPS, 1, LANES)
            dsk = jnp.repeat(ev_d_skip[i], SSD_HEAD_DIM).reshape(1, d)

            main, dtp = _fused_proj(x, mrow, norm_mix_w[l], mods, 0, [w_main, w_dt], [BF16, F32])
            xbc = _ssd_conv(main, meta, ev_conv_w[i], ev_conv_b[i], d, conv_dim)
            y, st = _ssd_call(xbc, dtp, bias, alog, dsk, sp, bp, 0, None, i, None)
            (y,) = _ssd_call(xbc, dtp, bias, alog, dsk, ss, bs, n_ctx_tok // ss, state_ssd, i, y)
            states.append(st)
            yc = _short_conv(main, meta, ev_sconv_w[i], o_dt, o_dt + d, o_dt + 2 * d, d)
            x = _out_proj(y, yc, ev_w_out[i].astype(BF16), x, mrow, mods, 2, z_src=main, norm_w=ev_norm_w[i])
        else:
            w_in = od_w_in[i]
            wq = w_in[:, :d].reshape(d, KV_HEADS, Q_PER_KV, HEAD_DIM).swapaxes(1, 2).reshape(d, d).astype(BF16)
            wkv = w_in[:, d:d + 2 * kvw].astype(BF16)
            wglu = w_in[:, d + 2 * kvw:].astype(BF16)
            w_out = od_w_out[i]
            w_att = w_out[:d].reshape(KV_HEADS, Q_PER_KV, HEAD_DIM, d).swapaxes(0, 1).reshape(d, d)
            w_out_p = jnp.concatenate([w_att, w_out[d:]], axis=0).astype(BF16)

            q, kv, glu = _fused_proj(x, mrow, norm_mix_w[l], mods, 0, [wq, wkv, wglu], [BF16, F32, BF16])
            qn, k_att, vt_att, k_n, v_n = _qk_prep(q, kv, meta, od_q_norm_w[i], od_k_norm_w[i], cos_t, sin_t)
            ctx_k.append(k_n[:n_ctx_tok].reshape(bp, sp, KV_HEADS, HEAD_DIM))
            ctx_v.append(v_n[:n_ctx_tok].reshape(bp, sp, KV_HEADS, HEAD_DIM))
            ck = cache_k[:, i].reshape(bs * past, kvw).astype(BF16)
            cvt = cache_v[:, i].reshape(bs, past, kvw).swapaxes(1, 2).astype(BF16)
            per = ATT_TK // sp
            ctx_src = (k_att, pl.BlockSpec((sp, kvw), lambda s, j: (s, 0)),
                       vt_att, pl.BlockSpec((None, kvw, sp), lambda s, j: (s // per, 0, s % per)), False)
            cache_src = (ck, pl.BlockSpec((past, kvw), lambda s, j: (s, 0)),
                         cvt, pl.BlockSpec((None, kvw, past), lambda s, j: (s, 0, 0)), False)
            lat_src = (k_att, pl.BlockSpec((ss, kvw), lambda s, j: (n_ctx_tok // ss + s, 0)),
                       vt_att, pl.BlockSpec((ss // ATT_TK, kvw, ATT_TK),
                                            lambda s, j: (n_ctx_tok // ss + s, 0, 0)), True)
            att = _attn_call(qn, [ctx_src], 0, bp, sp, None, "attention_ctx")
            att = _attn_call(qn, [cache_src, lat_src], n_ctx_tok // SEG, bs, ss, att, "attention_latent")
            u = _conformer_conv(glu, meta, od_dw_w[i], od_dw_b[i], od_ln_w[i], od_ln_b[i])
            x = _out_proj(att, u, w_out_p, x, mrow, mods, 2)

        h2, ti, gt, rk, cnt = _ffn_pre(x, mrow, norm_ffn_w[l], mods, router_w[l], router_b[l])
        x = _moe(x, h2, ti, gt, rk, cnt[0, :N_EXPERTS], mrow, mods, w_gu[l], b_gu[l], w_dn[l], b_dn[l])

    y = _final_norm(x, final_norm_w)
    y_prompt = y[:n_ctx_tok].reshape(bp, sp, d)
    y_sample = y[n_ctx_tok:].reshape(bs, ss, d)
    return (y_prompt, y_sample, jnp.stack(states, axis=1), jnp.stack(ctx_k, axis=1), jnp.stack(ctx_v, axis=1))
```

```python
import functools

import numpy as np
import jax
import jax.numpy as jnp
from jax import lax
from jax.experimental import pallas as pl
from jax.experimental.pallas import tpu as pltpu

F32 = jnp.float32
BF16 = jnp.bfloat16
HIGHEST = lax.Precision.HIGHEST

NORM_EPS = 1e-6
GRID_W = 64
ROPE_THETA = 10000.0
HEAD_DIM = 64
KV_HEADS = 4
Q_PER_KV = 4
SSD_HEAD_DIM = 64
SSD_GROUPS = 4
SSD_HEADS_PER_GROUP = 4
SSD_STATE = 128
SSD_CHUNK = 128
SSD_CONV = 4
SCONV_K = 3
CONF_K = 31
N_EXPERTS = 32
TOP_K = 4
SWIGLU_LIMIT = 7.0
SWIGLU_ALPHA = 1.702

SEG = 256
HALO = 16
LANES = 128
SUBLANES = 8
MOE_ROWS = 256
ATT_TK = 512
ATT_STRIP = 512
NEG_BIG = -1e30
LOG2_E = 1.4426950408889634
VMEM_LIMIT = 56 * 1024 * 1024

NT_DIMS = (((1,), (1,)), ((), ()))
TN_DIMS = (((0,), (0,)), ((), ()))


def _cparams(n_grid):
    return pltpu.CompilerParams(dimension_semantics=("arbitrary",) * n_grid,
                                vmem_limit_bytes=VMEM_LIMIT)


def _resident(shape):
    nd = len(shape)
    return pl.BlockSpec(shape, lambda *_: (0,) * nd, pipeline_mode=pl.Buffered(1))


def _silu(x):
    return x * jax.nn.sigmoid(x)


def _mod_body(c_ref, w_ref, b_ref, o_ref):
    c = c_ref[...]
    s = _silu(c).astype(BF16)
    o_ref[0] = jnp.dot(s, w_ref[0].astype(BF16), preferred_element_type=F32) + b_ref[0]


def _modulation(cvec, w_mod, b_mod):
    depth, d, n = w_mod.shape
    rows = cvec.shape[0]
    tn = 1536
    assert n % tn == 0
    return pl.pallas_call(
        _mod_body,
        grid=(depth, n // tn),
        in_specs=[pl.BlockSpec((rows, d), lambda l, j: (0, 0)),
                  pl.BlockSpec((1, d, tn), lambda l, j: (l, 0, j)),
                  pl.BlockSpec((1, 1, tn), lambda l, j: (l, 0, j))],
        out_specs=pl.BlockSpec((1, rows, tn), lambda l, j: (l, 0, j)),
        out_shape=jax.ShapeDtypeStruct((depth, rows, n), F32),
        compiler_params=_cparams(2),
        name="modulation",
    )(cvec, w_mod, b_mod.reshape(depth, 1, n))


def _normed(x, nw_ref, sh_ref, sc_ref):
    ms = jnp.mean(x * x, axis=-1, keepdims=True)
    h = x * lax.rsqrt(ms + NORM_EPS) * nw_ref[...]
    return h * (1.0 + sc_ref[0]) + sh_ref[0]


def _proj_body(mrow_ref, x_ref, nw_ref, sh_ref, sc_ref, *rest, n_out, col_chunk):
    del mrow_ref
    w_refs, o_refs = rest[:n_out], rest[n_out:]
    hb = _normed(x_ref[...], nw_ref, sh_ref, sc_ref).astype(BF16)
    for w_ref, o_ref in zip(w_refs, o_refs):
        n = w_ref.shape[1]
        for c0 in range(0, n, col_chunk):
            c1 = min(n, c0 + col_chunk)
            o_ref[:, c0:c1] = jnp.dot(hb, w_ref[:, c0:c1],
                                      preferred_element_type=F32).astype(o_ref.dtype)


def _mod_spec(d, chunk):
    return pl.BlockSpec((1, 1, d), lambda i, mrow: (mrow[i], 0, chunk))


def _fused_proj(x, mrow, norm_w, mods, shift_chunk, weights, out_dtypes):
    t, d = x.shape
    nseg = t // SEG
    n_out = len(weights)
    in_specs = [pl.BlockSpec((SEG, d), lambda i, mrow: (i, 0)),
                pl.BlockSpec((1, d), lambda i, mrow: (0, 0)),
                _mod_spec(d, shift_chunk), _mod_spec(d, shift_chunk + 1)]
    in_specs += [_resident(w.shape) for w in weights]
    out_specs = [pl.BlockSpec((SEG, w.shape[1]), lambda i, mrow: (i, 0)) for w in weights]
    out_shape = [jax.ShapeDtypeStruct((t, w.shape[1]), dt) for w, dt in zip(weights, out_dtypes)]
    return pl.pallas_call(
        functools.partial(_proj_body, n_out=n_out, col_chunk=512),
        grid_spec=pltpu.PrefetchScalarGridSpec(
            num_scalar_prefetch=1, grid=(nseg,), in_specs=in_specs, out_specs=out_specs),
        out_shape=out_shape,
        compiler_params=_cparams(1),
        name="norm_mod_proj",
    )(mrow, x, norm_w.reshape(1, d), mods, mods, *weights)


def _fill(scr, prev, cur, nxt, keep_prev, keep_next):
    scr[0:HALO, :] = prev * keep_prev
    scr[HALO:HALO + SEG, :] = cur
    scr[HALO + SEG:HALO + SEG + HALO, :] = nxt * keep_next


def _taps(scr, w_ref, c0, k_taps, left):
    acc = None
    for k in range(k_taps):
        term = scr[pl.ds(HALO - left + k, SEG), c0:c0 + LANES] * w_ref[k:k + 1, c0:c0 + LANES]
        acc = term if acc is None else acc + term
    return acc


def _keep(first_ref, last_ref):
    i = pl.program_id(0)
    return (1 - first_ref[i]).astype(F32), (1 - last_ref[i]).astype(F32)


def _ssdconv_body(first_ref, last_ref, p_ref, c_ref, n_ref, w_ref, b_ref, o_ref, scr):
    kp, kn = _keep(first_ref, last_ref)
    _fill(scr, p_ref[...].astype(F32), c_ref[...].astype(F32), n_ref[...].astype(F32), kp, kn)
    for c0 in range(0, o_ref.shape[1], LANES):
        y = _taps(scr, w_ref, c0, SSD_CONV, SSD_CONV // 2) + b_ref[:, c0:c0 + LANES]
        o_ref[:, c0:c0 + LANES] = _silu(y).astype(o_ref.dtype)


def _sconv_body(first_ref, last_ref, gp_ref, gc_ref, gn_ref, xp_ref, xc_ref, xn_ref, gb_ref,
                w_ref, o_ref, scr):
    kp, kn = _keep(first_ref, last_ref)
    f = lambda a, b: a[...].astype(F32) * b[...].astype(F32)
    _fill(scr, f(gp_ref, xp_ref), f(gc_ref, xc_ref), f(gn_ref, xn_ref), kp, kn)
    for c0 in range(0, o_ref.shape[1], LANES):
        y = _taps(scr, w_ref, c0, SCONV_K, SCONV_K // 2)
        o_ref[:, c0:c0 + LANES] = (gb_ref[:, c0:c0 + LANES].astype(F32) * y).astype(o_ref.dtype)


def _conf_body(first_ref, last_ref, ap_ref, ac_ref, an_ref, gp_ref, gc_ref, gn_ref,
               w_ref, b_ref, lnw_ref, lnb_ref, o_ref, scr, u_scr, sh_scr):
    kp, kn = _keep(first_ref, last_ref)
    f = lambda a, g: a[...].astype(F32) * jax.nn.sigmoid(g[...].astype(F32))
    _fill(scr, f(ap_ref, gp_ref), f(ac_ref, gc_ref), f(an_ref, gn_ref), kp, kn)
    span = SEG + 2 * HALO - SUBLANES
    for r in range(SUBLANES):
        sh_scr[r] = scr[pl.ds(r, span), :]
    left = CONF_K // 2
    rows = SEG // 4
    for c0 in range(0, o_ref.shape[1], LANES):
        for r0 in range(0, SEG, rows):
            acc = None
            for k in range(CONF_K):
                off = HALO - left + k
                term = (sh_scr[off % SUBLANES, pl.ds(r0 + off - off % SUBLANES, rows), c0:c0 + LANES]
                        * w_ref[k:k + 1, c0:c0 + LANES])
                acc = term if acc is None else acc + term
            u_scr[r0:r0 + rows, c0:c0 + LANES] = acc + b_ref[:, c0:c0 + LANES]
    u = u_scr[...]
    mu = jnp.mean(u, axis=-1, keepdims=True)
    uc = u - mu
    var = jnp.mean(uc * uc, axis=-1, keepdims=True)
    y = uc * lax.rsqrt(var + NORM_EPS) * lnw_ref[...] + lnb_ref[...]
    o_ref[...] = _silu(y).astype(o_ref.dtype)


def _halo_specs(t, cw, col_off):
    per = SEG // HALO
    last_blk = t // HALO - 1
    return [
        pl.BlockSpec((HALO, cw), lambda i, j, f, l: (jnp.maximum(i * per - 1, 0), col_off + j)),
        pl.BlockSpec((SEG, cw), lambda i, j, f, l: (i, col_off + j)),
        pl.BlockSpec((HALO, cw), lambda i, j, f, l: (jnp.minimum((i + 1) * per, last_blk), col_off + j)),
    ]


def _conv_call(body, name, meta, t, cw, n_col, in_specs, args, out_cols, scratch):
    nseg = t // SEG
    return pl.pallas_call(
        body,
        grid_spec=pltpu.PrefetchScalarGridSpec(
            num_scalar_prefetch=2, grid=(nseg, n_col), in_specs=in_specs,
            out_specs=pl.BlockSpec((SEG, cw), lambda i, j, f, l: (i, j)),
            scratch_shapes=scratch),
        out_shape=jax.ShapeDtypeStruct((t, out_cols), BF16),
        compiler_params=_cparams(2),
        name=name,
    )(meta["first"], meta["last"], *args)


def _colvec_spec(rows, cw):
    return pl.BlockSpec((rows, cw), lambda i, j, f, l: (0, j))


def _ssd_conv(main, meta, conv_w, conv_b, col0, width):
    t = main.shape[0]
    cw = 512
    in_specs = _halo_specs(t, cw, col0 // cw) + [_colvec_spec(SSD_CONV, cw), _colvec_spec(1, cw)]
    return _conv_call(_ssdconv_body, "ssd_conv_silu", meta, t, cw, width // cw, in_specs,
                      (main, main, main, conv_w, conv_b.reshape(1, width)), width,
                      [pltpu.VMEM((SEG + 2 * HALO, cw), F32)])


def _short_conv(main, meta, sconv_w, col_gb, col_gc, col_xc, width):
    t = main.shape[0]
    cw = 512
    in_specs = (_halo_specs(t, cw, col_gc // cw) + _halo_specs(t, cw, col_xc // cw)
                + [pl.BlockSpec((SEG, cw), lambda i, j, f, l: (i, col_gb // cw + j)),
                   _colvec_spec(SCONV_K, cw)])
    return _conv_call(_sconv_body, "short_gated_conv", meta, t, cw, width // cw, in_specs,
                      (main,) * 7 + (sconv_w,), width, [pltpu.VMEM((SEG + 2 * HALO, cw), F32)])


def _conformer_conv(glu, meta, dw_w, dw_b, ln_w, ln_b):
    t, two_w = glu.shape
    w = two_w // 2
    in_specs = (_halo_specs(t, w, 0) + _halo_specs(t, w, 1)
                + [_colvec_spec(CONF_K, w)] + [_colvec_spec(1, w)] * 3)
    return _conv_call(_conf_body, "conformer_conv", meta, t, w, 1, in_specs,
                      (glu,) * 6 + (dw_w, dw_b.reshape(1, w), ln_w.reshape(1, w), ln_b.reshape(1, w)),
                      w, [pltpu.VMEM((SEG + 2 * HALO, w), F32), pltpu.VMEM((SEG, w), F32),
                       pltpu.VMEM((SUBLANES, SEG + 2 * HALO - SUBLANES, w), F32)])


def _softplus(x):
    return jnp.maximum(x, 0.0) + jnp.log1p(jnp.exp(-jnp.abs(x)))


def _ssd_body(*refs, nc, has_h0, want_state):
    it = iter(refs)
    x_ref, b_ref, c_ref, dt_ref, bias_ref, alog_ref, dsk_ref = (next(it) for _ in range(7))
    h0_ref = next(it) if has_h0 else None
    y_alias = next(it) if not want_state else None
    del y_alias
    y_ref = next(it)
    st_ref = next(it) if want_state else None
    ybuf, sf_scr, sb_scr = next(it), next(it), next(it)

    q = SSD_CHUNK
    hp = SSD_HEADS_PER_GROUP
    width = hp * SSD_HEAD_DIM
    row = lax.broadcasted_iota(jnp.int32, (q, q), 0)
    col = lax.broadcasted_iota(jnp.int32, (q, q), 1)
    lower = row >= col
    upper = col >= row
    lower_f = lower.astype(F32)
    lane_w = lax.broadcasted_iota(jnp.int32, (q, width), 1) // SSD_HEAD_DIM
    sub_w = lax.broadcasted_iota(jnp.int32, (width, 1), 0) // SSD_HEAD_DIM

    def per_head_cols(mat, base):
        out = mat[:, base + hp - 1:base + hp]
        for r in range(hp - 2, -1, -1):
            out = jnp.where(lane_w == r, mat[:, base + r:base + r + 1], out)
        return out

    def per_head_rows(rowvec, base):
        out = rowvec[:, base + hp - 1:base + hp]
        for r in range(hp - 2, -1, -1):
            out = jnp.where(sub_w == r, rowvec[:, base + r:base + r + 1], out)
        return out

    a_row = -jnp.exp(alog_ref[0])
    bias_row = bias_ref[0]

    def chunk_terms(c):
        r0 = pl.multiple_of(c * q, q)
        rows = pl.ds(r0, q)
        dt = _softplus(dt_ref[rows, :] + bias_row)
        a = dt * a_row
        cs = jnp.dot(lower_f, a, precision=HIGHEST, preferred_element_type=F32)
        return rows, dt, a, cs

    if has_h0:
        sf_scr[...] = h0_ref[0].reshape(width, SSD_STATE)
        sb_scr[...] = h0_ref[1].reshape(width, SSD_STATE)
    else:
        sf_scr[...] = jnp.zeros_like(sf_scr)
        sb_scr[...] = jnp.zeros_like(sb_scr)

    def bwd(i, carry):
        rows, dt, a, cs = chunk_terms(nc - 1 - i)
        ex = cs - a
        tot = cs[q - 1:q, :]
        x = x_ref[rows, :].astype(F32)
        s_prev = sb_scr[...]
        y_off = lax.dot_general(c_ref[rows, :], s_prev.astype(BF16), NT_DIMS, preferred_element_type=F32)
        ybuf[rows, :] = y_off * per_head_cols(jnp.exp(tot - ex), hp)
        xs = (x * per_head_cols(jnp.exp(ex) * dt, hp)).astype(BF16)
        contrib = lax.dot_general(xs, b_ref[rows, :], TN_DIMS, preferred_element_type=F32)
        sb_scr[...] = s_prev * per_head_rows(jnp.exp(tot), hp) + contrib
        return carry

    lax.fori_loop(0, nc, bwd, 0, unroll=2)

    dsk = dsk_ref[...]

    def fwd(c, carry):
        rows, dt, a, cs = chunk_terms(c)
        ex = cs - a
        tot = cs[q - 1:q, :]
        xb = x_ref[rows, :]
        x = xb.astype(F32)
        bm = b_ref[rows, :]
        cm = c_ref[rows, :]
        cb = lax.dot_general(cm, bm, NT_DIMS, preferred_element_type=F32)
        cs_t, ex_t, dt_t = cs.T, ex.T, dt.T
        y = ybuf[rows, :] + x * dsk
        for r in range(hp):
            dec_f = jnp.exp(jnp.where(lower, cs[:, r:r + 1] - cs_t[r:r + 1, :], NEG_BIG))
            dec_b = jnp.exp(jnp.where(upper, ex_t[hp + r:hp + r + 1, :] - ex[:, hp + r:hp + r + 1], NEG_BIG))
            wm = cb * (dec_f * dt_t[r:r + 1, :] + dec_b * dt_t[hp + r:hp + r + 1, :])
            yd = jnp.dot(wm.astype(BF16), xb, preferred_element_type=F32)
            y = y + jnp.where(lane_w == r, yd, 0.0)
        s_prev = sf_scr[...]
        y_off = lax.dot_general(cm, s_prev.astype(BF16), NT_DIMS, preferred_element_type=F32)
        y = y + y_off * per_head_cols(jnp.exp(cs), 0)
        y_ref[rows, :] = y.astype(y_ref.dtype)
        xs = (x * per_head_cols(jnp.exp(tot - cs) * dt, 0)).astype(BF16)
        contrib = lax.dot_general(xs, bm, TN_DIMS, preferred_element_type=F32)
        sf_scr[...] = s_prev * per_head_rows(jnp.exp(tot), 0) + contrib
        return carry

    lax.fori_loop(0, nc, fwd, 0, unroll=2)

    if want_state:
        st_ref[0] = sf_scr[...].reshape(hp, SSD_HEAD_DIM, SSD_STATE)
        st_ref[1] = sb_scr[...].reshape(hp, SSD_HEAD_DIM, SSD_STATE)


def _ssd_call(xbc, dtp, bias, alog, dsk, seq_len, n_seq, row_blk0, h0, layer_idx, y_prev):
    t = xbc.shape[0]
    inner = SSD_GROUPS * SSD_HEADS_PER_GROUP * SSD_HEAD_DIM
    width = SSD_HEADS_PER_GROUP * SSD_HEAD_DIM
    nb = inner // SSD_STATE
    want_state = h0 is None
    in_specs = [
        pl.BlockSpec((seq_len, width), lambda s, g: (row_blk0 + s, g)),
        pl.BlockSpec((seq_len, SSD_STATE), lambda s, g: (row_blk0 + s, nb + g)),
        pl.BlockSpec((seq_len, SSD_STATE), lambda s, g: (row_blk0 + s, nb + SSD_GROUPS + g)),
        pl.BlockSpec((seq_len, LANES), lambda s, g: (row_blk0 + s, g)),
        pl.BlockSpec((1, 1, LANES), lambda s, g: (g, 0, 0)),
        pl.BlockSpec((1, 1, LANES), lambda s, g: (g, 0, 0)),
        pl.BlockSpec((1, width), lambda s, g: (0, g)),
    ]
    args = [xbc, xbc, xbc, dtp, bias, alog, dsk]
    aliases = {}
    if not want_state:
        in_specs.append(pl.BlockSpec((None, None, 2, SSD_HEADS_PER_GROUP, SSD_HEAD_DIM, SSD_STATE),
                                     lambda s, g: (s, layer_idx, 0, g, 0, 0)))
        args.append(h0)
        in_specs.append(pl.BlockSpec(memory_space=pl.ANY))
        args.append(y_prev)
        aliases = {len(args) - 1: 0}
    out_specs = [pl.BlockSpec((seq_len, width), lambda s, g: (row_blk0 + s, g))]
    out_shape = [jax.ShapeDtypeStruct((t, inner), BF16)]
    if want_state:
        out_specs.append(pl.BlockSpec((None, 2, SSD_HEADS_PER_GROUP, SSD_HEAD_DIM, SSD_STATE),
                                      lambda s, g: (s, 0, g, 0, 0)))
        out_shape.append(jax.ShapeDtypeStruct(
            (n_seq, 2, SSD_GROUPS * SSD_HEADS_PER_GROUP, SSD_HEAD_DIM, SSD_STATE), F32))
    return pl.pallas_call(
        functools.partial(_ssd_body, nc=seq_len // SSD_CHUNK, has_h0=not want_state, want_state=want_state),
        grid=(n_seq, SSD_GROUPS),
        in_specs=in_specs, out_specs=out_specs, out_shape=out_shape,
        scratch_shapes=[pltpu.VMEM((seq_len, width), F32),
                        pltpu.VMEM((width, SSD_STATE), F32), pltpu.VMEM((width, SSD_STATE), F32)],
        input_output_aliases=aliases,
        compiler_params=_cparams(2),
        name="ssd_scan_ctx" if want_state else "ssd_scan_latent",
    )(*args)


def _outproj_body(mrow_ref, *refs, gated_norm):
    del mrow_ref
    if gated_norm:
        y_ref, z_ref, nw_ref, b_ref, w_ref, x_ref, g_ref, o_ref = refs
        y = y_ref[...].astype(F32) * _silu(z_ref[...].astype(F32))
        ms = jnp.mean(y * y, axis=-1, keepdims=True)
        a = (y * lax.rsqrt(ms + NORM_EPS) * nw_ref[...]).astype(BF16)
    else:
        a_ref, b_ref, w_ref, x_ref, g_ref, o_ref = refs
        a = a_ref[...]
    d = a.shape[1]
    out = jnp.dot(a, w_ref[0:d, :], preferred_element_type=F32)
    out = out + jnp.dot(b_ref[...], w_ref[d:, :], preferred_element_type=F32)
    o_ref[...] = x_ref[...] + g_ref[0] * out


def _out_proj(a, b, w, x, mrow, mods, gate_chunk, z_src=None, norm_w=None):
    t, d = x.shape
    nseg = t // SEG
    row = lambda i, mrow: (i, 0)
    in_specs = [pl.BlockSpec((SEG, d), row)]
    args = [a]
    if z_src is not None:
        in_specs += [pl.BlockSpec((SEG, d), row), pl.BlockSpec((1, d), lambda i, mrow: (0, 0))]
        args += [z_src, norm_w.reshape(1, d)]
    in_specs += [pl.BlockSpec((SEG, d), row), _resident(w.shape), pl.BlockSpec((SEG, d), row),
                 _mod_spec(d, gate_chunk)]
    args += [b, w, x, mods]
    return pl.pallas_call(
        functools.partial(_outproj_body, gated_norm=z_src is not None),
        grid_spec=pltpu.PrefetchScalarGridSpec(
            num_scalar_prefetch=1, grid=(nseg,), in_specs=in_specs,
            out_specs=pl.BlockSpec((SEG, d), row)),
        out_shape=jax.ShapeDtypeStruct((t, d), F32),
        compiler_params=_cparams(1),
        name="mixer_out_proj",
    )(mrow, *args)


def _qkprep_body(lat_ref, pos_ref, q_ref, kv_ref, qw_ref, kw_ref, cos_ref, sin_ref,
                 qo_ref, ko_ref, vo_ref, kn_ref, vn_ref):
    del pos_ref
    is_lat = lat_ref[pl.program_id(0)] > 0
    r = lax.broadcasted_iota(jnp.int32, (LANES, LANES), 0) // HEAD_DIM
    c = lax.broadcasted_iota(jnp.int32, (LANES, LANES), 1) // HEAD_DIM
    head_mean = jnp.where(r == c, 1.0 / HEAD_DIM, 0.0).astype(F32)
    lane = lax.broadcasted_iota(jnp.int32, (SEG, LANES), 1)
    first_half = (lane % HEAD_DIM) < (HEAD_DIM // 2)
    cos = cos_ref[...]
    sin = sin_ref[...]

    def norm_rope(xs, w):
        ms = jnp.dot(xs * xs, head_mean, precision=HIGHEST, preferred_element_type=F32)
        xn = xs * lax.rsqrt(ms + NORM_EPS) * w
        rot = jnp.where(first_half, pltpu.roll(xn, LANES - HEAD_DIM // 2, 1),
                        pltpu.roll(xn, HEAD_DIM // 2, 1))
        return xn, jnp.where(is_lat, xn * cos + rot * sin, xn)

    scale = HEAD_DIM ** -0.5 * LOG2_E
    for c0 in range(0, q_ref.shape[1], LANES):
        _, qr = norm_rope(q_ref[:, c0:c0 + LANES].astype(F32), qw_ref[...])
        qo_ref[:, c0:c0 + LANES] = (qr * scale).astype(qo_ref.dtype)
    kvw = ko_ref.shape[1]
    for c0 in range(0, kvw, LANES):
        kn, kr = norm_rope(kv_ref[:, c0:c0 + LANES], kw_ref[...])
        kn_ref[:, c0:c0 + LANES] = kn
        ko_ref[:, c0:c0 + LANES] = kr.astype(ko_ref.dtype)
    v = kv_ref[:, kvw:]
    vn_ref[...] = v
    vo_ref[...] = v.T.astype(vo_ref.dtype)


def _qk_prep(q, kv, meta, q_norm_w, k_norm_w, cos_t, sin_t):
    t, qd = q.shape
    kvd = kv.shape[1] // 2
    nseg = t // SEG
    row = lambda i, lat, pos: (i, 0)
    tab = lambda i, lat, pos: (pos[i] * lat[i], 0)
    tile2 = lambda w: jnp.tile(w, LANES // HEAD_DIM).reshape(1, LANES)
    per_chunk = ATT_TK // SEG
    return pl.pallas_call(
        _qkprep_body,
        grid_spec=pltpu.PrefetchScalarGridSpec(
            num_scalar_prefetch=2, grid=(nseg,),
            in_specs=[pl.BlockSpec((SEG, qd), row), pl.BlockSpec((SEG, 2 * kvd), row),
                      pl.BlockSpec((1, LANES), lambda i, lat, pos: (0, 0)),
                      pl.BlockSpec((1, LANES), lambda i, lat, pos: (0, 0)),
                      pl.BlockSpec((SEG, LANES), tab), pl.BlockSpec((SEG, LANES), tab)],
            out_specs=[pl.BlockSpec((SEG, qd), row), pl.BlockSpec((SEG, kvd), row),
                       pl.BlockSpec((None, kvd, SEG), lambda i, lat, pos: (i // per_chunk, 0, i % per_chunk)),
                       pl.BlockSpec((SEG, kvd), row), pl.BlockSpec((SEG, kvd), row)]),
        out_shape=[jax.ShapeDtypeStruct((t, qd), BF16), jax.ShapeDtypeStruct((t, kvd), BF16),
                   jax.ShapeDtypeStruct((t // ATT_TK, kvd, ATT_TK), BF16),
                   jax.ShapeDtypeStruct((t, kvd), F32), jax.ShapeDtypeStruct((t, kvd), F32)],
        compiler_params=_cparams(1),
        name="qk_norm_rope",
    )(meta["lat"], meta["pos"], q, kv, tile2(q_norm_w), tile2(k_norm_w), cos_t, sin_t)


def _rope_tables(seq_len):
    rows = seq_len // GRID_W
    rowp = jnp.repeat(jnp.arange(rows), GRID_W).astype(F32)
    colp = jnp.tile(jnp.arange(GRID_W), rows).astype(F32)
    axis = HEAD_DIM // 2
    inv = ROPE_THETA ** (-jnp.arange(0, axis, 2, dtype=F32) / axis)
    ang = jnp.concatenate([rowp[:, None] * inv, colp[:, None] * inv], axis=-1)
    cos, sin = jnp.cos(ang), jnp.sin(ang)
    cos_h = jnp.concatenate([cos, cos], axis=-1)
    sin_h = jnp.concatenate([-sin, sin], axis=-1)
    rep = LANES // HEAD_DIM
    return jnp.tile(cos_h, (1, rep)), jnp.tile(sin_h, (1, rep))


ONES_ROWS = 16


def _attn_body(*refs, chunked, tq, aliased):
    n_src = len(chunked)
    q_ref = refs[0]
    kv_refs = refs[1:1 + 2 * n_src]
    pos = 1 + 2 * n_src + (1 if aliased else 0)
    o_ref = refs[pos]
    qs_scr, acc_scr, m_scr = refs[pos + 1:]
    kvw = KV_HEADS * HEAD_DIM
    lane_g = lax.broadcasted_iota(jnp.int32, (tq, kvw), 1) // HEAD_DIM

    for g in range(KV_HEADS):
        for r in range(Q_PER_KV):
            qr = q_ref[:, kvw * r:kvw * (r + 1)]
            qs_scr[g, r * tq:(r + 1) * tq, :] = jnp.where(lane_g == g, qr, jnp.zeros_like(qr))
    m_scr[...] = jnp.full_like(m_scr, NEG_BIG)
    acc_scr[...] = jnp.zeros_like(acc_scr)

    units = [(g, c) for g in range(KV_HEADS) for c in range(Q_PER_KV * tq // ATT_STRIP)]

    def step(kc, vt):
        ones = jnp.ones((ONES_ROWS, kc.shape[0]), BF16)

        def scores(u):
            g, c = u
            return lax.dot_general(kc, qs_scr[g, c * ATT_STRIP:(c + 1) * ATT_STRIP, :], NT_DIMS,
                                   preferred_element_type=F32)

        def softmax(u, s):
            g, c = u
            cols = slice(c * ATT_STRIP, (c + 1) * ATT_STRIP)
            m_prev = m_scr[g, :, cols]
            m_new = jnp.maximum(m_prev, jnp.max(s, axis=0, keepdims=True))
            m_scr[g, :, cols] = m_new
            return jnp.exp2(s - m_new).astype(BF16), jnp.exp2(m_prev - m_new)

        def accumulate(u, p, alpha):
            g, c = u
            cols = slice(c * ATT_STRIP, (c + 1) * ATT_STRIP)
            lhs = jnp.concatenate([vt[HEAD_DIM * g:HEAD_DIM * (g + 1), :], ones], axis=0)
            pv = jnp.dot(lhs, p, preferred_element_type=F32)
            acc_scr[g, :, cols] = acc_scr[g, :, cols] * alpha + pv

        s_cur = scores(units[0])
        pending = None
        for i, u in enumerate(units):
            s_next = scores(units[i + 1]) if i + 1 < len(units) else None
            if pending is not None:
                accumulate(*pending)
            pending = (u,) + softmax(u, s_cur)
            s_cur = s_next
        accumulate(*pending)

    for si, is_chunked in enumerate(chunked):
        k_ref, vt_ref = kv_refs[2 * si], kv_refs[2 * si + 1]
        if not is_chunked:
            step(k_ref[...], vt_ref[...])
        else:
            tk = vt_ref.shape[2]

            def body(j, carry, k_ref=k_ref, vt_ref=vt_ref, tk=tk):
                rows = pl.ds(pl.multiple_of(j * tk, tk), tk)
                step(k_ref[rows, :], vt_ref[j])
                return carry
            lax.fori_loop(0, vt_ref.shape[0], body, 0)

    for r in range(Q_PER_KV):
        cols = slice(r * tq, (r + 1) * tq)
        parts = [acc_scr[g, 0:HEAD_DIM, cols] / acc_scr[g, HEAD_DIM:HEAD_DIM + 1, cols]
                 for g in range(KV_HEADS)]
        o_ref[:, kvw * r:kvw * (r + 1)] = jnp.concatenate(parts, axis=0).T.astype(o_ref.dtype)


def _attn_call(q, srcs, q_blk0, n_seq, q_len, att_prev, name):
    t, qd = q.shape
    tq = SEG
    kvw = KV_HEADS * HEAD_DIM
    nq = q_len // tq
    qmap = lambda s, j: (q_blk0 + s * nq + j, 0)
    in_specs = [pl.BlockSpec((tq, qd), qmap)]
    args = [q]
    for k, k_spec, vt, vt_spec, _ in srcs:
        in_specs += [k_spec, vt_spec]
        args += [k, vt]
    aliases = {}
    if att_prev is not None:
        in_specs.append(pl.BlockSpec(memory_space=pl.ANY))
        args.append(att_prev)
        aliases = {len(args) - 1: 0}
    rows4 = Q_PER_KV * tq
    return pl.pallas_call(
        functools.partial(_attn_body, chunked=tuple(s[4] for s in srcs), tq=tq,
                          aliased=att_prev is not None),
        grid=(n_seq, nq),
        in_specs=in_specs,
        out_specs=pl.BlockSpec((tq, qd), qmap),
        out_shape=jax.ShapeDtypeStruct((t, qd), BF16),
        scratch_shapes=[pltpu.VMEM((KV_HEADS, rows4, kvw), BF16),
                        pltpu.VMEM((KV_HEADS, HEAD_DIM + ONES_ROWS, rows4), F32),
                        pltpu.VMEM((KV_HEADS, 1, rows4), F32)],
        input_output_aliases=aliases,
        compiler_params=_cparams(2),
        name=name,
    )(*args)


def _ffnpre_body(mrow_ref, x_ref, nw_ref, sh_ref, sc_ref, rw_ref, rb_ref,
                 h_ref, ti_ref, gt_ref, rk_ref, cnt_ref, cnt_scr):
    del mrow_ref

    @pl.when(pl.program_id(0) == 0)
    def _():
        cnt_scr[...] = jnp.zeros_like(cnt_scr)

    h = _normed(x_ref[...], nw_ref, sh_ref, sc_ref)
    hb = h.astype(BF16)
    h_ref[...] = hb
    logits = jnp.dot(hb, rw_ref[...], preferred_element_type=F32) + rb_ref[...]
    lane = lax.broadcasted_iota(jnp.int32, logits.shape, 1)
    work = logits
    vals, idxs = [], []
    for _ in range(TOP_K):
        m = jnp.max(work, axis=1, keepdims=True)
        idx = jnp.min(jnp.where(work == m, lane, LANES), axis=1, keepdims=True)
        vals.append(m)
        idxs.append(idx)
        work = jnp.where(lane == idx, NEG_BIG, work)
    es = [jnp.exp(v - vals[0]) for v in vals]
    den = es[0]
    for e in es[1:]:
        den = den + e
    ti = jnp.zeros(logits.shape, jnp.int32)
    gt = jnp.zeros(logits.shape, F32)
    for k in range(TOP_K):
        ti = jnp.where(lane == k, idxs[k], ti)
        gt = jnp.where(lane == k, es[k] / den, gt)
    ti_ref[...] = ti
    gt_ref[...] = gt

    onehot = jnp.zeros(logits.shape, F32)
    for k in range(TOP_K):
        onehot = onehot + (lane == idxs[k]).astype(F32)
    n = logits.shape[0]
    earlier = (lax.broadcasted_iota(jnp.int32, (n, n), 0) > lax.broadcasted_iota(jnp.int32, (n, n), 1))
    before = jnp.dot(earlier.astype(BF16), onehot.astype(BF16), preferred_element_type=F32) + cnt_scr[...]
    rk = jnp.zeros(logits.shape, jnp.int32)
    for k in range(TOP_K):
        r_k = jnp.sum(jnp.where(lane == idxs[k], before, 0.0), axis=1, keepdims=True)
        rk = jnp.where(lane == k, r_k.astype(jnp.int32), rk)
    rk_ref[...] = rk
    cnt_scr[...] = cnt_scr[...] + jnp.sum(onehot, axis=0, keepdims=True)
    cnt_ref[...] = cnt_scr[...]


def _ffn_pre(x, mrow, norm_w, mods, router_w, router_b):
    t, d = x.shape
    nseg = t // SEG
    ne = router_w.shape[1]
    rw = jnp.zeros((d, LANES), F32).at[:, :ne].set(router_w).astype(BF16)
    rb = jnp.full((1, LANES), NEG_BIG, F32).at[0, :ne].set(router_b)
    row = lambda i, mrow: (i, 0)
    const = lambda i, mrow: (0, 0)
    return pl.pallas_call(
        _ffnpre_body,
        grid_spec=pltpu.PrefetchScalarGridSpec(
            num_scalar_prefetch=1, grid=(nseg,),
            in_specs=[pl.BlockSpec((SEG, d), row), pl.BlockSpec((1, d), const),
                      _mod_spec(d, 3), _mod_spec(d, 4),
                      pl.BlockSpec((d, LANES), const), pl.BlockSpec((1, LANES), const)],
            out_specs=[pl.BlockSpec((SEG, d), row), pl.BlockSpec((SEG, LANES), row),
                       pl.BlockSpec((SEG, LANES), row), pl.BlockSpec((SEG, LANES), row),
                       pl.BlockSpec((1, LANES), const)],
            scratch_shapes=[pltpu.VMEM((1, LANES), F32)]),
        out_shape=[jax.ShapeDtypeStruct((t, d), BF16), jax.ShapeDtypeStruct((t, LANES), jnp.int32),
                   jax.ShapeDtypeStruct((t, LANES), F32), jax.ShapeDtypeStruct((t, LANES), jnp.int32),
                   jax.ShapeDtypeStruct((1, LANES), F32)],
        compiler_params=_cparams(1),
        name="ffn_norm_router_topk",
    )(mrow, x, norm_w.reshape(1, d), mods, mods, rw, rb)


def _experts_body(be_ref, nu_ref, x_ref, wgu_ref, bgu_ref, wdn_ref, bdn_ref, o_ref, wgu_bf, wdn_bf):
    i = pl.program_id(0)
    live = i < nu_ref[0]
    new_expert = jnp.logical_or(i == 0, be_ref[i] != be_ref[jnp.maximum(i - 1, 0)])

    @pl.when(jnp.logical_and(live, new_expert))
    def _():
        wgu_bf[...] = wgu_ref[...].astype(BF16)
        wdn_bf[...] = wdn_ref[...].astype(BF16)

    @pl.when(live)
    def _():
        hg = jnp.dot(x_ref[...], wgu_bf[...], preferred_element_type=F32) + bgu_ref[...]
        dff = hg.shape[1] // 2
        g = jnp.minimum(hg[:, :dff], SWIGLU_LIMIT)
        u = jnp.clip(hg[:, dff:], -SWIGLU_LIMIT, SWIGLU_LIMIT)
        act = g * jax.nn.sigmoid(SWIGLU_ALPHA * g) * (u + 1.0)
        y = jnp.dot(act.astype(BF16), wdn_bf[...], preferred_element_type=F32) + bdn_ref[...]
        o_ref[...] = y.astype(o_ref.dtype)

    @pl.when(i >= nu_ref[0])
    def _():
        o_ref[...] = jnp.zeros_like(o_ref)


def _experts(xg, block_e, n_used, layer, w_gu, b_gu, w_dn, b_dn):
    n_slots, d = xg.shape
    n_blocks = n_slots // MOE_ROWS
    depth, ne, _, two_f = w_gu.shape
    dff = two_f // 2
    b_gu = b_gu.reshape(depth, ne, 1, two_f)
    b_dn = b_dn.reshape(depth, ne, 1, d)
    return pl.pallas_call(
        _experts_body,
        grid_spec=pltpu.PrefetchScalarGridSpec(
            num_scalar_prefetch=2, grid=(n_blocks,),
            in_specs=[pl.BlockSpec((MOE_ROWS, d), lambda i, be, nu: (i, 0)),
                      pl.BlockSpec((None, None, d, two_f), lambda i, be, nu: (layer, be[i], 0, 0)),
                      pl.BlockSpec((None, None, 1, two_f), lambda i, be, nu: (layer, be[i], 0, 0)),
                      pl.BlockSpec((None, None, dff, d), lambda i, be, nu: (layer, be[i], 0, 0)),
                      pl.BlockSpec((None, None, 1, d), lambda i, be, nu: (layer, be[i], 0, 0))],
            out_specs=pl.BlockSpec((MOE_ROWS, d), lambda i, be, nu: (i, 0)),
            scratch_shapes=[pltpu.VMEM((d, two_f), BF16), pltpu.VMEM((dff, d), BF16)]),
        out_shape=jax.ShapeDtypeStruct((n_slots, d), BF16),
        compiler_params=_cparams(1),
        name="expert_swiglu",
    )(block_e, n_used, xg, w_gu, b_gu, w_dn, b_dn)


def _combine_body(mrow_ref, x_ref, y0_ref, y1_ref, y2_ref, y3_ref, gt_ref, g2_ref, o_ref):
    del mrow_ref
    gt = gt_ref[...]
    f = None
    for k, y_ref in enumerate((y0_ref, y1_ref, y2_ref, y3_ref)):
        term = y_ref[...].astype(F32) * gt[:, k:k + 1]
        f = term if f is None else f + term
    o_ref[...] = x_ref[...] + g2_ref[0] * f


def _combine(x, yg, gates, mrow, mods):
    t, d = x.shape
    nseg = t // SEG
    row = lambda i, mrow: (i, 0)
    assert TOP_K == 4
    in_specs = [pl.BlockSpec((SEG, d), row)]
    in_specs += [pl.BlockSpec((SEG, d), lambda i, mrow, k=k: (k * nseg + i, 0)) for k in range(TOP_K)]
    in_specs += [pl.BlockSpec((SEG, LANES), row), _mod_spec(d, 5)]
    return pl.pallas_call(
        _combine_body,
        grid_spec=pltpu.PrefetchScalarGridSpec(
            num_scalar_prefetch=1, grid=(nseg,), in_specs=in_specs,
            out_specs=pl.BlockSpec((SEG, d), row)),
        out_shape=jax.ShapeDtypeStruct((t, d), F32),
        compiler_params=_cparams(1),
        name="moe_combine_residual",
    )(mrow, x, yg, yg, yg, yg, gates, mods)


def _moe(x, h2, top_i, gates, rank, counts, mrow, mods, layer, w_gu, b_gu, w_dn, b_dn):
    t, d = x.shape
    tk = t * TOP_K
    bm = MOE_ROWS
    counts = counts.astype(jnp.int32)
    padded = (counts + bm - 1) // bm * bm
    pad_end = jnp.cumsum(padded)
    pad_start = pad_end - padded
    start = jnp.cumsum(counts) - counts
    experts = jnp.arange(N_EXPERTS, dtype=jnp.int32)
    ti4 = top_i[:, :TOP_K]
    dest = rank[:, :TOP_K] + jnp.sum(jnp.where(ti4[:, :, None] == experts, pad_start, 0), axis=-1)
    n_blocks = (tk + N_EXPERTS * (bm - 1) + bm - 1) // bm
    blk_start = jnp.arange(n_blocks, dtype=jnp.int32) * bm
    block_e = jnp.minimum(jnp.sum(pad_end[None, :] <= blk_start[:, None], axis=1),
                          N_EXPERTS - 1).astype(jnp.int32)
    n_used = (pad_end[-1:] // bm).astype(jnp.int32)
    order = jnp.argsort(ti4.reshape(tk), stable=True).astype(jnp.int32)
    off = (blk_start - pad_start[block_e])[:, None] + jnp.arange(bm, dtype=jnp.int32)
    src = jnp.where(off < counts[block_e][:, None], start[block_e][:, None] + off, 0)
    slot_tok = order[src.reshape(n_blocks * bm)] // TOP_K
    xg = h2[slot_tok]
    yb = _experts(xg, block_e, n_used, layer, w_gu, b_gu, w_dn, b_dn)
    yg = yb[dest.T.reshape(tk)]
    return _combine(x, yg, gates, mrow, mods)


def _final_body(x_ref, w_ref, o_ref):
    x = x_ref[...]
    ms = jnp.mean(x * x, axis=-1, keepdims=True)
    o_ref[...] = x * lax.rsqrt(ms + NORM_EPS) * w_ref[...]


def _final_norm(x, w):
    t, d = x.shape
    return pl.pallas_call(
        _final_body,
        grid=(t // SEG,),
        in_specs=[pl.BlockSpec((SEG, d), lambda i: (i, 0)), pl.BlockSpec((1, d), lambda i: (0, 0))],
        out_specs=pl.BlockSpec((SEG, d), lambda i: (i, 0)),
        out_shape=jax.ShapeDtypeStruct((t, d), F32),
        compiler_params=_cparams(1),
        name="final_rmsnorm",
    )(x, w.reshape(1, d))


def _segment_meta(n_ctx, sp, n_lat, ss):
    mrow, first, last, pos, lat = [], [], [], [], []
    for n_seq, length, is_lat in ((n_ctx, sp, 0), (n_lat, ss, 1)):
        per = length // SEG
        for s in range(n_seq):
            for j in range(per):
                mrow.append(1 + s if is_lat else 0)
                first.append(int(j == 0))
                last.append(int(j == per - 1))
                pos.append(j)
                lat.append(is_lat)
    as_arr = lambda v: jnp.asarray(np.asarray(v, np.int32))
    return dict(mrow=as_arr(mrow), first=as_arr(first), last=as_arr(last), pos=as_arr(pos),
                lat=as_arr(lat), mrow_np=np.asarray(mrow, np.int32))


def kernel(x_prompt, x_sample, state_ssd, cache_k, cache_v, c, c_ctx, w_mod, b_mod, norm_mix_w, norm_ffn_w, ev_w_in, ev_conv_w, ev_conv_b, ev_dt_bias, ev_a_log, ev_d_skip, ev_norm_w, ev_sconv_w, ev_w_out, od_w_in, od_q_norm_w, od_k_norm_w, od_dw_w, od_dw_b, od_ln_w, od_ln_b, od_w_out, router_w, router_b, w_gu, b_gu, w_dn, b_dn, final_norm_w):
    bp, sp, d = x_prompt.shape
    bs, ss, _ = x_sample.shape
    depth = w_mod.shape[0]
    n_ctx_tok = bp * sp
    t = n_ctx_tok + bs * ss
    past = cache_k.shape[2]
    kvw = KV_HEADS * HEAD_DIM
    assert sp % SEG == 0 and ss % SEG == 0 and n_ctx_tok % ss == 0 and past % SEG == 0
    assert sp % SSD_CHUNK == 0 and ss % ATT_TK == 0 and ATT_TK % sp == 0 and d ==SSD_GROUPS * SSD_HEADS_PER_GROUP * SSD_HEAD_DIM

    meta = _segment_meta(bp, sp, bs, ss)
    mrow = meta["mrow"]
    x = jnp.concatenate([x_prompt.reshape(n_ctx_tok, d), x_sample.reshape(bs * ss, d)], axis=0)

    n_rows = 16
    cvec = jnp.zeros((n_rows, d), F32).at[0].set(c_ctx).at[1:1 + bs].set(c)
    mods_all = _modulation(cvec, w_mod, b_mod)
    cos_t, sin_t = _rope_tables(ss)

    n_heads = SSD_GROUPS * SSD_HEADS_PER_GROUP
    conv_dim = d + 2 * SSD_GROUPS * SSD_STATE
    states, ctx_k, ctx_v = [], [], []
    for l in range(depth):
        mods = mods_all[l].reshape(n_rows, 1, 6 * d)
        i = l // 2
        if l % 2 == 0:
            w_in = ev_w_in[i]
            o_dt = d + conv_dim
            w_main = jnp.concatenate([w_in[:, :o_dt], w_in[:, o_dt + 2 * n_heads:]], axis=1).astype(BF16)
            regroup = lambda v: v.reshape(v.shape[:-1] + (2, SSD_GROUPS, SSD_HEADS_PER_GROUP)).swapaxes(-3, -2)
            pad_lanes = lambda v: jnp.zeros(v.shape[:-3] + (SSD_GROUPS, LANES), F32).at[..., :2 * SSD_HEADS_PER_GROUP].set(
                v.reshape(v.shape[:-3] + (SSD_GROUPS, 2 * SSD_HEADS_PER_GROUP)))
            w_dt = pad_lanes(regroup(w_in[:, o_dt:o_dt + 2 * n_heads])).reshape(d, SSD_GROUPS * LANES).astype(BF16)
            bias = pad_lanes(regroup(ev_dt_bias[i].reshape(2 * n_heads))).reshape(SSD_GROUPS, 1, LANES)
            alog = pad_lanes(regroup(ev_a_log[i].reshape(2 * n_heads))).reshape(SSD_GROUPS, 1, LANES)
            dsk = jnp.repeat(ev_d_skip[i], SSD_HEAD_DIM).reshape(1, d)

            main, dtp = _fused_proj(x, mrow, norm_mix_w[l], mods, 0, [w_main, w_dt], [BF16, F32])
            xbc = _ssd_conv(main, meta, ev_conv_w[i], ev_conv_b[i], d, conv_dim)
            y, st = _ssd_call(xbc, dtp, bias, alog, dsk, sp, bp, 0, None, i, None)
            (y,) = _ssd_call(xbc, dtp, bias, alog, dsk, ss, bs, n_ctx_tok // ss, state_ssd, i, y)
            states.append(st)
            yc = _short_conv(main, meta, ev_sconv_w[i], o_dt, o_dt + d, o_dt + 2 * d, d)
            x = _out_proj(y, yc, ev_w_out[i].astype(BF16), x, mrow, mods, 2, z_src=main, norm_w=ev_norm_w[i])
        else:
            w_in = od_w_in[i]
            wq = w_in[:, :d].reshape(d, KV_HEADS, Q_PER_KV, HEAD_DIM).swapaxes(1, 2).reshape(d, d).astype(BF16)
            wkv = w_in[:, d:d + 2 * kvw].astype(BF16)
            wglu = w_in[:, d + 2 * kvw:].astype(BF16)
            w_out = od_w_out[i]
            w_att = w_out[:d].reshape(KV_HEADS, Q_PER_KV, HEAD_DIM, d).swapaxes(0, 1).reshape(d, d)
            w_out_p = jnp.concatenate([w_att, w_out[d:]], axis=0).astype(BF16)

            q, kv, glu = _fused_proj(x, mrow, norm_mix_w[l], mods, 0, [wq, wkv, wglu], [BF16, F32, BF16])
            qn, k_att, vt_att, k_n, v_n = _qk_prep(q, kv, meta, od_q_norm_w[i], od_k_norm_w[i], cos_t, sin_t)
            ctx_k.append(k_n[:n_ctx_tok].reshape(bp, sp, KV_HEADS, HEAD_DIM))
            ctx_v.append(v_n[:n_ctx_tok].reshape(bp, sp, KV_HEADS, HEAD_DIM))
            ck = cache_k[:, i].reshape(bs * past, kvw).astype(BF16)
            cvt = cache_v[:, i].reshape(bs, past, kvw).swapaxes(1, 2).astype(BF16)
            per = ATT_TK // sp
            ctx_src = (k_att, pl.BlockSpec((sp, kvw), lambda s, j: (s, 0)),
                       vt_att, pl.BlockSpec((None, kvw, sp), lambda s, j: (s // per, 0, s % per)), False)
            cache_src = (ck, pl.BlockSpec((past, kvw), lambda s, j: (s, 0)),
                         cvt, pl.BlockSpec((None, kvw, past), lambda s, j: (s, 0, 0)), False)
            lat_src = (k_att, pl.BlockSpec((ss, kvw), lambda s, j: (n_ctx_tok // ss + s, 0)),
                       vt_att, pl.BlockSpec((ss // ATT_TK, kvw, ATT_TK),
                                            lambda s, j: (n_ctx_tok // ss + s, 0, 0)), True)
            att = _attn_call(qn, [ctx_src], 0, bp, sp, None, "attention_ctx")
            att = _attn_call(qn, [cache_src, lat_src], n_ctx_tok // SEG, bs, ss, att, "attention_latent")
            u = _conformer_conv(glu, meta, od_dw_w[i], od_dw_b[i], od_ln_w[i], od_ln_b[i])
            x = _out_proj(att, u, w_out_p, x, mrow, mods, 2)

        h2, ti, gt, rk, cnt = _ffn_pre(x, mrow, norm_ffn_w[l], mods, router_w[l], router_b[l])
        x = _moe(x, h2, ti, gt, rk, cnt[0, :N_EXPERTS], mrow, mods, l, w_gu, b_gu, w_dn, b_dn)

    y = _final_norm(x, final_norm_w)
    y_prompt = y[:n_ctx_tok].reshape(bp, sp, d)
    y_sample = y[n_ctx_tok:].reshape(bs, ss, d)
    return (y_prompt, y_sample, jnp.stack(states, axis=1), jnp.stack(ctx_k, axis=1), jnp.stack(ctx_v, axis=1))
```

```python
import functools

import numpy as np
import jax
import jax.numpy as jnp
from jax import lax
from jax.experimental import pallas as pl
from jax.experimental.pallas import tpu as pltpu

F32 = jnp.float32
BF16 = jnp.bfloat16
HIGHEST = lax.Precision.HIGHEST

NORM_EPS = 1e-6
GRID_W = 64
ROPE_THETA = 10000.0
HEAD_DIM = 64
KV_HEADS = 4
Q_PER_KV = 4
SSD_HEAD_DIM = 64
SSD_GROUPS = 4
SSD_HEADS_PER_GROUP = 4
SSD_STATE = 128
SSD_CHUNK = 128
SSD_CONV = 4
SCONV_K = 3
CONF_K = 31
N_EXPERTS = 32
TOP_K = 4
SWIGLU_LIMIT = 7.0
SWIGLU_ALPHA = 1.702

SEG = 256
HALO = 16
LANES = 128
SUBLANES = 8
MOE_ROWS = 256
ATT_TK = 512
ATT_STRIP = 512
NEG_BIG = -1e30
LOG2_E = 1.4426950408889634
VMEM_LIMIT = 56 * 1024 * 1024

NT_DIMS = (((1,), (1,)), ((), ()))
TN_DIMS = (((0,), (0,)), ((), ()))


def _cparams(n_grid):
    return pltpu.CompilerParams(dimension_semantics=("arbitrary",) * n_grid,
                                vmem_limit_bytes=VMEM_LIMIT)


def _resident(shape):
    nd = len(shape)
    return pl.BlockSpec(shape, lambda *_: (0,) * nd, pipeline_mode=pl.Buffered(1))


def _silu(x):
    return x * jax.nn.sigmoid(x)


def _mod_body(c_ref, w_ref, b_ref, o_ref):
    c = c_ref[...]
    s = _silu(c).astype(BF16)
    o_ref[0] = jnp.dot(s, w_ref[0].astype(BF16), preferred_element_type=F32) + b_ref[0]


def _modulation(cvec, w_mod, b_mod):
    depth, d, n = w_mod.shape
    rows = cvec.shape[0]
    tn = 1536
    assert n % tn == 0
    return pl.pallas_call(
        _mod_body,
        grid=(depth, n // tn),
        in_specs=[pl.BlockSpec((rows, d), lambda l, j: (0, 0)),
                  pl.BlockSpec((1, d, tn), lambda l, j: (l, 0, j)),
                  pl.BlockSpec((1, 1, tn), lambda l, j: (l, 0, j))],
        out_specs=pl.BlockSpec((1, rows, tn), lambda l, j: (l, 0, j)),
        out_shape=jax.ShapeDtypeStruct((depth, rows, n), F32),
        compiler_params=_cparams(2),
        name="modulation",
    )(cvec, w_mod, b_mod.reshape(depth, 1, n))


def _normed(x, nw_ref, sh_ref, sc_ref):
    ms = jnp.mean(x * x, axis=-1, keepdims=True)
    h = x * lax.rsqrt(ms + NORM_EPS) * nw_ref[...]
    return h * (1.0 + sc_ref[0]) + sh_ref[0]


def _proj_body(mrow_ref, x_ref, nw_ref, sh_ref, sc_ref, *rest, n_out, col_chunk):
    del mrow_ref
    w_refs, o_refs = rest[:n_out], rest[n_out:]
    hb = _normed(x_ref[...], nw_ref, sh_ref, sc_ref).astype(BF16)
    for w_ref, o_ref in zip(w_refs, o_refs):
        n = w_ref.shape[1]
        for c0 in range(0, n, col_chunk):
            c1 = min(n, c0 + col_chunk)
            o_ref[:, c0:c1] = jnp.dot(hb, w_ref[:, c0:c1],
                                      preferred_element_type=F32).astype(o_ref.dtype)


def _mod_spec(d, chunk):
    return pl.BlockSpec((1, 1, d), lambda i, mrow: (mrow[i], 0, chunk))


def _fused_proj(x, mrow, norm_w, mods, shift_chunk, weights, out_dtypes):
    t, d = x.shape
    nseg = t // SEG
    n_out = len(weights)
    in_specs = [pl.BlockSpec((SEG, d), lambda i, mrow: (i, 0)),
                pl.BlockSpec((1, d), lambda i, mrow: (0, 0)),
                _mod_spec(d, shift_chunk), _mod_spec(d, shift_chunk + 1)]
    in_specs += [_resident(w.shape) for w in weights]
    out_specs = [pl.BlockSpec((SEG, w.shape[1]), lambda i, mrow: (i, 0)) for w in weights]
    out_shape = [jax.ShapeDtypeStruct((t, w.shape[1]), dt) for w, dt in zip(weights, out_dtypes)]
    return pl.pallas_call(
        functools.partial(_proj_body, n_out=n_out, col_chunk=512),
        grid_spec=pltpu.PrefetchScalarGridSpec(
            num_scalar_prefetch=1, grid=(nseg,), in_specs=in_specs, out_specs=out_specs),
        out_shape=out_shape,
        compiler_params=_cparams(1),
        name="norm_mod_proj",
    )(mrow, x, norm_w.reshape(1, d), mods, mods, *weights)


def _fill(scr, prev, cur, nxt, keep_prev, keep_next):
    scr[0:HALO, :] = prev * keep_prev
    scr[HALO:HALO + SEG, :] = cur
    scr[HALO + SEG:HALO + SEG + HALO, :] = nxt * keep_next


def _taps(scr, w_ref, c0, k_taps, left):
    acc = None
    for k in range(k_taps):
        term = scr[pl.ds(HALO - left + k, SEG), c0:c0 + LANES] * w_ref[k:k + 1, c0:c0 + LANES]
        acc = term if acc is None else acc + term
    return acc


def _keep(first_ref, last_ref):
    i = pl.program_id(0)
    return (1 - first_ref[i]).astype(F32), (1 - last_ref[i]).astype(F32)


def _ssdconv_body(first_ref, last_ref, p_ref, c_ref, n_ref, w_ref, b_ref, o_ref, scr):
    kp, kn = _keep(first_ref, last_ref)
    _fill(scr, p_ref[...].astype(F32), c_ref[...].astype(F32), n_ref[...].astype(F32), kp, kn)
    for c0 in range(0, o_ref.shape[1], LANES):
        y = _taps(scr, w_ref, c0, SSD_CONV, SSD_CONV // 2) + b_ref[:, c0:c0 + LANES]
        o_ref[:, c0:c0 + LANES] = _silu(y).astype(o_ref.dtype)


def _sconv_body(first_ref, last_ref, gp_ref, gc_ref, gn_ref, xp_ref, xc_ref, xn_ref, gb_ref,
                w_ref, o_ref, scr):
    kp, kn = _keep(first_ref, last_ref)
    f = lambda a, b: a[...].astype(F32) * b[...].astype(F32)
    _fill(scr, f(gp_ref, xp_ref), f(gc_ref, xc_ref), f(gn_ref, xn_ref), kp, kn)
    for c0 in range(0, o_ref.shape[1], LANES):
        y = _taps(scr, w_ref, c0, SCONV_K, SCONV_K // 2)
        o_ref[:, c0:c0 + LANES] = (gb_ref[:, c0:c0 + LANES].astype(F32) * y).astype(o_ref.dtype)


def _conf_body(first_ref, last_ref, ap_ref, ac_ref, an_ref, gp_ref, gc_ref, gn_ref,
               w_ref, b_ref, lnw_ref, lnb_ref, o_ref, scr, u_scr, sh_scr):
    kp, kn = _keep(first_ref, last_ref)
    f = lambda a, g: a[...].astype(F32) * jax.nn.sigmoid(g[...].astype(F32))
    _fill(scr, f(ap_ref, gp_ref), f(ac_ref, gc_ref), f(an_ref, gn_ref), kp, kn)
    span = SEG + 2 * HALO - SUBLANES
    for r in range(SUBLANES):
        sh_scr[r] = scr[pl.ds(r, span), :]
    left = CONF_K // 2
    rows = SEG // 4
    for c0 in range(0, o_ref.shape[1], LANES):
        for r0 in range(0, SEG, rows):
            acc = None
            for k in range(CONF_K):
                off = HALO - left + k
                term = (sh_scr[off % SUBLANES, pl.ds(r0 + off - off % SUBLANES, rows), c0:c0 + LANES]
                        * w_ref[k:k + 1, c0:c0 + LANES])
                acc = term if acc is None else acc + term
            u_scr[r0:r0 + rows, c0:c0 + LANES] = acc + b_ref[:, c0:c0 + LANES]
    u = u_scr[...]
    mu = jnp.mean(u, axis=-1, keepdims=True)
    uc = u - mu
    var = jnp.mean(uc * uc, axis=-1, keepdims=True)
    y = uc * lax.rsqrt(var + NORM_EPS) * lnw_ref[...] + lnb_ref[...]
    o_ref[...] = _silu(y).astype(o_ref.dtype)


def _halo_specs(t, cw, col_off):
    per = SEG // HALO
    last_blk = t // HALO - 1
    return [
        pl.BlockSpec((HALO, cw), lambda i, j, f, l: (jnp.maximum(i * per - 1, 0), col_off + j)),
        pl.BlockSpec((SEG, cw), lambda i, j, f, l: (i, col_off + j)),
        pl.BlockSpec((HALO, cw), lambda i, j, f, l: (jnp.minimum((i + 1) * per, last_blk), col_off + j)),
    ]


def _conv_call(body, name, meta, t, cw, n_col, in_specs, args, out_cols, scratch):
    nseg = t // SEG
    return pl.pallas_call(
        body,
        grid_spec=pltpu.PrefetchScalarGridSpec(
            num_scalar_prefetch=2, grid=(nseg, n_col), in_specs=in_specs,
            out_specs=pl.BlockSpec((SEG, cw), lambda i, j, f, l: (i, j)),
            scratch_shapes=scratch),
        out_shape=jax.ShapeDtypeStruct((t, out_cols), BF16),
        compiler_params=_cparams(2),
        name=name,
    )(meta["first"], meta["last"], *args)


def _colvec_spec(rows, cw):
    return pl.BlockSpec((rows, cw), lambda i, j, f, l: (0, j))


def _ssd_conv(main, meta, conv_w, conv_b, col0, width):
    t = main.shape[0]
    cw = 512
    in_specs = _halo_specs(t, cw, col0 // cw) + [_colvec_spec(SSD_CONV, cw), _colvec_spec(1, cw)]
    return _conv_call(_ssdconv_body, "ssd_conv_silu", meta, t, cw, width // cw, in_specs,
                      (main, main, main, conv_w, conv_b.reshape(1, width)), width,
                      [pltpu.VMEM((SEG + 2 * HALO, cw), F32)])


def _short_conv(main, meta, sconv_w, col_gb, col_gc, col_xc, width):
    t = main.shape[0]
    cw = 512
    in_specs = (_halo_specs(t, cw, col_gc // cw) + _halo_specs(t, cw, col_xc // cw)
                + [pl.BlockSpec((SEG, cw), lambda i, j, f, l: (i, col_gb // cw + j)),
                   _colvec_spec(SCONV_K, cw)])
    return _conv_call(_sconv_body, "short_gated_conv", meta, t, cw, width // cw, in_specs,
                      (main,) * 7 + (sconv_w,), width, [pltpu.VMEM((SEG + 2 * HALO, cw), F32)])


def _conformer_conv(glu, meta, dw_w, dw_b, ln_w, ln_b):
    t, two_w = glu.shape
    w = two_w // 2
    in_specs = (_halo_specs(t, w, 0) + _halo_specs(t, w, 1)
                + [_colvec_spec(CONF_K, w)] + [_colvec_spec(1, w)] * 3)
    return _conv_call(_conf_body, "conformer_conv", meta, t, w, 1, in_specs,
                      (glu,) * 6 + (dw_w, dw_b.reshape(1, w), ln_w.reshape(1, w), ln_b.reshape(1, w)),
                      w, [pltpu.VMEM((SEG + 2 * HALO, w), F32), pltpu.VMEM((SEG, w), F32),
                       pltpu.VMEM((SUBLANES, SEG + 2 * HALO - SUBLANES, w), F32)])


def _softplus(x):
    return jnp.maximum(x, 0.0) + jnp.log1p(jnp.exp(-jnp.abs(x)))


def _ssd_body(*refs, nc, has_h0, want_state):
    it = iter(refs)
    x_ref, b_ref, c_ref, dt_ref, bias_ref, alog_ref, dsk_ref = (next(it) for _ in range(7))
    h0_ref = next(it) if has_h0 else None
    y_ref = next(it)
    st_ref = next(it) if want_state else None
    ybuf, sf_scr, sb_scr, decf_scr, decb_scr, xsf_scr, xsb_scr, tot_scr = (next(it) for _ in range(8))

    q = SSD_CHUNK
    hp = SSD_HEADS_PER_GROUP
    width = hp * SSD_HEAD_DIM
    row = lax.broadcasted_iota(jnp.int32, (q, q), 0)
    col = lax.broadcasted_iota(jnp.int32, (q, q), 1)
    lower = row >= col
    upper = col >= row
    lower_f = lower.astype(F32)
    lane_w = lax.broadcasted_iota(jnp.int32, (q, width), 1) // SSD_HEAD_DIM
    sub_w = lax.broadcasted_iota(jnp.int32, (width, 1), 0) // SSD_HEAD_DIM

    def per_head_cols(mat, base):
        out = mat[:, base + hp - 1:base + hp]
        for r in range(hp - 2, -1, -1):
            out = jnp.where(lane_w == r, mat[:, base + r:base + r + 1], out)
        return out

    def per_head_rows(rowvec, base):
        out = rowvec[:, base + hp - 1:base + hp]
        for r in range(hp - 2, -1, -1):
            out = jnp.where(sub_w == r, rowvec[:, base + r:base + r + 1], out)
        return out

    a_row = -jnp.exp(alog_ref[0])
    bias_row = bias_ref[0]

    dsk = dsk_ref[...]

    def intra(c, carry):
        rows = pl.ds(pl.multiple_of(c * q, q), q)
        dt = _softplus(dt_ref[rows, :] + bias_row)
        a = dt * a_row
        cs = jnp.dot(lower_f, a, precision=HIGHEST, preferred_element_type=F32)
        ex = cs - a
        tot = cs[q - 1:q, :]
        xb = x_ref[rows, :]
        x = xb.astype(F32)
        cb = lax.dot_general(c_ref[rows, :], b_ref[rows, :], NT_DIMS, preferred_element_type=F32)
        cs_t, ex_t, dt_t = cs.T, ex.T, dt.T
        y = x * dsk
        for r in range(hp):
            dec_f = jnp.exp(jnp.where(lower, cs[:, r:r + 1] - cs_t[r:r + 1, :], NEG_BIG))
            dec_b = jnp.exp(jnp.where(upper, ex_t[hp + r:hp + r + 1, :] - ex[:, hp + r:hp + r + 1], NEG_BIG))
            wm = cb * (dec_f * dt_t[r:r + 1, :] + dec_b * dt_t[hp + r:hp + r + 1, :])
            yd = jnp.dot(wm.astype(BF16), xb, preferred_element_type=F32)
            y = y + jnp.where(lane_w == r, yd, 0.0)
        ybuf[rows, :] = y
        decf_scr[rows, :] = per_head_cols(jnp.exp(cs), 0)
        decb_scr[rows, :] = per_head_cols(jnp.exp(tot - ex), hp)
        xsf_scr[rows, :] = (x * per_head_cols(jnp.exp(tot - cs) * dt, 0)).astype(BF16)
        xsb_scr[rows, :] = (x * per_head_cols(jnp.exp(ex) * dt, hp)).astype(BF16)
        tot_scr[pl.ds(c, 1), :] = jnp.exp(tot)
        return carry

    lax.fori_loop(0, nc, intra, 0, unroll=2)

    if has_h0:
        sf_scr[...] = h0_ref[0].reshape(width, SSD_STATE)
        sb_scr[...] = h0_ref[1].reshape(width, SSD_STATE)
    else:
        sf_scr[...] = jnp.zeros_like(sf_scr)
        sb_scr[...] = jnp.zeros_like(sb_scr)

    def one_direction(c, s_scr, dec_scr, xs_scr, base):
        rows = pl.ds(pl.multiple_of(c * q, q), q)
        s_prev = s_scr[...]
        y_off = lax.dot_general(c_ref[rows, :], s_prev.astype(BF16), NT_DIMS, preferred_element_type=F32)
        ybuf[rows, :] = ybuf[rows, :] + y_off * dec_scr[rows, :]
        contrib = lax.dot_general(xs_scr[rows, :], b_ref[rows, :], TN_DIMS, preferred_element_type=F32)
        s_scr[...] = s_prev * per_head_rows(tot_scr[pl.ds(c, 1), :], base) + contrib

    def recur(i, carry):
        one_direction(i, sf_scr, decf_scr, xsf_scr, 0)
        one_direction(nc - 1 - i, sb_scr, decb_scr, xsb_scr, hp)
        return carry

    lax.fori_loop(0, nc, recur, 0, unroll=2)

    y_ref[...] = ybuf[...].astype(y_ref.dtype)
    if want_state:
        st_ref[0] = sf_scr[...].reshape(hp, SSD_HEAD_DIM, SSD_STATE)
        st_ref[1] = sb_scr[...].reshape(hp, SSD_HEAD_DIM, SSD_STATE)


def _ssd_call(xbc, dtp, bias, alog, dsk, seq_len, n_seq, row_blk0, h0, layer_idx):
    inner = SSD_GROUPS * SSD_HEADS_PER_GROUP * SSD_HEAD_DIM
    width = SSD_HEADS_PER_GROUP * SSD_HEAD_DIM
    nb = inner // SSD_STATE
    nc = seq_len // SSD_CHUNK
    want_state = h0 is None
    in_specs = [
        pl.BlockSpec((seq_len, width), lambda s, g: (row_blk0 + s, g)),
        pl.BlockSpec((seq_len, SSD_STATE), lambda s, g: (row_blk0 + s, nb + g)),
        pl.BlockSpec((seq_len, SSD_STATE), lambda s, g: (row_blk0 + s, nb + SSD_GROUPS + g)),
        pl.BlockSpec((seq_len, LANES), lambda s, g: (row_blk0 + s, g)),
        pl.BlockSpec((1, 1, LANES), lambda s, g: (g, 0, 0)),
        pl.BlockSpec((1, 1, LANES), lambda s, g: (g, 0, 0)),
        pl.BlockSpec((1, width), lambda s, g: (0, g)),
    ]
    args = [xbc, xbc, xbc, dtp, bias, alog, dsk]
    if not want_state:
        in_specs.append(pl.BlockSpec((None, None, 2, SSD_HEADS_PER_GROUP, SSD_HEAD_DIM, SSD_STATE),
                                     lambda s, g: (s, layer_idx, 0, g, 0, 0)))
        args.append(h0)
    out_specs = [pl.BlockSpec((seq_len, width), lambda s, g: (s, g))]
    out_shape = [jax.ShapeDtypeStruct((n_seq * seq_len, inner), BF16)]
    if want_state:
        out_specs.append(pl.BlockSpec((None, 2, SSD_HEADS_PER_GROUP, SSD_HEAD_DIM, SSD_STATE),
                                      lambda s, g: (s, 0, g, 0, 0)))
        out_shape.append(jax.ShapeDtypeStruct(
            (n_seq, 2, SSD_GROUPS * SSD_HEADS_PER_GROUP, SSD_HEAD_DIM, SSD_STATE), F32))
    return pl.pallas_call(
        functools.partial(_ssd_body, nc=nc, has_h0=not want_state, want_state=want_state),
        grid=(n_seq, SSD_GROUPS),
        in_specs=in_specs, out_specs=out_specs, out_shape=out_shape,
        scratch_shapes=[pltpu.VMEM((seq_len, width), F32),
                        pltpu.VMEM((width, SSD_STATE), F32), pltpu.VMEM((width, SSD_STATE), F32),
                        pltpu.VMEM((seq_len, width), F32), pltpu.VMEM((seq_len, width), F32),
                        pltpu.VMEM((seq_len, width), BF16), pltpu.VMEM((seq_len, width), BF16),
                        pltpu.VMEM((max(nc, SUBLANES), LANES), F32)],
        compiler_params=_cparams(2),
        name="ssd_scan_ctx" if want_state else "ssd_scan_latent",
    )(*args)


def _outproj_body(mrow_ref, ac_ref, al_ref, *refs, gated_norm):
    a = jnp.where(mrow_ref[pl.program_id(0)] > 0, al_ref[...], ac_ref[...])
    if gated_norm:
        z_ref, nw_ref, b_ref, w_ref, x_ref, g_ref, o_ref = refs
        y = a.astype(F32) * _silu(z_ref[...].astype(F32))
        ms = jnp.mean(y * y, axis=-1, keepdims=True)
        a = (y * lax.rsqrt(ms + NORM_EPS) * nw_ref[...]).astype(BF16)
    else:
        b_ref, w_ref, x_ref, g_ref, o_ref = refs
    d = a.shape[1]
    out = jnp.dot(a, w_ref[0:d, :], preferred_element_type=F32)
    out = out + jnp.dot(b_ref[...], w_ref[d:, :], preferred_element_type=F32)
    o_ref[...] = x_ref[...] + g_ref[0] * out


def _out_proj(a_ctx, a_lat, b, w, x, mrow, mods, gate_chunk, z_src=None, norm_w=None):
    t, d = x.shape
    nseg = t // SEG
    n_ctx_seg = a_ctx.shape[0] // SEG
    row = lambda i, mrow: (i, 0)
    in_specs = [pl.BlockSpec((SEG, d), lambda i, mrow: (jnp.minimum(i, n_ctx_seg - 1), 0)),
                pl.BlockSpec((SEG, d), lambda i, mrow: (jnp.maximum(i - n_ctx_seg, 0), 0))]
    args = [a_ctx, a_lat]
    if z_src is not None:
        in_specs += [pl.BlockSpec((SEG, d), row), pl.BlockSpec((1, d), lambda i, mrow: (0, 0))]
        args += [z_src, norm_w.reshape(1, d)]
    in_specs += [pl.BlockSpec((SEG, d), row), _resident(w.shape), pl.BlockSpec((SEG, d), row),
                 _mod_spec(d, gate_chunk)]
    args += [b, w, x, mods]
    return pl.pallas_call(
        functools.partial(_outproj_body, gated_norm=z_src is not None),
        grid_spec=pltpu.PrefetchScalarGridSpec(
            num_scalar_prefetch=1, grid=(nseg,), in_specs=in_specs,
            out_specs=pl.BlockSpec((SEG, d), row)),
        out_shape=jax.ShapeDtypeStruct((t, d), F32),
        compiler_params=_cparams(1),
        name="mixer_out_proj",
    )(mrow, *args)


def _qkprep_body(lat_ref, pos_ref, q_ref, kv_ref, qw_ref, kw_ref, cos_ref, sin_ref,
                 qo_ref, ko_ref, vo_ref, kn_ref, vn_ref):
    del pos_ref
    is_lat = lat_ref[pl.program_id(0)] > 0
    r = lax.broadcasted_iota(jnp.int32, (LANES, LANES), 0) // HEAD_DIM
    c = lax.broadcasted_iota(jnp.int32, (LANES, LANES), 1) // HEAD_DIM
    head_mean = jnp.where(r == c, 1.0 / HEAD_DIM, 0.0).astype(F32)
    lane = lax.broadcasted_iota(jnp.int32, (SEG, LANES), 1)
    first_half = (lane % HEAD_DIM) < (HEAD_DIM // 2)
    cos = cos_ref[...]
    sin = sin_ref[...]

    def norm_rope(xs, w):
        ms = jnp.dot(xs * xs, head_mean, precision=HIGHEST, preferred_element_type=F32)
        xn = xs * lax.rsqrt(ms + NORM_EPS) * w
        rot = jnp.where(first_half, pltpu.roll(xn, LANES - HEAD_DIM // 2, 1),
                        pltpu.roll(xn, HEAD_DIM // 2, 1))
        return xn, jnp.where(is_lat, xn * cos + rot * sin, xn)

    scale = HEAD_DIM ** -0.5 * LOG2_E
    for c0 in range(0, q_ref.shape[1], LANES):
        _, qr = norm_rope(q_ref[:, c0:c0 + LANES].astype(F32), qw_ref[...])
        qo_ref[:, c0:c0 + LANES] = (qr * scale).astype(qo_ref.dtype)
    kvw = ko_ref.shape[1]
    for c0 in range(0, kvw, LANES):
        kn, kr = norm_rope(kv_ref[:, c0:c0 + LANES], kw_ref[...])
        kn_ref[:, c0:c0 + LANES] = kn
        ko_ref[:, c0:c0 + LANES] = kr.astype(ko_ref.dtype)
    v = kv_ref[:, kvw:]
    vn_ref[...] = v
    vo_ref[...] = v.T.astype(vo_ref.dtype)


def _qk_prep(q, kv, meta, q_norm_w, k_norm_w, cos_t, sin_t):
    t, qd = q.shape
    kvd = kv.shape[1] // 2
    nseg = t // SEG
    row = lambda i, lat, pos: (i, 0)
    tab = lambda i, lat, pos: (pos[i] * lat[i], 0)
    tile2 = lambda w: jnp.tile(w, LANES // HEAD_DIM).reshape(1, LANES)
    per_chunk = ATT_TK // SEG
    return pl.pallas_call(
        _qkprep_body,
        grid_spec=pltpu.PrefetchScalarGridSpec(
            num_scalar_prefetch=2, grid=(nseg,),
            in_specs=[pl.BlockSpec((SEG, qd), row), pl.BlockSpec((SEG, 2 * kvd), row),
                      pl.BlockSpec((1, LANES), lambda i, lat, pos: (0, 0)),
                      pl.BlockSpec((1, LANES), lambda i, lat, pos: (0, 0)),
                      pl.BlockSpec((SEG, LANES), tab), pl.BlockSpec((SEG, LANES), tab)],
            out_specs=[pl.BlockSpec((SEG, qd), row), pl.BlockSpec((SEG, kvd), row),
                       pl.BlockSpec((None, kvd, SEG), lambda i, lat, pos: (i // per_chunk, 0, i % per_chunk)),
                       pl.BlockSpec((SEG, kvd), row), pl.BlockSpec((SEG, kvd), row)]),
        out_shape=[jax.ShapeDtypeStruct((t, qd), BF16), jax.ShapeDtypeStruct((t, kvd), BF16),
                   jax.ShapeDtypeStruct((t // ATT_TK, kvd, ATT_TK), BF16),
                   jax.ShapeDtypeStruct((t, kvd), F32), jax.ShapeDtypeStruct((t, kvd), F32)],
        compiler_params=_cparams(1),
        name="qk_norm_rope",
    )(meta["lat"], meta["pos"], q, kv, tile2(q_norm_w), tile2(k_norm_w), cos_t, sin_t)


def _rope_tables(seq_len):
    rows = seq_len // GRID_W
    rowp = jnp.repeat(jnp.arange(rows), GRID_W).astype(F32)
    colp = jnp.tile(jnp.arange(GRID_W), rows).astype(F32)
    axis = HEAD_DIM // 2
    inv = ROPE_THETA ** (-jnp.arange(0, axis, 2, dtype=F32) / axis)
    ang = jnp.concatenate([rowp[:, None] * inv, colp[:, None] * inv], axis=-1)
    cos, sin = jnp.cos(ang), jnp.sin(ang)
    cos_h = jnp.concatenate([cos, cos], axis=-1)
    sin_h = jnp.concatenate([-sin, sin], axis=-1)
    rep = LANES // HEAD_DIM
    return jnp.tile(cos_h, (1, rep)), jnp.tile(sin_h, (1, rep))


ONES_ROWS = 16


def _attn_body(*refs, chunked, tq):
    n_src = len(chunked)
    q_ref = refs[0]
    kv_refs = refs[1:1 + 2 * n_src]
    pos = 1 + 2 * n_src
    o_ref = refs[pos]
    qs_scr, acc_scr, m_scr = refs[pos + 1:]
    kvw = KV_HEADS * HEAD_DIM
    lane_g = lax.broadcasted_iota(jnp.int32, (tq, kvw), 1) // HEAD_DIM

    for g in range(KV_HEADS):
        for r in range(Q_PER_KV):
            qr = q_ref[:, kvw * r:kvw * (r + 1)]
            qs_scr[g, r * tq:(r + 1) * tq, :] = jnp.where(lane_g == g, qr, jnp.zeros_like(qr))
    m_scr[...] = jnp.full_like(m_scr, NEG_BIG)
    acc_scr[...] = jnp.zeros_like(acc_scr)

    units = [(g, c) for g in range(KV_HEADS) for c in range(Q_PER_KV * tq // ATT_STRIP)]

    def step(kc, vt):
        ones = jnp.ones((ONES_ROWS, kc.shape[0]), BF16)

        def scores(u):
            g, c = u
            return lax.dot_general(kc, qs_scr[g, c * ATT_STRIP:(c + 1) * ATT_STRIP, :], NT_DIMS,
                                   preferred_element_type=F32)

        def softmax(u, s):
            g, c = u
            cols = slice(c * ATT_STRIP, (c + 1) * ATT_STRIP)
            m_prev = m_scr[g, :, cols]
            m_new = jnp.maximum(m_prev, jnp.max(s, axis=0, keepdims=True))
            m_scr[g, :, cols] = m_new
            return jnp.exp2(s - m_new).astype(BF16), jnp.exp2(m_prev - m_new)

        def accumulate(u, p, alpha):
            g, c = u
            cols = slice(c * ATT_STRIP, (c + 1) * ATT_STRIP)
            lhs = jnp.concatenate([vt[HEAD_DIM * g:HEAD_DIM * (g + 1), :], ones], axis=0)
            pv = jnp.dot(lhs, p, preferred_element_type=F32)
            acc_scr[g, :, cols] = acc_scr[g, :, cols] * alpha + pv

        s_cur = scores(units[0])
        pending = None
        for i, u in enumerate(units):
            s_next = scores(units[i + 1]) if i + 1 < len(units) else None
            if pending is not None:
                accumulate(*pending)
            pending = (u,) + softmax(u, s_cur)
            s_cur = s_next
        accumulate(*pending)

    for si, is_chunked in enumerate(chunked):
        k_ref, vt_ref = kv_refs[2 * si], kv_refs[2 * si + 1]
        if not is_chunked:
            step(k_ref[...], vt_ref[...])
        else:
            tk = vt_ref.shape[2]

            def body(j, carry, k_ref=k_ref, vt_ref=vt_ref, tk=tk):
                rows = pl.ds(pl.multiple_of(j * tk, tk), tk)
                step(k_ref[rows, :], vt_ref[j])
                return carry
            lax.fori_loop(0, vt_ref.shape[0], body, 0)

    for r in range(Q_PER_KV):
        cols = slice(r * tq, (r + 1) * tq)
        parts = [acc_scr[g, 0:HEAD_DIM, cols] / acc_scr[g, HEAD_DIM:HEAD_DIM + 1, cols]
                 for g in range(KV_HEADS)]
        o_ref[:, kvw * r:kvw * (r + 1)] = jnp.concatenate(parts, axis=0).T.astype(o_ref.dtype)


def _attn_call(q, srcs, q_blk0, n_seq, q_len, name):
    qd = q.shape[1]
    tq = SEG
    kvw = KV_HEADS * HEAD_DIM
    nq = q_len // tq
    in_specs = [pl.BlockSpec((tq, qd), lambda s, j: (q_blk0 + s * nq + j, 0))]
    args = [q]
    for k, k_spec, vt, vt_spec, _ in srcs:
        in_specs += [k_spec, vt_spec]
        args += [k, vt]
    rows4 = Q_PER_KV * tq
    return pl.pallas_call(
        functools.partial(_attn_body, chunked=tuple(s[4] for s in srcs), tq=tq),
        grid=(n_seq, nq),
        in_specs=in_specs,
        out_specs=pl.BlockSpec((tq, qd), lambda s, j: (s * nq + j, 0)),
        out_shape=jax.ShapeDtypeStruct((n_seq * q_len, qd), BF16),
        scratch_shapes=[pltpu.VMEM((KV_HEADS, rows4, kvw), BF16),
                        pltpu.VMEM((KV_HEADS, HEAD_DIM + ONES_ROWS, rows4), F32),
                        pltpu.VMEM((KV_HEADS, 1, rows4), F32)],
        compiler_params=_cparams(2),
        name=name,
    )(*args)


def _ffnpre_body(mrow_ref, x_ref, nw_ref, sh_ref, sc_ref, rw_ref, rb_ref,
                 h_ref, ti_ref, gt_ref, rk_ref, cnt_ref, cnt_scr):
    del mrow_ref

    @pl.when(pl.program_id(0) == 0)
    def _():
        cnt_scr[...] = jnp.zeros_like(cnt_scr)

    h = _normed(x_ref[...], nw_ref, sh_ref, sc_ref)
    hb = h.astype(BF16)
    h_ref[...] = hb
    logits = jnp.dot(hb, rw_ref[...], preferred_element_type=F32) + rb_ref[...]
    lane = lax.broadcasted_iota(jnp.int32, logits.shape, 1)
    work = logits
    vals, idxs = [], []
    for _ in range(TOP_K):
        m = jnp.max(work, axis=1, keepdims=True)
        idx = jnp.min(jnp.where(work == m, lane, LANES), axis=1, keepdims=True)
        vals.append(m)
        idxs.append(idx)
        work = jnp.where(lane == idx, NEG_BIG, work)
    es = [jnp.exp(v - vals[0]) for v in vals]
    den = es[0]
    for e in es[1:]:
        den = den + e
    ti = jnp.zeros(logits.shape, jnp.int32)
    gt = jnp.zeros(logits.shape, F32)
    for k in range(TOP_K):
        ti = jnp.where(lane == k, idxs[k], ti)
        gt = jnp.where(lane == k, es[k] / den, gt)
    ti_ref[...] = ti
    gt_ref[...] = gt

    onehot = jnp.zeros(logits.shape, F32)
    for k in range(TOP_K):
        onehot = onehot + (lane == idxs[k]).astype(F32)
    n = logits.shape[0]
    earlier = (lax.broadcasted_iota(jnp.int32, (n, n), 0) > lax.broadcasted_iota(jnp.int32, (n, n), 1))
    before = jnp.dot(earlier.astype(BF16), onehot.astype(BF16), preferred_element_type=F32) + cnt_scr[...]
    rk = jnp.zeros(logits.shape, jnp.int32)
    for k in range(TOP_K):
        r_k = jnp.sum(jnp.where(lane == idxs[k], before, 0.0), axis=1, keepdims=True)
        rk = jnp.where(lane == k, r_k.astype(jnp.int32), rk)
    rk_ref[...] = rk
    cnt_scr[...] = cnt_scr[...] + jnp.sum(onehot, axis=0, keepdims=True)
    cnt_ref[...] = cnt_scr[...]


def _ffn_pre(x, mrow, norm_w, mods, router_w, router_b):
    t, d = x.shape
    nseg = t // SEG
    ne = router_w.shape[1]
    rw = jnp.zeros((d, LANES), F32).at[:, :ne].set(router_w).astype(BF16)
    rb = jnp.full((1, LANES), NEG_BIG, F32).at[0, :ne].set(router_b)
    row = lambda i, mrow: (i, 0)
    const = lambda i, mrow: (0, 0)
    return pl.pallas_call(
        _ffnpre_body,
        grid_spec=pltpu.PrefetchScalarGridSpec(
            num_scalar_prefetch=1, grid=(nseg,),
            in_specs=[pl.BlockSpec((SEG, d), row), pl.BlockSpec((1, d), const),
                      _mod_spec(d, 3), _mod_spec(d, 4),
                      pl.BlockSpec((d, LANES), const), pl.BlockSpec((1, LANES), const)],
            out_specs=[pl.BlockSpec((SEG, d), row), pl.BlockSpec((SEG, LANES), row),
                       pl.BlockSpec((SEG, LANES), row), pl.BlockSpec((SEG, LANES), row),
                       pl.BlockSpec((1, LANES), const)],
            scratch_shapes=[pltpu.VMEM((1, LANES), F32)]),
        out_shape=[jax.ShapeDtypeStruct((t, d), BF16), jax.ShapeDtypeStruct((t, LANES), jnp.int32),
                   jax.ShapeDtypeStruct((t, LANES), F32), jax.ShapeDtypeStruct((t, LANES), jnp.int32),
                   jax.ShapeDtypeStruct((1, LANES), F32)],
        compiler_params=_cparams(1),
        name="ffn_norm_router_topk",
    )(mrow, x, norm_w.reshape(1, d), mods, mods, rw, rb)


def _experts_body(be_ref, nu_ref, x_ref, wgu_ref, bgu_ref, wdn_ref, bdn_ref, o_ref, wgu_bf, wdn_bf):
    i = pl.program_id(0)
    live = i < nu_ref[0]
    new_expert = jnp.logical_or(i == 0, be_ref[i] != be_ref[jnp.maximum(i - 1, 0)])

    @pl.when(jnp.logical_and(live, new_expert))
    def _():
        wgu_bf[...] = wgu_ref[...].astype(BF16)
        wdn_bf[...] = wdn_ref[...].astype(BF16)

    @pl.when(live)
    def _():
        hg = jnp.dot(x_ref[...], wgu_bf[...], preferred_element_type=F32) + bgu_ref[...]
        dff = hg.shape[1] // 2
        g = jnp.minimum(hg[:, :dff], SWIGLU_LIMIT)
        u = jnp.clip(hg[:, dff:], -SWIGLU_LIMIT, SWIGLU_LIMIT)
        act = g * jax.nn.sigmoid(SWIGLU_ALPHA * g) * (u + 1.0)
        y = jnp.dot(act.astype(BF16), wdn_bf[...], preferred_element_type=F32) + bdn_ref[...]
        o_ref[...] = y.astype(o_ref.dtype)

    @pl.when(i >= nu_ref[0])
    def _():
        o_ref[...] = jnp.zeros_like(o_ref)


def _experts(xg, block_e, n_used, layer, w_gu, b_gu, w_dn, b_dn):
    n_slots, d = xg.shape
    n_blocks = n_slots // MOE_ROWS
    depth, ne, _, two_f = w_gu.shape
    dff = two_f // 2
    b_gu = b_gu.reshape(depth, ne, 1, two_f)
    b_dn = b_dn.reshape(depth, ne, 1, d)
    return pl.pallas_call(
        _experts_body,
        grid_spec=pltpu.PrefetchScalarGridSpec(
            num_scalar_prefetch=2, grid=(n_blocks,),
            in_specs=[pl.BlockSpec((MOE_ROWS, d), lambda i, be, nu: (i, 0)),
                      pl.BlockSpec((None, None, d, two_f), lambda i, be, nu: (layer, be[i], 0, 0)),
                      pl.BlockSpec((None, None, 1, two_f), lambda i, be, nu: (layer, be[i], 0, 0)),
                      pl.BlockSpec((None, None, dff, d), lambda i, be, nu: (layer, be[i], 0, 0)),
                      pl.BlockSpec((None, None, 1, d), lambda i, be, nu: (layer, be[i], 0, 0))],
            out_specs=pl.BlockSpec((MOE_ROWS, d), lambda i, be, nu: (i, 0)),
            scratch_shapes=[pltpu.VMEM((d, two_f), BF16), pltpu.VMEM((dff, d), BF16)]),
        out_shape=jax.ShapeDtypeStruct((n_slots, d), BF16),
        compiler_params=_cparams(1),
        name="expert_swiglu",
    )(block_e, n_used, xg, w_gu, b_gu, w_dn, b_dn)


def _combine_body(mrow_ref, x_ref, y0_ref, y1_ref, y2_ref, y3_ref, gt_ref, g2_ref, o_ref):
    del mrow_ref
    gt = gt_ref[...]
    f = None
    for k, y_ref in enumerate((y0_ref, y1_ref, y2_ref, y3_ref)):
        term = y_ref[...].astype(F32) * gt[:, k:k + 1]
        f = term if f is None else f + term
    o_ref[...] = x_ref[...] + g2_ref[0] * f


def _combine(x, yg, gates, mrow, mods):
    t, d = x.shape
    nseg = t // SEG
    row = lambda i, mrow: (i, 0)
    assert TOP_K == 4
    in_specs = [pl.BlockSpec((SEG, d), row)]
    in_specs += [pl.BlockSpec((SEG, d), lambda i, mrow, k=k: (k * nseg + i, 0)) for k in range(TOP_K)]
    in_specs += [pl.BlockSpec((SEG, LANES), row), _mod_spec(d, 5)]
    return pl.pallas_call(
        _combine_body,
        grid_spec=pltpu.PrefetchScalarGridSpec(
            num_scalar_prefetch=1, grid=(nseg,), in_specs=in_specs,
            out_specs=pl.BlockSpec((SEG, d), row)),
        out_shape=jax.ShapeDtypeStruct((t, d), F32),
        compiler_params=_cparams(1),
        name="moe_combine_residual",
    )(mrow, x, yg, yg, yg, yg, gates, mods)


def _moe(x, h2, top_i, gates, rank, counts, mrow, mods, layer, w_gu, b_gu, w_dn, b_dn):
    t, d = x.shape
    tk = t * TOP_K
    bm = MOE_ROWS
    counts = counts.astype(jnp.int32)
    padded = (counts + bm - 1) // bm * bm
    pad_end = jnp.cumsum(padded)
    pad_start = pad_end - padded
    start = jnp.cumsum(counts) - counts
    experts = jnp.arange(N_EXPERTS, dtype=jnp.int32)
    ti4 = top_i[:, :TOP_K]
    dest = rank[:, :TOP_K] + jnp.sum(jnp.where(ti4[:, :, None] == experts, pad_start, 0), axis=-1)
    n_blocks = (tk + N_EXPERTS * (bm - 1) + bm - 1) // bm
    blk_start = jnp.arange(n_blocks, dtype=jnp.int32) * bm
    block_e = jnp.minimum(jnp.sum(pad_end[None, :] <= blk_start[:, None], axis=1),
                          N_EXPERTS - 1).astype(jnp.int32)
    n_used = (pad_end[-1:] // bm).astype(jnp.int32)
    order = jnp.argsort(ti4.reshape(tk), stable=True).astype(jnp.int32)
    off = (blk_start - pad_start[block_e])[:, None] + jnp.arange(bm, dtype=jnp.int32)
    src = jnp.where(off < counts[block_e][:, None], start[block_e][:, None] + off, 0)
    slot_tok = order[src.reshape(n_blocks * bm)] // TOP_K
    xg = h2[slot_tok]
    yb = _experts(xg, block_e, n_used, layer, w_gu, b_gu, w_dn, b_dn)
    yg = yb[dest.T.reshape(tk)]
    return _combine(x, yg, gates, mrow, mods)


def _final_body(x_ref, w_ref, o_ref):
    x = x_ref[...]
    ms = jnp.mean(x * x, axis=-1, keepdims=True)
    o_ref[...] = x * lax.rsqrt(ms + NORM_EPS) * w_ref[...]


def _final_norm(x, w):
    t, d = x.shape
    return pl.pallas_call(
        _final_body,
        grid=(t // SEG,),
        in_specs=[pl.BlockSpec((SEG, d), lambda i: (i, 0)), pl.BlockSpec((1, d), lambda i: (0, 0))],
        out_specs=pl.BlockSpec((SEG, d), lambda i: (i, 0)),
        out_shape=jax.ShapeDtypeStruct((t, d), F32),
        compiler_params=_cparams(1),
        name="final_rmsnorm",
    )(x, w.reshape(1, d))


def _segment_meta(n_ctx, sp, n_lat, ss):
    mrow, first, last, pos, lat = [], [], [], [], []
    for n_seq, length, is_lat in ((n_ctx, sp, 0), (n_lat, ss, 1)):
        per = length // SEG
        for s in range(n_seq):
            for j in range(per):
                mrow.append(1 + s if is_lat else 0)
                first.append(int(j == 0))
                last.append(int(j == per - 1))
                pos.append(j)
                lat.append(is_lat)
    as_arr = lambda v: jnp.asarray(np.asarray(v, np.int32))
    return dict(mrow=as_arr(mrow), first=as_arr(first), last=as_arr(last), pos=as_arr(pos),
                lat=as_arr(lat), mrow_np=np.asarray(mrow, np.int32))


def kernel(x_prompt, x_sample, state_ssd, cache_k, cache_v, c, c_ctx, w_mod, b_mod, norm_mix_w, norm_ffn_w, ev_w_in, ev_conv_w, ev_conv_b, ev_dt_bias, ev_a_log, ev_d_skip, ev_norm_w, ev_sconv_w, ev_w_out, od_w_in, od_q_norm_w, od_k_norm_w, od_dw_w, od_dw_b, od_ln_w, od_ln_b, od_w_out, router_w, router_b, w_gu, b_gu, w_dn, b_dn, final_norm_w):
    bp, sp, d = x_prompt.shape
    bs, ss, _ = x_sample.shape
    depth = w_mod.shape[0]
    n_ctx_tok = bp * sp
    t = n_ctx_tok + bs * ss
    past = cache_k.shape[2]
    kvw = KV_HEADS * HEAD_DIM
    assert sp % SEG == 0 and ss % SEG == 0 and n_ctx_tok % ss == 0 and past % SEG == 0
    assert sp % SSD_CHUNK == 0 and ss % ATT_TK == 0 and ATT_TK % sp == 0 and d ==SSD_GROUPS * SSD_HEADS_PER_GROUP * SSD_HEAD_DIM

    meta = _segment_meta(bp, sp, bs, ss)
    mrow = meta["mrow"]
    x = jnp.concatenate([x_prompt.reshape(n_ctx_tok, d), x_sample.reshape(bs * ss, d)], axis=0)

    n_rows = 16
    cvec = jnp.zeros((n_rows, d), F32).at[0].set(c_ctx).at[1:1 + bs].set(c)
    mods_all = _modulation(cvec, w_mod, b_mod)
    cos_t, sin_t = _rope_tables(ss)

    n_heads = SSD_GROUPS * SSD_HEADS_PER_GROUP
    conv_dim = d + 2 * SSD_GROUPS * SSD_STATE
    states, ctx_k, ctx_v = [], [], []
    for l in range(depth):
        mods = mods_all[l].reshape(n_rows, 1, 6 * d)
        i = l // 2
        if l % 2 == 0:
            w_in = ev_w_in[i]
            o_dt = d + conv_dim
            w_main = jnp.concatenate([w_in[:, :o_dt], w_in[:, o_dt + 2 * n_heads:]], axis=1).astype(BF16)
            regroup = lambda v: v.reshape(v.shape[:-1] + (2, SSD_GROUPS, SSD_HEADS_PER_GROUP)).swapaxes(-3, -2)
            pad_lanes = lambda v: jnp.zeros(v.shape[:-3] + (SSD_GROUPS, LANES), F32).at[..., :2 * SSD_HEADS_PER_GROUP].set(
                v.reshape(v.shape[:-3] + (SSD_GROUPS, 2 * SSD_HEADS_PER_GROUP)))
            w_dt = pad_lanes(regroup(w_in[:, o_dt:o_dt + 2 * n_heads])).reshape(d, SSD_GROUPS * LANES).astype(BF16)
            bias = pad_lanes(regroup(ev_dt_bias[i].reshape(2 * n_heads))).reshape(SSD_GROUPS, 1, LANES)
            alog = pad_lanes(regroup(ev_a_log[i].reshape(2 * n_heads))).reshape(SSD_GROUPS, 1, LANES)
            dsk = jnp.repeat(ev_d_skip[i], SSD_HEAD_DIM).reshape(1, d)

            main, dtp = _fused_proj(x, mrow, norm_mix_w[l], mods, 0, [w_main, w_dt], [BF16, F32])
            xbc = _ssd_conv(main, meta, ev_conv_w[i], ev_conv_b[i], d, conv_dim)
            y_ctx, st = _ssd_call(xbc, dtp, bias, alog, dsk, sp, bp, 0, None, i)
            (y_lat,) = _ssd_call(xbc, dtp, bias, alog, dsk, ss, bs, n_ctx_tok // ss, state_ssd, i)
            states.append(st)
            yc = _short_conv(main, meta, ev_sconv_w[i], o_dt, o_dt + d, o_dt + 2 * d, d)
            x = _out_proj(y_ctx, y_lat, yc, ev_w_out[i].astype(BF16), x, mrow, mods, 2,
                          z_src=main, norm_w=ev_norm_w[i])
        else:
            w_in = od_w_in[i]
            wq = w_in[:, :d].reshape(d, KV_HEADS, Q_PER_KV, HEAD_DIM).swapaxes(1, 2).reshape(d, d).astype(BF16)
            wkv = w_in[:, d:d + 2 * kvw].astype(BF16)
            wglu = w_in[:, d + 2 * kvw:].astype(BF16)
            w_out = od_w_out[i]
            w_att = w_out[:d].reshape(KV_HEADS, Q_PER_KV, HEAD_DIM, d).swapaxes(0, 1).reshape(d, d)
            w_out_p = jnp.concatenate([w_att, w_out[d:]], axis=0).astype(BF16)

            q, kv, glu = _fused_proj(x, mrow, norm_mix_w[l], mods, 0, [wq, wkv, wglu], [BF16, F32, BF16])
            qn, k_att, vt_att, k_n, v_n = _qk_prep(q, kv, meta, od_q_norm_w[i], od_k_norm_w[i], cos_t, sin_t)
            ctx_k.append(k_n[:n_ctx_tok].reshape(bp, sp, KV_HEADS, HEAD_DIM))
            ctx_v.append(v_n[:n_ctx_tok].reshape(bp, sp, KV_HEADS, HEAD_DIM))
            ck = cache_k[:, i].reshape(bs * past, kvw).astype(BF16)
            cvt = cache_v[:, i].reshape(bs, past, kvw).swapaxes(1, 2).astype(BF16)
            per = ATT_TK // sp
            ctx_src = (k_att, pl.BlockSpec((sp, kvw), lambda s, j: (s, 0)),
                       vt_att, pl.BlockSpec((None, kvw, sp), lambda s, j: (s // per, 0, s % per)), False)
            cache_src = (ck, pl.BlockSpec((past, kvw), lambda s, j: (s, 0)),
                         cvt, pl.BlockSpec((None, kvw, past), lambda s, j: (s, 0, 0)), False)
            lat_src = (k_att, pl.BlockSpec((ss, kvw), lambda s, j: (n_ctx_tok // ss + s, 0)),
                       vt_att, pl.BlockSpec((ss // ATT_TK, kvw, ATT_TK),
                                            lambda s, j: (n_ctx_tok // ss + s, 0, 0)), True)
            att_ctx = _attn_call(qn, [ctx_src], 0, bp, sp, "attention_ctx")
            att_lat = _attn_call(qn, [cache_src, lat_src], n_ctx_tok // SEG, bs, ss, "attention_latent")
            u = _conformer_conv(glu, meta, od_dw_w[i], od_dw_b[i], od_ln_w[i], od_ln_b[i])
            x = _out_proj(att_ctx, att_lat, u, w_out_p, x, mrow, mods, 2)

        h2, ti, gt, rk, cnt = _ffn_pre(x, mrow, norm_ffn_w[l], mods, router_w[l], router_b[l])
        x = _moe(x, h2, ti, gt, rk, cnt[0, :N_EXPERTS], mrow, mods, l, w_gu, b_gu, w_dn, b_dn)

    y = _final_norm(x, final_norm_w)
    y_prompt = y[:n_ctx_tok].reshape(bp, sp, d)
    y_sample = y[n_ctx_tok:].reshape(bs, ss, d)
    return (y_prompt, y_sample, jnp.stack(states, axis=1), jnp.stack(ctx_k, axis=1), jnp.stack(ctx_v, axis=1))
```

```python
import functools

import numpy as np
import jax
import jax.numpy as jnp
from jax import lax
from jax.experimental import pallas as pl
from jax.experimental.pallas import tpu as pltpu

F32 = jnp.float32
BF16 = jnp.bfloat16
HIGHEST = lax.Precision.HIGHEST

NORM_EPS = 1e-6
GRID_W = 64
ROPE_THETA = 10000.0
HEAD_DIM = 64
KV_HEADS = 4
Q_PER_KV = 4
SSD_HEAD_DIM = 64
SSD_GROUPS = 4
SSD_HEADS_PER_GROUP = 4
SSD_STATE = 128
SSD_CHUNK = 128
SSD_CONV = 4
SCONV_K = 3
CONF_K = 31
N_EXPERTS = 32
TOP_K = 4
SWIGLU_LIMIT = 7.0
SWIGLU_ALPHA = 1.702

SEG = 256
HALO = 16
LANES = 128
SUBLANES = 8
MOE_ROWS = 256
ATT_TK = 1024
ATT_STRIP = 512
NEG_BIG = -1e30
LOG2_E = 1.4426950408889634
VMEM_LIMIT = 56 * 1024 * 1024

NT_DIMS = (((1,), (1,)), ((), ()))
TN_DIMS = (((0,), (0,)), ((), ()))


def _cparams(n_grid):
    return pltpu.CompilerParams(dimension_semantics=("arbitrary",) * n_grid,
                                vmem_limit_bytes=VMEM_LIMIT)


def _resident(shape):
    nd = len(shape)
    return pl.BlockSpec(shape, lambda *_: (0,) * nd, pipeline_mode=pl.Buffered(1))


def _silu(x):
    return x * jax.nn.sigmoid(x)


def _mod_body(c_ref, w_ref, b_ref, o_ref):
    c = c_ref[...]
    s = _silu(c).astype(BF16)
    o_ref[0] = jnp.dot(s, w_ref[0].astype(BF16), preferred_element_type=F32) + b_ref[0]


def _modulation(cvec, w_mod, b_mod):
    depth, d, n = w_mod.shape
    rows = cvec.shape[0]
    tn = 1536
    assert n % tn == 0
    return pl.pallas_call(
        _mod_body,
        grid=(depth, n // tn),
        in_specs=[pl.BlockSpec((rows, d), lambda l, j: (0, 0)),
                  pl.BlockSpec((1, d, tn), lambda l, j: (l, 0, j)),
                  pl.BlockSpec((1, 1, tn), lambda l, j: (l, 0, j))],
        out_specs=pl.BlockSpec((1, rows, tn), lambda l, j: (l, 0, j)),
        out_shape=jax.ShapeDtypeStruct((depth, rows, n), F32),
        compiler_params=_cparams(2),
        name="modulation",
    )(cvec, w_mod, b_mod.reshape(depth, 1, n))


def _normed(x, nw_ref, sh_ref, sc_ref):
    ms = jnp.mean(x * x, axis=-1, keepdims=True)
    h = x * lax.rsqrt(ms + NORM_EPS) * nw_ref[...]
    return h * (1.0 + sc_ref[0]) + sh_ref[0]


def _proj_body(mrow_ref, x_ref, nw_ref, sh_ref, sc_ref, *rest, n_out, col_chunk):
    del mrow_ref
    w_refs, o_refs = rest[:n_out], rest[n_out:]
    hb = _normed(x_ref[...], nw_ref, sh_ref, sc_ref).astype(BF16)
    for w_ref, o_ref in zip(w_refs, o_refs):
        n = w_ref.shape[1]
        for c0 in range(0, n, col_chunk):
            c1 = min(n, c0 + col_chunk)
            o_ref[:, c0:c1] = jnp.dot(hb, w_ref[:, c0:c1],
                                      preferred_element_type=F32).astype(o_ref.dtype)


def _mod_spec(d, chunk):
    return pl.BlockSpec((1, 1, d), lambda i, mrow: (mrow[i], 0, chunk))


def _fused_proj(x, mrow, norm_w, mods, shift_chunk, weights, out_dtypes):
    t, d = x.shape
    nseg = t // SEG
    n_out = len(weights)
    in_specs = [pl.BlockSpec((SEG, d), lambda i, mrow: (i, 0)),
                pl.BlockSpec((1, d), lambda i, mrow: (0, 0)),
                _mod_spec(d, shift_chunk), _mod_spec(d, shift_chunk + 1)]
    in_specs += [_resident(w.shape) for w in weights]
    out_specs = [pl.BlockSpec((SEG, w.shape[1]), lambda i, mrow: (i, 0)) for w in weights]
    out_shape = [jax.ShapeDtypeStruct((t, w.shape[1]), dt) for w, dt in zip(weights, out_dtypes)]
    return pl.pallas_call(
        functools.partial(_proj_body, n_out=n_out, col_chunk=512),
        grid_spec=pltpu.PrefetchScalarGridSpec(
            num_scalar_prefetch=1, grid=(nseg,), in_specs=in_specs, out_specs=out_specs),
        out_shape=out_shape,
        compiler_params=_cparams(1),
        name="norm_mod_proj",
    )(mrow, x, norm_w.reshape(1, d), mods, mods, *weights)


def _fill(scr, prev, cur, nxt, keep_prev, keep_next):
    scr[0:HALO, :] = prev * keep_prev
    scr[HALO:HALO + SEG, :] = cur
    scr[HALO + SEG:HALO + SEG + HALO, :] = nxt * keep_next


def _taps(scr, w_ref, c0, k_taps, left):
    acc = None
    for k in range(k_taps):
        term = scr[pl.ds(HALO - left + k, SEG), c0:c0 + LANES] * w_ref[k:k + 1, c0:c0 + LANES]
        acc = term if acc is None else acc + term
    return acc


def _keep(first_ref, last_ref):
    i = pl.program_id(0)
    return (1 - first_ref[i]).astype(F32), (1 - last_ref[i]).astype(F32)


def _ssdconv_body(first_ref, last_ref, p_ref, c_ref, n_ref, w_ref, b_ref, o_ref, scr):
    kp, kn = _keep(first_ref, last_ref)
    _fill(scr, p_ref[...].astype(F32), c_ref[...].astype(F32), n_ref[...].astype(F32), kp, kn)
    for c0 in range(0, o_ref.shape[1], LANES):
        y = _taps(scr, w_ref, c0, SSD_CONV, SSD_CONV // 2) + b_ref[:, c0:c0 + LANES]
        o_ref[:, c0:c0 + LANES] = _silu(y).astype(o_ref.dtype)


def _sconv_body(first_ref, last_ref, gp_ref, gc_ref, gn_ref, xp_ref, xc_ref, xn_ref, gb_ref,
                w_ref, o_ref, scr):
    kp, kn = _keep(first_ref, last_ref)
    f = lambda a, b: a[...].astype(F32) * b[...].astype(F32)
    _fill(scr, f(gp_ref, xp_ref), f(gc_ref, xc_ref), f(gn_ref, xn_ref), kp, kn)
    for c0 in range(0, o_ref.shape[1], LANES):
        y = _taps(scr, w_ref, c0, SCONV_K, SCONV_K // 2)
        o_ref[:, c0:c0 + LANES] = (gb_ref[:, c0:c0 + LANES].astype(F32) * y).astype(o_ref.dtype)


def _conf_body(first_ref, last_ref, ap_ref, ac_ref, an_ref, gp_ref, gc_ref, gn_ref,
               w_ref, b_ref, lnw_ref, lnb_ref, o_ref, scr, u_scr, sh_scr):
    kp, kn = _keep(first_ref, last_ref)
    f = lambda a, g: a[...].astype(F32) * jax.nn.sigmoid(g[...].astype(F32))
    _fill(scr, f(ap_ref, gp_ref), f(ac_ref, gc_ref), f(an_ref, gn_ref), kp, kn)
    span = SEG + 2 * HALO - SUBLANES
    for r in range(SUBLANES):
        sh_scr[r] = scr[pl.ds(r, span), :]
    left = CONF_K // 2
    rows = SEG // 4
    for c0 in range(0, o_ref.shape[1], LANES):
        for r0 in range(0, SEG, rows):
            acc = None
            for k in range(CONF_K):
                off = HALO - left + k
                term = (sh_scr[off % SUBLANES, pl.ds(r0 + off - off % SUBLANES, rows), c0:c0 + LANES]
                        * w_ref[k:k + 1, c0:c0 + LANES])
                acc = term if acc is None else acc + term
            u_scr[r0:r0 + rows, c0:c0 + LANES] = acc + b_ref[:, c0:c0 + LANES]
    u = u_scr[...]
    mu = jnp.mean(u, axis=-1, keepdims=True)
    uc = u - mu
    var = jnp.mean(uc * uc, axis=-1, keepdims=True)
    y = uc * lax.rsqrt(var + NORM_EPS) * lnw_ref[...] + lnb_ref[...]
    o_ref[...] = _silu(y).astype(o_ref.dtype)


def _halo_specs(t, cw, col_off):
    per = SEG // HALO
    last_blk = t // HALO - 1
    return [
        pl.BlockSpec((HALO, cw), lambda i, j, f, l: (jnp.maximum(i * per - 1, 0), col_off + j)),
        pl.BlockSpec((SEG, cw), lambda i, j, f, l: (i, col_off + j)),
        pl.BlockSpec((HALO, cw), lambda i, j, f, l: (jnp.minimum((i + 1) * per, last_blk), col_off + j)),
    ]


def _conv_call(body, name, meta, t, cw, n_col, in_specs, args, out_cols, scratch):
    nseg = t // SEG
    return pl.pallas_call(
        body,
        grid_spec=pltpu.PrefetchScalarGridSpec(
            num_scalar_prefetch=2, grid=(nseg, n_col), in_specs=in_specs,
            out_specs=pl.BlockSpec((SEG, cw), lambda i, j, f, l: (i, j)),
            scratch_shapes=scratch),
        out_shape=jax.ShapeDtypeStruct((t, out_cols), BF16),
        compiler_params=_cparams(2),
        name=name,
    )(meta["first"], meta["last"], *args)


def _colvec_spec(rows, cw):
    return pl.BlockSpec((rows, cw), lambda i, j, f, l: (0, j))


def _ssd_conv(main, meta, conv_w, conv_b, col0, width):
    t = main.shape[0]
    cw = 512
    in_specs = _halo_specs(t, cw, col0 // cw) + [_colvec_spec(SSD_CONV, cw), _colvec_spec(1, cw)]
    return _conv_call(_ssdconv_body, "ssd_conv_silu", meta, t, cw, width // cw, in_specs,
                      (main, main, main, conv_w, conv_b.reshape(1, width)), width,
                      [pltpu.VMEM((SEG + 2 * HALO, cw), F32)])


def _short_conv(main, meta, sconv_w, col_gb, col_gc, col_xc, width):
    t = main.shape[0]
    cw = 512
    in_specs = (_halo_specs(t, cw, col_gc // cw) + _halo_specs(t, cw, col_xc // cw)
                + [pl.BlockSpec((SEG, cw), lambda i, j, f, l: (i, col_gb // cw + j)),
                   _colvec_spec(SCONV_K, cw)])
    return _conv_call(_sconv_body, "short_gated_conv", meta, t, cw, width // cw, in_specs,
                      (main,) * 7 + (sconv_w,), width, [pltpu.VMEM((SEG + 2 * HALO, cw), F32)])


def _conformer_conv(glu, meta, dw_w, dw_b, ln_w, ln_b):
    t, two_w = glu.shape
    w = two_w // 2
    in_specs = (_halo_specs(t, w, 0) + _halo_specs(t, w, 1)
                + [_colvec_spec(CONF_K, w)] + [_colvec_spec(1, w)] * 3)
    return _conv_call(_conf_body, "conformer_conv", meta, t, w, 1, in_specs,
                      (glu,) * 6 + (dw_w, dw_b.reshape(1, w), ln_w.reshape(1, w), ln_b.reshape(1, w)),
                      w, [pltpu.VMEM((SEG + 2 * HALO, w), F32), pltpu.VMEM((SEG, w), F32),
                       pltpu.VMEM((SUBLANES, SEG + 2 * HALO - SUBLANES, w), F32)])


def _softplus(x):
    return jnp.maximum(x, 0.0) + jnp.log1p(jnp.exp(-jnp.abs(x)))


def _ssd_body(*refs, nc, has_h0, want_state):
    it = iter(refs)
    x_ref, b_ref, c_ref, dt_ref, bias_ref, alog_ref, dsk_ref = (next(it) for _ in range(7))
    h0_ref = next(it) if has_h0 else None
    y_ref = next(it)
    st_ref = next(it) if want_state else None
    ybuf, sf_scr, sb_scr, decf_scr, decb_scr, xsf_scr, xsb_scr, tot_scr = (next(it) for _ in range(8))

    q = SSD_CHUNK
    hp = SSD_HEADS_PER_GROUP
    width = hp * SSD_HEAD_DIM
    row = lax.broadcasted_iota(jnp.int32, (q, q), 0)
    col = lax.broadcasted_iota(jnp.int32, (q, q), 1)
    lower = row >= col
    upper = col >= row
    lower_f = lower.astype(F32)
    lane_w = lax.broadcasted_iota(jnp.int32, (q, width), 1) // SSD_HEAD_DIM
    sub_w = lax.broadcasted_iota(jnp.int32, (width, 1), 0) // SSD_HEAD_DIM

    def per_head_cols(mat, base):
        out = mat[:, base + hp - 1:base + hp]
        for r in range(hp - 2, -1, -1):
            out = jnp.where(lane_w == r, mat[:, base + r:base + r + 1], out)
        return out

    def per_head_rows(rowvec, base):
        out = rowvec[:, base + hp - 1:base + hp]
        for r in range(hp - 2, -1, -1):
            out = jnp.where(sub_w == r, rowvec[:, base + r:base + r + 1], out)
        return out

    a_row = -jnp.exp(alog_ref[0])
    bias_row = bias_ref[0]

    dsk = dsk_ref[...]

    def intra(c, carry):
        rows = pl.ds(pl.multiple_of(c * q, q), q)
        dt = _softplus(dt_ref[rows, :] + bias_row)
        a = dt * a_row
        cs = jnp.dot(lower_f, a, precision=HIGHEST, preferred_element_type=F32)
        ex = cs - a
        tot = cs[q - 1:q, :]
        xb = x_ref[rows, :]
        x = xb.astype(F32)
        cb = lax.dot_general(c_ref[rows, :], b_ref[rows, :], NT_DIMS, preferred_element_type=F32)
        cs_t, ex_t, dt_t = cs.T, ex.T, dt.T
        y = x * dsk
        for r in range(hp):
            dec_f = jnp.exp(jnp.where(lower, cs[:, r:r + 1] - cs_t[r:r + 1, :], NEG_BIG))
            dec_b = jnp.exp(jnp.where(upper, ex_t[hp + r:hp + r + 1, :] - ex[:, hp + r:hp + r + 1], NEG_BIG))
            wm = cb * (dec_f * dt_t[r:r + 1, :] + dec_b * dt_t[hp + r:hp + r + 1, :])
            yd = jnp.dot(wm.astype(BF16), xb, preferred_element_type=F32)
            y = y + jnp.where(lane_w == r, yd, 0.0)
        ybuf[rows, :] = y
        decf_scr[rows, :] = per_head_cols(jnp.exp(cs), 0)
        decb_scr[rows, :] = per_head_cols(jnp.exp(tot - ex), hp)
        xsf_scr[rows, :] = (x * per_head_cols(jnp.exp(tot - cs) * dt, 0)).astype(BF16)
        xsb_scr[rows, :] = (x * per_head_cols(jnp.exp(ex) * dt, hp)).astype(BF16)
        tot_scr[pl.ds(c, 1), :] = jnp.exp(tot)
        return carry

    lax.fori_loop(0, nc, intra, 0, unroll=2)

    if has_h0:
        sf_scr[...] = h0_ref[0].reshape(width, SSD_STATE)
        sb_scr[...] = h0_ref[1].reshape(width, SSD_STATE)
    else:
        sf_scr[...] = jnp.zeros_like(sf_scr)
        sb_scr[...] = jnp.zeros_like(sb_scr)

    def one_direction(c, s_scr, dec_scr, xs_scr, base):
        rows = pl.ds(pl.multiple_of(c * q, q), q)
        s_prev = s_scr[...]
        y_off = lax.dot_general(c_ref[rows, :], s_prev.astype(BF16), NT_DIMS, preferred_element_type=F32)
        ybuf[rows, :] = ybuf[rows, :] + y_off * dec_scr[rows, :]
        contrib = lax.dot_general(xs_scr[rows, :], b_ref[rows, :], TN_DIMS, preferred_element_type=F32)
        s_scr[...] = s_prev * per_head_rows(tot_scr[pl.ds(c, 1), :], base) + contrib

    def recur(i, carry):
        one_direction(i, sf_scr, decf_scr, xsf_scr, 0)
        one_direction(nc - 1 - i, sb_scr, decb_scr, xsb_scr, hp)
        return carry

    lax.fori_loop(0, nc, recur, 0, unroll=2)

    y_ref[...] = ybuf[...].astype(y_ref.dtype)
    if want_state:
        st_ref[0] = sf_scr[...].reshape(hp, SSD_HEAD_DIM, SSD_STATE)
        st_ref[1] = sb_scr[...].reshape(hp, SSD_HEAD_DIM, SSD_STATE)


def _ssd_call(xbc, dtp, bias, alog, dsk, seq_len, n_seq, row_blk0, h0, layer_idx, seq0):
    inner = SSD_GROUPS * SSD_HEADS_PER_GROUP * SSD_HEAD_DIM
    width = SSD_HEADS_PER_GROUP * SSD_HEAD_DIM
    nb = inner // SSD_STATE
    nc = seq_len // SSD_CHUNK
    want_state = h0 is None
    in_specs = [
        pl.BlockSpec((seq_len, width), lambda s, g: (row_blk0 + s, g)),
        pl.BlockSpec((seq_len, SSD_STATE), lambda s, g: (row_blk0 + s, nb + g)),
        pl.BlockSpec((seq_len, SSD_STATE), lambda s, g: (row_blk0 + s, nb + SSD_GROUPS + g)),
        pl.BlockSpec((seq_len, LANES), lambda s, g: (row_blk0 + s, g)),
        pl.BlockSpec((1, 1, LANES), lambda s, g: (g, 0, 0)),
        pl.BlockSpec((1, 1, LANES), lambda s, g: (g, 0, 0)),
        pl.BlockSpec((1, width), lambda s, g: (0, g)),
    ]
    args = [xbc, xbc, xbc, dtp, bias, alog, dsk]
    if not want_state:
        in_specs.append(pl.BlockSpec((None, None, 2, SSD_HEADS_PER_GROUP, SSD_HEAD_DIM, SSD_STATE),
                                     lambda s, g: (seq0 + s, layer_idx, 0, g, 0, 0)))
        args.append(h0)
    out_specs = [pl.BlockSpec((seq_len, width), lambda s, g: (s, g))]
    out_shape = [jax.ShapeDtypeStruct((n_seq * seq_len, inner), BF16)]
    if want_state:
        out_specs.append(pl.BlockSpec((None, 2, SSD_HEADS_PER_GROUP, SSD_HEAD_DIM, SSD_STATE),
                                      lambda s, g: (s, 0, g, 0, 0)))
        out_shape.append(jax.ShapeDtypeStruct(
            (n_seq, 2, SSD_GROUPS * SSD_HEADS_PER_GROUP, SSD_HEAD_DIM, SSD_STATE), F32))
    return pl.pallas_call(
        functools.partial(_ssd_body, nc=nc, has_h0=not want_state, want_state=want_state),
        grid=(n_seq, SSD_GROUPS),
        in_specs=in_specs, out_specs=out_specs, out_shape=out_shape,
        scratch_shapes=[pltpu.VMEM((seq_len, width), F32),
                        pltpu.VMEM((width, SSD_STATE), F32), pltpu.VMEM((width, SSD_STATE), F32),
                        pltpu.VMEM((seq_len, width), F32), pltpu.VMEM((seq_len, width), F32),
                        pltpu.VMEM((seq_len, width), BF16), pltpu.VMEM((seq_len, width), BF16),
                        pltpu.VMEM((max(nc, SUBLANES), LANES), F32)],
        compiler_params=_cparams(2),
        name="ssd_scan_ctx" if want_state else "ssd_scan_latent",
    )(*args)


def _outproj_body(mrow_ref, al_ref, ac_ref, *refs, gated_norm):
    a = jnp.where(mrow_ref[pl.program_id(0)] > 0, al_ref[...], ac_ref[...])
    if gated_norm:
        z_ref, nw_ref, b_ref, w_ref, x_ref, g_ref, o_ref = refs
        y = a.astype(F32) * _silu(z_ref[...].astype(F32))
        ms = jnp.mean(y * y, axis=-1, keepdims=True)
        a = (y * lax.rsqrt(ms + NORM_EPS) * nw_ref[...]).astype(BF16)
    else:
        b_ref, w_ref, x_ref, g_ref, o_ref = refs
    d = a.shape[1]
    out = jnp.dot(a, w_ref[0:d, :], preferred_element_type=F32)
    out = out + jnp.dot(b_ref[...], w_ref[d:, :], preferred_element_type=F32)
    o_ref[...] = x_ref[...] + g_ref[0] * out


def _out_proj(a_lat, a_ctx, b, w, x, mrow, mods, gate_chunk, z_src=None, norm_w=None):
    t, d = x.shape
    nseg = t // SEG
    n_lat_seg = a_lat.shape[0] // SEG
    row = lambda i, mrow: (i, 0)
    in_specs = [pl.BlockSpec((SEG, d), lambda i, mrow: (jnp.minimum(i, n_lat_seg - 1), 0)),
                pl.BlockSpec((SEG, d), lambda i, mrow: (jnp.maximum(i - n_lat_seg, 0), 0))]
    args = [a_lat, a_ctx]
    if z_src is not None:
        in_specs += [pl.BlockSpec((SEG, d), row), pl.BlockSpec((1, d), lambda i, mrow: (0, 0))]
        args += [z_src, norm_w.reshape(1, d)]
    in_specs += [pl.BlockSpec((SEG, d), row), _resident(w.shape), pl.BlockSpec((SEG, d), row),
                 _mod_spec(d, gate_chunk)]
    args += [b, w, x, mods]
    return pl.pallas_call(
        functools.partial(_outproj_body, gated_norm=z_src is not None),
        grid_spec=pltpu.PrefetchScalarGridSpec(
            num_scalar_prefetch=1, grid=(nseg,), in_specs=in_specs,
            out_specs=pl.BlockSpec((SEG, d), row)),
        out_shape=jax.ShapeDtypeStruct((t, d), F32),
        compiler_params=_cparams(1),
        name="mixer_out_proj",
    )(mrow, *args)


def _qkprep_body(lat_ref, pos_ref, q_ref, kv_ref, qw_ref, kw_ref, cos_ref, sin_ref,
                 qo_ref, ko_ref, vo_ref, kn_ref, vn_ref):
    del pos_ref
    is_lat = lat_ref[pl.program_id(0)] > 0
    r = lax.broadcasted_iota(jnp.int32, (LANES, LANES), 0) // HEAD_DIM
    c = lax.broadcasted_iota(jnp.int32, (LANES, LANES), 1) // HEAD_DIM
    head_mean = jnp.where(r == c, 1.0 / HEAD_DIM, 0.0).astype(F32)
    lane = lax.broadcasted_iota(jnp.int32, (SEG, LANES), 1)
    first_half = (lane % HEAD_DIM) < (HEAD_DIM // 2)
    cos = cos_ref[...]
    sin = sin_ref[...]

    def norm_rope(xs, w):
        ms = jnp.dot(xs * xs, head_mean, precision=HIGHEST, preferred_element_type=F32)
        xn = xs * lax.rsqrt(ms + NORM_EPS) * w
        rot = jnp.where(first_half, pltpu.roll(xn, LANES - HEAD_DIM // 2, 1),
                        pltpu.roll(xn, HEAD_DIM // 2, 1))
        return xn, jnp.where(is_lat, xn * cos + rot * sin, xn)

    scale = HEAD_DIM ** -0.5 * LOG2_E
    for c0 in range(0, q_ref.shape[1], LANES):
        _, qr = norm_rope(q_ref[:, c0:c0 + LANES].astype(F32), qw_ref[...])
        qo_ref[:, c0:c0 + LANES] = (qr * scale).astype(qo_ref.dtype)
    kvw = ko_ref.shape[1]
    for c0 in range(0, kvw, LANES):
        kn, kr = norm_rope(kv_ref[:, c0:c0 + LANES], kw_ref[...])
        kn_ref[:, c0:c0 + LANES] = kn
        ko_ref[:, c0:c0 + LANES] = kr.astype(ko_ref.dtype)
    v = kv_ref[:, kvw:]
    vn_ref[...] = v
    vo_ref[...] = v.T.astype(vo_ref.dtype)


def _qk_prep(q, kv, meta, q_norm_w, k_norm_w, cos_t, sin_t):
    t, qd = q.shape
    kvd = kv.shape[1] // 2
    nseg = t // SEG
    row = lambda i, lat, pos: (i, 0)
    tab = lambda i, lat, pos: (pos[i] * lat[i], 0)
    tile2 = lambda w: jnp.tile(w, LANES // HEAD_DIM).reshape(1, LANES)
    per_chunk = ATT_TK // SEG
    return pl.pallas_call(
        _qkprep_body,
        grid_spec=pltpu.PrefetchScalarGridSpec(
            num_scalar_prefetch=2, grid=(nseg,),
            in_specs=[pl.BlockSpec((SEG, qd), row), pl.BlockSpec((SEG, 2 * kvd), row),
                      pl.BlockSpec((1, LANES), lambda i, lat, pos: (0, 0)),
                      pl.BlockSpec((1, LANES), lambda i, lat, pos: (0, 0)),
                      pl.BlockSpec((SEG, LANES), tab), pl.BlockSpec((SEG, LANES), tab)],
            out_specs=[pl.BlockSpec((SEG, qd), row), pl.BlockSpec((SEG, kvd), row),
                       pl.BlockSpec((None, kvd, SEG), lambda i, lat, pos: (i // per_chunk, 0, i % per_chunk)),
                       pl.BlockSpec((SEG, kvd), row), pl.BlockSpec((SEG, kvd), row)]),
        out_shape=[jax.ShapeDtypeStruct((t, qd), BF16), jax.ShapeDtypeStruct((t, kvd), BF16),
                   jax.ShapeDtypeStruct((t // ATT_TK, kvd, ATT_TK), BF16),
                   jax.ShapeDtypeStruct((t, kvd), F32), jax.ShapeDtypeStruct((t, kvd), F32)],
        compiler_params=_cparams(1),
        name="qk_norm_rope",
    )(meta["lat"], meta["pos"], q, kv, tile2(q_norm_w), tile2(k_norm_w), cos_t, sin_t)


def _rope_tables(seq_len):
    rows = seq_len // GRID_W
    rowp = jnp.repeat(jnp.arange(rows), GRID_W).astype(F32)
    colp = jnp.tile(jnp.arange(GRID_W), rows).astype(F32)
    axis = HEAD_DIM // 2
    inv = ROPE_THETA ** (-jnp.arange(0, axis, 2, dtype=F32) / axis)
    ang = jnp.concatenate([rowp[:, None] * inv, colp[:, None] * inv], axis=-1)
    cos, sin = jnp.cos(ang), jnp.sin(ang)
    cos_h = jnp.concatenate([cos, cos], axis=-1)
    sin_h = jnp.concatenate([-sin, sin], axis=-1)
    rep = LANES // HEAD_DIM
    return jnp.tile(cos_h, (1, rep)), jnp.tile(sin_h, (1, rep))


ONES_ROWS = 16


def _attn_body(*refs, chunked, tq):
    n_src = len(chunked)
    q_ref = refs[0]
    kv_refs = refs[1:1 + 2 * n_src]
    pos = 1 + 2 * n_src
    o_ref = refs[pos]
    qs_scr, acc_scr, m_scr = refs[pos + 1:]
    kvw = KV_HEADS * HEAD_DIM
    lane_g = lax.broadcasted_iota(jnp.int32, (tq, kvw), 1) // HEAD_DIM

    for g in range(KV_HEADS):
        for r in range(Q_PER_KV):
            qr = q_ref[:, kvw * r:kvw * (r + 1)]
            qs_scr[g, r * tq:(r + 1) * tq, :] = jnp.where(lane_g == g, qr, jnp.zeros_like(qr))
    m_scr[...] = jnp.full_like(m_scr, NEG_BIG)
    acc_scr[...] = jnp.zeros_like(acc_scr)

    units = [(g, c) for g in range(KV_HEADS) for c in range(Q_PER_KV * tq // ATT_STRIP)]

    def step(kc, vt):
        ones = jnp.ones((ONES_ROWS, kc.shape[0]), BF16)

        def scores(u):
            g, c = u
            return lax.dot_general(kc, qs_scr[g, c * ATT_STRIP:(c + 1) * ATT_STRIP, :], NT_DIMS,
                                   preferred_element_type=F32)

        def softmax(u, s):
            g, c = u
            cols = slice(c * ATT_STRIP, (c + 1) * ATT_STRIP)
            m_prev = m_scr[g, :, cols]
            m_new = jnp.maximum(m_prev, jnp.max(s, axis=0, keepdims=True))
            m_scr[g, :, cols] = m_new
            return jnp.exp2(s - m_new).astype(BF16), jnp.exp2(m_prev - m_new)

        def accumulate(u, p, alpha):
            g, c = u
            cols = slice(c * ATT_STRIP, (c + 1) * ATT_STRIP)
            lhs = jnp.concatenate([vt[HEAD_DIM * g:HEAD_DIM * (g + 1), :], ones], axis=0)
            pv = jnp.dot(lhs, p, preferred_element_type=F32)
            acc_scr[g, :, cols] = acc_scr[g, :, cols] * alpha + pv

        s_cur = scores(units[0])
        pending = None
        for i, u in enumerate(units):
            s_next = scores(units[i + 1]) if i + 1 < len(units) else None
            if pending is not None:
                accumulate(*pending)
            pending = (u,) + softmax(u, s_cur)
            s_cur = s_next
        accumulate(*pending)

    for si, is_chunked in enumerate(chunked):
        k_ref, vt_ref = kv_refs[2 * si], kv_refs[2 * si + 1]
        if not is_chunked:
            step(k_ref[...], vt_ref[...])
        else:
            tk = vt_ref.shape[2]

            def body(j, carry, k_ref=k_ref, vt_ref=vt_ref, tk=tk):
                rows = pl.ds(pl.multiple_of(j * tk, tk), tk)
                step(k_ref[rows, :], vt_ref[j])
                return carry
            lax.fori_loop(0, vt_ref.shape[0], body, 0)

    for r in range(Q_PER_KV):
        cols = slice(r * tq, (r + 1) * tq)
        parts = [acc_scr[g, 0:HEAD_DIM, cols] / acc_scr[g, HEAD_DIM:HEAD_DIM + 1, cols]
                 for g in range(KV_HEADS)]
        o_ref[:, kvw * r:kvw * (r + 1)] = jnp.concatenate(parts, axis=0).T.astype(o_ref.dtype)


def _attn_call(q, srcs, q_blk0, n_seq, q_len, name):
    qd = q.shape[1]
    tq = SEG
    kvw = KV_HEADS * HEAD_DIM
    nq = q_len // tq
    in_specs = [pl.BlockSpec((tq, qd), lambda s, j: (q_blk0 + s * nq + j, 0))]
    args = [q]
    for k, k_spec, vt, vt_spec, _ in srcs:
        in_specs += [k_spec, vt_spec]
        args += [k, vt]
    rows4 = Q_PER_KV * tq
    return pl.pallas_call(
        functools.partial(_attn_body, chunked=tuple(s[4] for s in srcs), tq=tq),
        grid=(n_seq, nq),
        in_specs=in_specs,
        out_specs=pl.BlockSpec((tq, qd), lambda s, j: (s * nq + j, 0)),
        out_shape=jax.ShapeDtypeStruct((n_seq * q_len, qd), BF16),
        scratch_shapes=[pltpu.VMEM((KV_HEADS, rows4, kvw), BF16),
                        pltpu.VMEM((KV_HEADS, HEAD_DIM + ONES_ROWS, rows4), F32),
                        pltpu.VMEM((KV_HEADS, 1, rows4), F32)],
        compiler_params=_cparams(2),
        name=name,
    )(*args)


def _ffnpre_body(mrow_ref, x_ref, nw_ref, sh_ref, sc_ref, rw_ref, rb_ref,
                 h_ref, ti_ref, gt_ref, rk_ref, cnt_ref, cnt_scr):
    del mrow_ref

    @pl.when(pl.program_id(0) == 0)
    def _():
        cnt_scr[...] = jnp.zeros_like(cnt_scr)

    h = _normed(x_ref[...], nw_ref, sh_ref, sc_ref)
    hb = h.astype(BF16)
    h_ref[...] = hb
    logits = jnp.dot(hb, rw_ref[...], preferred_element_type=F32) + rb_ref[...]
    lane = lax.broadcasted_iota(jnp.int32, logits.shape, 1)
    work = logits
    vals, idxs = [], []
    for _ in range(TOP_K):
        m = jnp.max(work, axis=1, keepdims=True)
        idx = jnp.min(jnp.where(work == m, lane, LANES), axis=1, keepdims=True)
        vals.append(m)
        idxs.append(idx)
        work = jnp.where(lane == idx, NEG_BIG, work)
    es = [jnp.exp(v - vals[0]) for v in vals]
    den = es[0]
    for e in es[1:]:
        den = den + e
    ti = jnp.zeros(logits.shape, jnp.int32)
    gt = jnp.zeros(logits.shape, F32)
    for k in range(TOP_K):
        ti = jnp.where(lane == k, idxs[k], ti)
        gt = jnp.where(lane == k, es[k] / den, gt)
    ti_ref[...] = ti
    gt_ref[...] = gt

    onehot = jnp.zeros(logits.shape, F32)
    for k in range(TOP_K):
        onehot = onehot + (lane == idxs[k]).astype(F32)
    n = logits.shape[0]
    earlier = (lax.broadcasted_iota(jnp.int32, (n, n), 0) > lax.broadcasted_iota(jnp.int32, (n, n), 1))
    before = jnp.dot(earlier.astype(BF16), onehot.astype(BF16), preferred_element_type=F32) + cnt_scr[...]
    rk = jnp.zeros(logits.shape, jnp.int32)
    for k in range(TOP_K):
        r_k = jnp.sum(jnp.where(lane == idxs[k], before, 0.0), axis=1, keepdims=True)
        rk = jnp.where(lane == k, r_k.astype(jnp.int32), rk)
    rk_ref[...] = rk
    cnt_scr[...] = cnt_scr[...] + jnp.sum(onehot, axis=0, keepdims=True)
    cnt_ref[...] = cnt_scr[...]


def _ffn_pre(x, mrow, norm_w, mods, router_w, router_b):
    t, d = x.shape
    nseg = t // SEG
    ne = router_w.shape[1]
    rw = jnp.zeros((d, LANES), F32).at[:, :ne].set(router_w).astype(BF16)
    rb = jnp.full((1, LANES), NEG_BIG, F32).at[0, :ne].set(router_b)
    row = lambda i, mrow: (i, 0)
    const = lambda i, mrow: (0, 0)
    return pl.pallas_call(
        _ffnpre_body,
        grid_spec=pltpu.PrefetchScalarGridSpec(
            num_scalar_prefetch=1, grid=(nseg,),
            in_specs=[pl.BlockSpec((SEG, d), row), pl.BlockSpec((1, d), const),
                      _mod_spec(d, 3), _mod_spec(d, 4),
                      pl.BlockSpec((d, LANES), const), pl.BlockSpec((1, LANES), const)],
            out_specs=[pl.BlockSpec((SEG, d), row), pl.BlockSpec((SEG, LANES), row),
                       pl.BlockSpec((SEG, LANES), row), pl.BlockSpec((SEG, LANES), row),
                       pl.BlockSpec((1, LANES), const)],
            scratch_shapes=[pltpu.VMEM((1, LANES), F32)]),
        out_shape=[jax.ShapeDtypeStruct((t, d), BF16), jax.ShapeDtypeStruct((t, LANES), jnp.int32),
                   jax.ShapeDtypeStruct((t, LANES), F32), jax.ShapeDtypeStruct((t, LANES), jnp.int32),
                   jax.ShapeDtypeStruct((1, LANES), F32)],
        compiler_params=_cparams(1),
        name="ffn_norm_router_topk",
    )(mrow, x, norm_w.reshape(1, d), mods, mods, rw, rb)


def _experts_body(be_ref, nu_ref, x_ref, wgu_ref, bgu_ref, wdn_ref, bdn_ref, o_ref, wgu_bf, wdn_bf):
    i = pl.program_id(0)
    live = i < nu_ref[0]
    new_expert = jnp.logical_or(i == 0, be_ref[i] != be_ref[jnp.maximum(i - 1, 0)])

    @pl.when(jnp.logical_and(live, new_expert))
    def _():
        wgu_bf[...] = wgu_ref[...].astype(BF16)
        wdn_bf[...] = wdn_ref[...].astype(BF16)

    @pl.when(live)
    def _():
        hg = jnp.dot(x_ref[...], wgu_bf[...], preferred_element_type=F32) + bgu_ref[...]
        dff = hg.shape[1] // 2
        g = jnp.minimum(hg[:, :dff], SWIGLU_LIMIT)
        u = jnp.clip(hg[:, dff:], -SWIGLU_LIMIT, SWIGLU_LIMIT)
        act = g * jax.nn.sigmoid(SWIGLU_ALPHA * g) * (u + 1.0)
        y = jnp.dot(act.astype(BF16), wdn_bf[...], preferred_element_type=F32) + bdn_ref[...]
        o_ref[...] = y.astype(o_ref.dtype)

    @pl.when(i >= nu_ref[0])
    def _():
        o_ref[...] = jnp.zeros_like(o_ref)


def _experts(xg, block_e, n_used, layer, w_gu, b_gu, w_dn, b_dn):
    n_slots, d = xg.shape
    n_blocks = n_slots // MOE_ROWS
    depth, ne, _, two_f = w_gu.shape
    dff = two_f // 2
    b_gu = b_gu.reshape(depth, ne, 1, two_f)
    b_dn = b_dn.reshape(depth, ne, 1, d)
    return pl.pallas_call(
        _experts_body,
        grid_spec=pltpu.PrefetchScalarGridSpec(
            num_scalar_prefetch=2, grid=(n_blocks,),
            in_specs=[pl.BlockSpec((MOE_ROWS, d), lambda i, be, nu: (i, 0)),
                      pl.BlockSpec((None, None, d, two_f), lambda i, be, nu: (layer, be[i], 0, 0)),
                      pl.BlockSpec((None, None, 1, two_f), lambda i, be, nu: (layer, be[i], 0, 0)),
                      pl.BlockSpec((None, None, dff, d), lambda i, be, nu: (layer, be[i], 0, 0)),
                      pl.BlockSpec((None, None, 1, d), lambda i, be, nu: (layer, be[i], 0, 0))],
            out_specs=pl.BlockSpec((MOE_ROWS, d), lambda i, be, nu: (i, 0)),
            scratch_shapes=[pltpu.VMEM((d, two_f), BF16), pltpu.VMEM((dff, d), BF16)]),
        out_shape=jax.ShapeDtypeStruct((n_slots, d), BF16),
        compiler_params=_cparams(1),
        name="expert_swiglu",
    )(block_e, n_used, xg, w_gu, b_gu, w_dn, b_dn)


def _combine_body(mrow_ref, x_ref, y0_ref, y1_ref, y2_ref, y3_ref, gt_ref, g2_ref, o_ref):
    del mrow_ref
    gt = gt_ref[...]
    f = None
    for k, y_ref in enumerate((y0_ref, y1_ref, y2_ref, y3_ref)):
        term = y_ref[...].astype(F32) * gt[:, k:k + 1]
        f = term if f is None else f + term
    o_ref[...] = x_ref[...] + g2_ref[0] * f


def _combine(x, yg, gates, mrow, mods):
    t, d = x.shape
    nseg = t // SEG
    row = lambda i, mrow: (i, 0)
    assert TOP_K == 4
    in_specs = [pl.BlockSpec((SEG, d), row)]
    in_specs += [pl.BlockSpec((SEG, d), lambda i, mrow, k=k: (k * nseg + i, 0)) for k in range(TOP_K)]
    in_specs += [pl.BlockSpec((SEG, LANES), row), _mod_spec(d, 5)]
    return pl.pallas_call(
        _combine_body,
        grid_spec=pltpu.PrefetchScalarGridSpec(
            num_scalar_prefetch=1, grid=(nseg,), in_specs=in_specs,
            out_specs=pl.BlockSpec((SEG, d), row)),
        out_shape=jax.ShapeDtypeStruct((t, d), F32),
        compiler_params=_cparams(1),
        name="moe_combine_residual",
    )(mrow, x, yg, yg, yg, yg, gates, mods)


def _moe(x, h2, top_i, gates, rank, counts, mrow, mods, layer, w_gu, b_gu, w_dn, b_dn):
    t, d = x.shape
    tk = t * TOP_K
    bm = MOE_ROWS
    counts = counts.astype(jnp.int32)
    padded = (counts + bm - 1) // bm * bm
    pad_end = jnp.cumsum(padded)
    pad_start = pad_end - padded
    start = jnp.cumsum(counts) - counts
    experts = jnp.arange(N_EXPERTS, dtype=jnp.int32)
    ti4 = top_i[:, :TOP_K]
    dest = rank[:, :TOP_K] + jnp.sum(jnp.where(ti4[:, :, None] == experts, pad_start, 0), axis=-1)
    n_blocks = (tk + N_EXPERTS * (bm - 1) + bm - 1) // bm
    blk_start = jnp.arange(n_blocks, dtype=jnp.int32) * bm
    block_e = jnp.minimum(jnp.sum(pad_end[None, :] <= blk_start[:, None], axis=1),
                          N_EXPERTS - 1).astype(jnp.int32)
    n_used = (pad_end[-1:] // bm).astype(jnp.int32)
    order = jnp.argsort(ti4.reshape(tk), stable=True).astype(jnp.int32)
    off = (blk_start - pad_start[block_e])[:, None] + jnp.arange(bm, dtype=jnp.int32)
    src = jnp.where(off < counts[block_e][:, None], start[block_e][:, None] + off, 0)
    slot_tok = order[src.reshape(n_blocks * bm)] // TOP_K
    xg = h2[slot_tok]
    yb = _experts(xg, block_e, n_used, layer, w_gu, b_gu, w_dn, b_dn)
    yg = yb[dest.T.reshape(tk)]
    return _combine(x, yg, gates, mrow, mods)


def _final_body(x_ref, w_ref, o_ref):
    x = x_ref[...]
    ms = jnp.mean(x * x, axis=-1, keepdims=True)
    o_ref[...] = x * lax.rsqrt(ms + NORM_EPS) * w_ref[...]


def _final_norm(x, w):
    t, d = x.shape
    return pl.pallas_call(
        _final_body,
        grid=(t // SEG,),
        in_specs=[pl.BlockSpec((SEG, d), lambda i: (i, 0)), pl.BlockSpec((1, d), lambda i: (0, 0))],
        out_specs=pl.BlockSpec((SEG, d), lambda i: (i, 0)),
        out_shape=jax.ShapeDtypeStruct((t, d), F32),
        compiler_params=_cparams(1),
        name="final_rmsnorm",
    )(x, w.reshape(1, d))


def _segment_meta(n_lat, ss, lat0, n_ctx, sp):
    mrow, first, last, pos, lat = [], [], [], [], []
    for n_seq, length, is_lat in ((n_lat, ss, 1), (n_ctx, sp, 0)):
        per = length // SEG
        for s in range(n_seq):
            for j in range(per):
                mrow.append(1 + lat0 + s if is_lat else 0)
                first.append(int(j == 0))
                last.append(int(j == per - 1))
                pos.append(j)
                lat.append(is_lat)
    as_arr = lambda v: jnp.asarray(np.asarray(v, np.int32))
    return dict(mrow=as_arr(mrow), first=as_arr(first), last=as_arr(last), pos=as_arr(pos),
                lat=as_arr(lat), mrow_np=np.asarray(mrow, np.int32))


def _run_stream(lat0, bs, ctx0, bp, mods_all, cos_t, sin_t, x_prompt, x_sample, state_ssd, cache_k, cache_v, norm_mix_w, norm_ffn_w, ev_w_in, ev_conv_w, ev_conv_b, ev_dt_bias, ev_a_log, ev_d_skip, ev_norm_w, ev_sconv_w, ev_w_out, od_w_in, od_q_norm_w, od_k_norm_w, od_dw_w, od_dw_b, od_ln_w, od_ln_b, od_w_out, router_w, router_b, w_gu, b_gu, w_dn, b_dn, final_norm_w):
    _, sp, d = x_prompt.shape
    bs_all, ss, _ = x_sample.shape
    depth, n_rows = mods_all.shape[0], mods_all.shape[1]
    n_lat_tok = bs * ss
    n_ctx_tok = bp * sp
    past = cache_k.shape[2]
    kvw = KV_HEADS * HEAD_DIM
    assert sp % SEG == 0 and ss % SEG == 0 and past % SEG == 0 and sp % SSD_CHUNK == 0
    assert ss % ATT_TK == 0 and ATT_TK % sp == 0 and (n_lat_tok + n_ctx_tok) % ATT_TK == 0
    assert d == SSD_GROUPS * SSD_HEADS_PER_GROUP * SSD_HEAD_DIM

    meta = _segment_meta(bs, ss, lat0, bp, sp)
    mrow = meta["mrow"]
    x = jnp.concatenate([x_sample[lat0:lat0 + bs].reshape(n_lat_tok, d),
                         x_prompt[ctx0:ctx0 + bp].reshape(n_ctx_tok, d)], axis=0)
    ctx_blk0 = n_lat_tok // sp

    n_heads = SSD_GROUPS * SSD_HEADS_PER_GROUP
    conv_dim = d + 2 * SSD_GROUPS * SSD_STATE
    states, ctx_k, ctx_v = [], [], []
    for l in range(depth):
        mods = mods_all[l].reshape(n_rows, 1, 6 * d)
        i = l // 2
        if l % 2 == 0:
            w_in = ev_w_in[i]
            o_dt = d + conv_dim
            w_main = jnp.concatenate([w_in[:, :o_dt], w_in[:, o_dt + 2 * n_heads:]], axis=1).astype(BF16)
            regroup = lambda v: v.reshape(v.shape[:-1] + (2, SSD_GROUPS, SSD_HEADS_PER_GROUP)).swapaxes(-3, -2)
            pad_lanes = lambda v: jnp.zeros(v.shape[:-3] + (SSD_GROUPS, LANES), F32).at[..., :2 * SSD_HEADS_PER_GROUP].set(
                v.reshape(v.shape[:-3] + (SSD_GROUPS, 2 * SSD_HEADS_PER_GROUP)))
            w_dt = pad_lanes(regroup(w_in[:, o_dt:o_dt + 2 * n_heads])).reshape(d, SSD_GROUPS * LANES).astype(BF16)
            bias = pad_lanes(regroup(ev_dt_bias[i].reshape(2 * n_heads))).reshape(SSD_GROUPS, 1, LANES)
            alog = pad_lanes(regroup(ev_a_log[i].reshape(2 * n_heads))).reshape(SSD_GROUPS, 1, LANES)
            dsk = jnp.repeat(ev_d_skip[i], SSD_HEAD_DIM).reshape(1, d)

            main, dtp = _fused_proj(x, mrow, norm_mix_w[l], mods, 0, [w_main, w_dt], [BF16, F32])
            xbc = _ssd_conv(main, meta, ev_conv_w[i], ev_conv_b[i], d, conv_dim)
            (y_lat,) = _ssd_call(xbc, dtp, bias, alog, dsk, ss, bs, 0, state_ssd, i, lat0)
            y_ctx, st = _ssd_call(xbc, dtp, bias, alog, dsk, sp, bp, ctx_blk0, None, i, 0)
            states.append(st)
            yc = _short_conv(main, meta, ev_sconv_w[i], o_dt, o_dt + d, o_dt + 2 * d, d)
            x = _out_proj(y_lat, y_ctx, yc, ev_w_out[i].astype(BF16), x, mrow, mods, 2,
                          z_src=main, norm_w=ev_norm_w[i])
        else:
            w_in = od_w_in[i]
            wq = w_in[:, :d].reshape(d, KV_HEADS, Q_PER_KV, HEAD_DIM).swapaxes(1, 2).reshape(d, d).astype(BF16)
            wkv = w_in[:, d:d + 2 * kvw].astype(BF16)
            wglu = w_in[:, d + 2 * kvw:].astype(BF16)
            w_out = od_w_out[i]
            w_att = w_out[:d].reshape(KV_HEADS, Q_PER_KV, HEAD_DIM, d).swapaxes(0, 1).reshape(d, d)
            w_out_p = jnp.concatenate([w_att, w_out[d:]], axis=0).astype(BF16)

            q, kv, glu = _fused_proj(x, mrow, norm_mix_w[l], mods, 0, [wq, wkv, wglu], [BF16, F32, BF16])
            qn, k_att, vt_att, k_n, v_n = _qk_prep(q, kv, meta, od_q_norm_w[i], od_k_norm_w[i], cos_t, sin_t)
            ctx_k.append(k_n[n_lat_tok:].reshape(bp, sp, KV_HEADS, HEAD_DIM))
            ctx_v.append(v_n[n_lat_tok:].reshape(bp, sp, KV_HEADS, HEAD_DIM))
            ck = cache_k[:, i].reshape(bs_all * past, kvw).astype(BF16)
            cvt = cache_v[:, i].reshape(bs_all, past, kvw).swapaxes(1, 2).astype(BF16)
            per = ATT_TK // sp
            ctx_src = (k_att, pl.BlockSpec((sp, kvw), lambda s, j: (ctx_blk0 + s, 0)),
                       vt_att, pl.BlockSpec((None, kvw, sp),
                                            lambda s, j: ((ctx_blk0 + s) // per, 0, (ctx_blk0 + s) % per)), False)
            cache_src = (ck, pl.BlockSpec((past, kvw), lambda s, j: (lat0 + s, 0)),
                         cvt, pl.BlockSpec((None, kvw, past), lambda s, j: (lat0 + s, 0, 0)), False)
            lat_src = (k_att, pl.BlockSpec((ss, kvw), lambda s, j: (s, 0)),
                       vt_att, pl.BlockSpec((ss // ATT_TK, kvw, ATT_TK), lambda s, j: (s, 0, 0)), True)
            att_lat = _attn_call(qn, [cache_src, lat_src], 0, bs, ss, "attention_latent")
            att_ctx = _attn_call(qn, [ctx_src], n_lat_tok // SEG, bp, sp, "attention_ctx")
            u = _conformer_conv(glu, meta, od_dw_w[i], od_dw_b[i], od_ln_w[i], od_ln_b[i])
            x = _out_proj(att_lat, att_ctx, u, w_out_p, x, mrow, mods, 2)

        h2, ti, gt, rk, cnt = _ffn_pre(x, mrow, norm_ffn_w[l], mods, router_w[l], router_b[l])
        x = _moe(x, h2, ti, gt, rk, cnt[0, :N_EXPERTS], mrow, mods, l, w_gu, b_gu, w_dn, b_dn)

    y = _final_norm(x, final_norm_w)
    y_sample = y[:n_lat_tok].reshape(bs, ss, d)
    y_prompt = y[n_lat_tok:].reshape(bp, sp, d)
    return (y_prompt, y_sample, jnp.stack(states, axis=1), jnp.stack(ctx_k, axis=1), jnp.stack(ctx_v, axis=1))


N_STREAMS = 2


def kernel(x_prompt, x_sample, state_ssd, cache_k, cache_v, c, c_ctx, w_mod, b_mod, norm_mix_w, norm_ffn_w, ev_w_in, ev_conv_w, ev_conv_b, ev_dt_bias, ev_a_log, ev_d_skip, ev_norm_w, ev_sconv_w, ev_w_out, od_w_in, od_q_norm_w, od_k_norm_w, od_dw_w, od_dw_b, od_ln_w, od_ln_b, od_w_out, router_w, router_b, w_gu, b_gu, w_dn, b_dn, final_norm_w):
    bp, _, d = x_prompt.shape
    bs, ss, _ = x_sample.shape
    n_rows = 16
    assert 1 + bs <= n_rows
    cvec = jnp.zeros((n_rows, d), F32).at[0].set(c_ctx).at[1:1 + bs].set(c)
    mods_all = _modulation(cvec, w_mod, b_mod)
    cos_t, sin_t = _rope_tables(ss)
    n_streams = N_STREAMS if (bp % N_STREAMS == 0 and bs % N_STREAMS == 0) else 1
    outs = [_run_stream(s * (bs // n_streams), bs // n_streams, s * (bp // n_streams), bp // n_streams,
                        mods_all, cos_t, sin_t, x_prompt, x_sample, state_ssd, cache_k, cache_v,
                        norm_mix_w, norm_ffn_w, ev_w_in, ev_conv_w, ev_conv_b, ev_dt_bias, ev_a_log, ev_d_skip,
                        ev_norm_w, ev_sconv_w, ev_w_out, od_w_in, od_q_norm_w, od_k_norm_w, od_dw_w, od_dw_b,
                        od_ln_w, od_ln_b, od_w_out, router_w, router_b, w_gu, b_gu, w_dn, b_dn, final_norm_w)
            for s in range(n_streams)]
    return tuple(jnp.concatenate(parts, axis=0) for parts in zip(*outs))
```

```python
import functools

import numpy as np
import jax
import jax.numpy as jnp
from jax import lax
from jax.experimental import pallas as pl
from jax.experimental.pallas import tpu as pltpu

F32 = jnp.float32
BF16 = jnp.bfloat16
HIGHEST = lax.Precision.HIGHEST

NORM_EPS = 1e-6
GRID_W = 64
ROPE_THETA = 10000.0
HEAD_DIM = 64
KV_HEADS = 4
Q_PER_KV = 4
SSD_HEAD_DIM = 64
SSD_GROUPS = 4
SSD_HEADS_PER_GROUP = 4
SSD_STATE = 128
SSD_CHUNK = 128
SSD_CONV = 4
SCONV_K = 3
CONF_K = 31
N_EXPERTS = 32
TOP_K = 4
SWIGLU_LIMIT = 7.0
SWIGLU_ALPHA = 1.702

SEG = 256
HALO = 16
LANES = 128
SUBLANES = 8
MOE_ROWS = 256
ATT_TK = 1024
ATT_STRIP = 512
NEG_BIG = -1e30
LOG2_E = 1.4426950408889634
VMEM_LIMIT = 56 * 1024 * 1024

NT_DIMS = (((1,), (1,)), ((), ()))
TN_DIMS = (((0,), (0,)), ((), ()))


def _cparams(n_grid):
    return pltpu.CompilerParams(dimension_semantics=("arbitrary",) * n_grid,
                                vmem_limit_bytes=VMEM_LIMIT)


def _resident(shape):
    nd = len(shape)
    return pl.BlockSpec(shape, lambda *_: (0,) * nd, pipeline_mode=pl.Buffered(1))


def _silu(x):
    return x * jax.nn.sigmoid(x)


def _mod_body(c_ref, w_ref, b_ref, o_ref):
    c = c_ref[...]
    s = _silu(c).astype(BF16)
    o_ref[0] = jnp.dot(s, w_ref[0].astype(BF16), preferred_element_type=F32) + b_ref[0]


def _modulation(cvec, w_mod, b_mod):
    depth, d, n = w_mod.shape
    rows = cvec.shape[0]
    tn = 1536
    assert n % tn == 0
    return pl.pallas_call(
        _mod_body,
        grid=(depth, n // tn),
        in_specs=[pl.BlockSpec((rows, d), lambda l, j: (0, 0)),
                  pl.BlockSpec((1, d, tn), lambda l, j: (l, 0, j)),
                  pl.BlockSpec((1, 1, tn), lambda l, j: (l, 0, j))],
        out_specs=pl.BlockSpec((1, rows, tn), lambda l, j: (l, 0, j)),
        out_shape=jax.ShapeDtypeStruct((depth, rows, n), F32),
        compiler_params=_cparams(2),
        name="modulation",
    )(cvec, w_mod, b_mod.reshape(depth, 1, n))


def _normed(x, nw_ref, sh_ref, sc_ref):
    ms = jnp.mean(x * x, axis=-1, keepdims=True)
    h = x * lax.rsqrt(ms + NORM_EPS) * nw_ref[...]
    return h * (1.0 + sc_ref[0]) + sh_ref[0]


def _proj_body(mrow_ref, x_ref, nw_ref, sh_ref, sc_ref, *rest, n_out, col_chunk):
    del mrow_ref
    w_refs, o_refs = rest[:n_out], rest[n_out:]
    hb = _normed(x_ref[...], nw_ref, sh_ref, sc_ref).astype(BF16)
    for w_ref, o_ref in zip(w_refs, o_refs):
        n = w_ref.shape[1]
        for c0 in range(0, n, col_chunk):
            c1 = min(n, c0 + col_chunk)
            o_ref[:, c0:c1] = jnp.dot(hb, w_ref[:, c0:c1],
                                      preferred_element_type=F32).astype(o_ref.dtype)


def _mod_spec(d, chunk):
    return pl.BlockSpec((1, 1, d), lambda i, mrow: (mrow[i], 0, chunk))


def _fused_proj(x, mrow, norm_w, mods, shift_chunk, weights, out_dtypes):
    t, d = x.shape
    nseg = t // SEG
    n_out = len(weights)
    in_specs = [pl.BlockSpec((SEG, d), lambda i, mrow: (i, 0)),
                pl.BlockSpec((1, d), lambda i, mrow: (0, 0)),
                _mod_spec(d, shift_chunk), _mod_spec(d, shift_chunk + 1)]
    in_specs += [_resident(w.shape) for w in weights]
    out_specs = [pl.BlockSpec((SEG, w.shape[1]), lambda i, mrow: (i, 0)) for w in weights]
    out_shape = [jax.ShapeDtypeStruct((t, w.shape[1]), dt) for w, dt in zip(weights, out_dtypes)]
    return pl.pallas_call(
        functools.partial(_proj_body, n_out=n_out, col_chunk=512),
        grid_spec=pltpu.PrefetchScalarGridSpec(
            num_scalar_prefetch=1, grid=(nseg,), in_specs=in_specs, out_specs=out_specs),
        out_shape=out_shape,
        compiler_params=_cparams(1),
        name="norm_mod_proj",
    )(mrow, x, norm_w.reshape(1, d), mods, mods, *weights)


def _fill(scr, prev, cur, nxt, keep_prev, keep_next):
    scr[0:HALO, :] = prev * keep_prev
    scr[HALO:HALO + SEG, :] = cur
    scr[HALO + SEG:HALO + SEG + HALO, :] = nxt * keep_next


def _taps(scr, w_ref, c0, k_taps, left):
    acc = None
    for k in range(k_taps):
        term = scr[pl.ds(HALO - left + k, SEG), c0:c0 + LANES] * w_ref[k:k + 1, c0:c0 + LANES]
        acc = term if acc is None else acc + term
    return acc


def _keep(first_ref, last_ref):
    i = pl.program_id(0)
    return (1 - first_ref[i]).astype(F32), (1 - last_ref[i]).astype(F32)


def _ssdconv_body(first_ref, last_ref, p_ref, c_ref, n_ref, w_ref, b_ref, o_ref, scr):
    kp, kn = _keep(first_ref, last_ref)
    _fill(scr, p_ref[...].astype(F32), c_ref[...].astype(F32), n_ref[...].astype(F32), kp, kn)
    for c0 in range(0, o_ref.shape[1], LANES):
        y = _taps(scr, w_ref, c0, SSD_CONV, SSD_CONV // 2) + b_ref[:, c0:c0 + LANES]
        o_ref[:, c0:c0 + LANES] = _silu(y).astype(o_ref.dtype)


def _sconv_body(first_ref, last_ref, gp_ref, gc_ref, gn_ref, xp_ref, xc_ref, xn_ref, gb_ref,
                w_ref, o_ref, scr):
    kp, kn = _keep(first_ref, last_ref)
    f = lambda a, b: a[...].astype(F32) * b[...].astype(F32)
    _fill(scr, f(gp_ref, xp_ref), f(gc_ref, xc_ref), f(gn_ref, xn_ref), kp, kn)
    for c0 in range(0, o_ref.shape[1], LANES):
        y = _taps(scr, w_ref, c0, SCONV_K, SCONV_K // 2)
        o_ref[:, c0:c0 + LANES] = (gb_ref[:, c0:c0 + LANES].astype(F32) * y).astype(o_ref.dtype)


def _conf_body(first_ref, last_ref, ap_ref, ac_ref, an_ref, gp_ref, gc_ref, gn_ref,
               w_ref, b_ref, lnw_ref, lnb_ref, o_ref, scr, u_scr, sh_scr):
    kp, kn = _keep(first_ref, last_ref)
    f = lambda a, g: a[...].astype(F32) * jax.nn.sigmoid(g[...].astype(F32))
    _fill(scr, f(ap_ref, gp_ref), f(ac_ref, gc_ref), f(an_ref, gn_ref), kp, kn)
    span = SEG + 2 * HALO - SUBLANES
    for r in range(SUBLANES):
        sh_scr[r] = scr[pl.ds(r, span), :]
    left = CONF_K // 2
    rows = SEG // 4
    for c0 in range(0, o_ref.shape[1], LANES):
        for r0 in range(0, SEG, rows):
            acc = None
            for k in range(CONF_K):
                off = HALO - left + k
                term = (sh_scr[off % SUBLANES, pl.ds(r0 + off - off % SUBLANES, rows), c0:c0 + LANES]
                        * w_ref[k:k + 1, c0:c0 + LANES])
                acc = term if acc is None else acc + term
            u_scr[r0:r0 + rows, c0:c0 + LANES] = acc + b_ref[:, c0:c0 + LANES]
    u = u_scr[...]
    mu = jnp.mean(u, axis=-1, keepdims=True)
    uc = u - mu
    var = jnp.mean(uc * uc, axis=-1, keepdims=True)
    y = uc * lax.rsqrt(var + NORM_EPS) * lnw_ref[...] + lnb_ref[...]
    o_ref[...] = _silu(y).astype(o_ref.dtype)


def _halo_specs(t, cw, col_off):
    per = SEG // HALO
    last_blk = t // HALO - 1
    return [
        pl.BlockSpec((HALO, cw), lambda i, j, f, l: (jnp.maximum(i * per - 1, 0), col_off + j)),
        pl.BlockSpec((SEG, cw), lambda i, j, f, l: (i, col_off + j)),
        pl.BlockSpec((HALO, cw), lambda i, j, f, l: (jnp.minimum((i + 1) * per, last_blk), col_off + j)),
    ]


def _conv_call(body, name, meta, t, cw, n_col, in_specs, args, out_cols, scratch):
    nseg = t // SEG
    return pl.pallas_call(
        body,
        grid_spec=pltpu.PrefetchScalarGridSpec(
            num_scalar_prefetch=2, grid=(nseg, n_col), in_specs=in_specs,
            out_specs=pl.BlockSpec((SEG, cw), lambda i, j, f, l: (i, j)),
            scratch_shapes=scratch),
        out_shape=jax.ShapeDtypeStruct((t, out_cols), BF16),
        compiler_params=_cparams(2),
        name=name,
    )(meta["first"], meta["last"], *args)


def _colvec_spec(rows, cw):
    return pl.BlockSpec((rows, cw), lambda i, j, f, l: (0, j))


def _ssd_conv(main, meta, conv_w, conv_b, col0, width):
    t = main.shape[0]
    cw = 512
    in_specs = _halo_specs(t, cw, col0 // cw) + [_colvec_spec(SSD_CONV, cw), _colvec_spec(1, cw)]
    return _conv_call(_ssdconv_body, "ssd_conv_silu", meta, t, cw, width // cw, in_specs,
                      (main, main, main, conv_w, conv_b.reshape(1, width)), width,
                      [pltpu.VMEM((SEG + 2 * HALO, cw), F32)])


def _short_conv(main, meta, sconv_w, col_gb, col_gc, col_xc, width):
    t = main.shape[0]
    cw = 512
    in_specs = (_halo_specs(t, cw, col_gc // cw) + _halo_specs(t, cw, col_xc // cw)
                + [pl.BlockSpec((SEG, cw), lambda i, j, f, l: (i, col_gb // cw + j)),
                   _colvec_spec(SCONV_K, cw)])
    return _conv_call(_sconv_body, "short_gated_conv", meta, t, cw, width // cw, in_specs,
                      (main,) * 7 + (sconv_w,), width, [pltpu.VMEM((SEG + 2 * HALO, cw), F32)])


def _conformer_conv(glu, meta, dw_w, dw_b, ln_w, ln_b):
    t, two_w = glu.shape
    w = two_w // 2
    in_specs = (_halo_specs(t, w, 0) + _halo_specs(t, w, 1)
                + [_colvec_spec(CONF_K, w)] + [_colvec_spec(1, w)] * 3)
    return _conv_call(_conf_body, "conformer_conv", meta, t, w, 1, in_specs,
                      (glu,) * 6 + (dw_w, dw_b.reshape(1, w), ln_w.reshape(1, w), ln_b.reshape(1, w)),
                      w, [pltpu.VMEM((SEG + 2 * HALO, w), F32), pltpu.VMEM((SEG, w), F32),
                       pltpu.VMEM((SUBLANES, SEG + 2 * HALO - SUBLANES, w), F32)])


def _softplus(x):
    return jnp.maximum(x, 0.0) + jnp.log1p(jnp.exp(-jnp.abs(x)))


def _ssd_body(*refs, nc, has_h0, want_state):
    it = iter(refs)
    x_ref, b_ref, c_ref, dt_ref, bias_ref, alog_ref, dsk_ref = (next(it) for _ in range(7))
    h0_ref = next(it) if has_h0 else None
    y_ref = next(it)
    st_ref = next(it) if want_state else None
    ybuf, sf_scr, sb_scr, decf_scr, decb_scr, xsf_scr, xsb_scr, tot_scr = (next(it) for _ in range(8))

    q = SSD_CHUNK
    hp = SSD_HEADS_PER_GROUP
    width = hp * SSD_HEAD_DIM
    row = lax.broadcasted_iota(jnp.int32, (q, q), 0)
    col = lax.broadcasted_iota(jnp.int32, (q, q), 1)
    lower = row >= col
    upper = col >= row
    lower_f = lower.astype(F32)
    lane_w = lax.broadcasted_iota(jnp.int32, (q, width), 1) // SSD_HEAD_DIM
    sub_w = lax.broadcasted_iota(jnp.int32, (width, 1), 0) // SSD_HEAD_DIM

    def per_head_cols(mat, base):
        out = mat[:, base + hp - 1:base + hp]
        for r in range(hp - 2, -1, -1):
            out = jnp.where(lane_w == r, mat[:, base + r:base + r + 1], out)
        return out

    def per_head_rows(rowvec, base):
        out = rowvec[:, base + hp - 1:base + hp]
        for r in range(hp - 2, -1, -1):
            out = jnp.where(sub_w == r, rowvec[:, base + r:base + r + 1], out)
        return out

    a_row = -jnp.exp(alog_ref[0])
    bias_row = bias_ref[0]

    dsk = dsk_ref[...]

    def intra(c, carry):
        rows = pl.ds(pl.multiple_of(c * q, q), q)
        dt = _softplus(dt_ref[rows, :] + bias_row)
        a = dt * a_row
        cs = jnp.dot(lower_f, a, precision=HIGHEST, preferred_element_type=F32)
        ex = cs - a
        tot = cs[q - 1:q, :]
        xb = x_ref[rows, :]
        x = xb.astype(F32)
        cb = lax.dot_general(c_ref[rows, :], b_ref[rows, :], NT_DIMS, preferred_element_type=F32)
        cs_t, ex_t, dt_t = cs.T, ex.T, dt.T
        y = x * dsk
        for r in range(hp):
            dec_f = jnp.exp(jnp.where(lower, cs[:, r:r + 1] - cs_t[r:r + 1, :], NEG_BIG))
            dec_b = jnp.exp(jnp.where(upper, ex_t[hp + r:hp + r + 1, :] - ex[:, hp + r:hp + r + 1], NEG_BIG))
            wm = cb * (dec_f * dt_t[r:r + 1, :] + dec_b * dt_t[hp + r:hp + r + 1, :])
            yd = jnp.dot(wm.astype(BF16), xb, preferred_element_type=F32)
            y = y + jnp.where(lane_w == r, yd, 0.0)
        ybuf[rows, :] = y
        decf_scr[rows, :] = per_head_cols(jnp.exp(cs), 0)
        decb_scr[rows, :] = per_head_cols(jnp.exp(tot - ex), hp)
        xsf_scr[rows, :] = (x * per_head_cols(jnp.exp(tot - cs) * dt, 0)).astype(BF16)
        xsb_scr[rows, :] = (x * per_head_cols(jnp.exp(ex) * dt, hp)).astype(BF16)
        tot_scr[pl.ds(c, 1), :] = jnp.exp(tot)
        return carry

    lax.fori_loop(0, nc, intra, 0, unroll=2)

    if has_h0:
        sf_scr[...] = h0_ref[0].reshape(width, SSD_STATE)
        sb_scr[...] = h0_ref[1].reshape(width, SSD_STATE)
    else:
        sf_scr[...] = jnp.zeros_like(sf_scr)
        sb_scr[...] = jnp.zeros_like(sb_scr)

    def one_direction(c, s_scr, dec_scr, xs_scr, base):
        rows = pl.ds(pl.multiple_of(c * q, q), q)
        s_prev = s_scr[...]
        y_off = lax.dot_general(c_ref[rows, :], s_prev.astype(BF16), NT_DIMS, preferred_element_type=F32)
        ybuf[rows, :] = ybuf[rows, :] + y_off * dec_scr[rows, :]
        contrib = lax.dot_general(xs_scr[rows, :], b_ref[rows, :], TN_DIMS, preferred_element_type=F32)
        s_scr[...] = s_prev * per_head_rows(tot_scr[pl.ds(c, 1), :], base) + contrib

    def recur(i, carry):
        one_direction(i, sf_scr, decf_scr, xsf_scr, 0)
        one_direction(nc - 1 - i, sb_scr, decb_scr, xsb_scr, hp)
        return carry

    lax.fori_loop(0, nc, recur, 0, unroll=2)

    y_ref[...] = ybuf[...].astype(y_ref.dtype)
    if want_state:
        st_ref[0] = sf_scr[...].reshape(hp, SSD_HEAD_DIM, SSD_STATE)
        st_ref[1] = sb_scr[...].reshape(hp, SSD_HEAD_DIM, SSD_STATE)


def _ssd_call(xbc, dtp, bias, alog, dsk, seq_len, n_seq, row_blk0, h0, layer_idx, seq0):
    inner = SSD_GROUPS * SSD_HEADS_PER_GROUP * SSD_HEAD_DIM
    width = SSD_HEADS_PER_GROUP * SSD_HEAD_DIM
    nb = inner // SSD_STATE
    nc = seq_len // SSD_CHUNK
    want_state = h0 is None
    in_specs = [
        pl.BlockSpec((seq_len, width), lambda s, g: (row_blk0 + s, g)),
        pl.BlockSpec((seq_len, SSD_STATE), lambda s, g: (row_blk0 + s, nb + g)),
        pl.BlockSpec((seq_len, SSD_STATE), lambda s, g: (row_blk0 + s, nb + SSD_GROUPS + g)),
        pl.BlockSpec((seq_len, LANES), lambda s, g: (row_blk0 + s, g)),
        pl.BlockSpec((1, 1, LANES), lambda s, g: (g, 0, 0)),
        pl.BlockSpec((1, 1, LANES), lambda s, g: (g, 0, 0)),
        pl.BlockSpec((1, width), lambda s, g: (0, g)),
    ]
    args = [xbc, xbc, xbc, dtp, bias, alog, dsk]
    if not want_state:
        in_specs.append(pl.BlockSpec((None, None, 2, SSD_HEADS_PER_GROUP, SSD_HEAD_DIM, SSD_STATE),
                                     lambda s, g: (seq0 + s, layer_idx, 0, g, 0, 0)))
        args.append(h0)
    out_specs = [pl.BlockSpec((seq_len, width), lambda s, g: (s, g))]
    out_shape = [jax.ShapeDtypeStruct((n_seq * seq_len, inner), BF16)]
    if want_state:
        out_specs.append(pl.BlockSpec((None, 2, SSD_HEADS_PER_GROUP, SSD_HEAD_DIM, SSD_STATE),
                                      lambda s, g: (s, 0, g, 0, 0)))
        out_shape.append(jax.ShapeDtypeStruct(
            (n_seq, 2, SSD_GROUPS * SSD_HEADS_PER_GROUP, SSD_HEAD_DIM, SSD_STATE), F32))
    return pl.pallas_call(
        functools.partial(_ssd_body, nc=nc, has_h0=not want_state, want_state=want_state),
        grid=(n_seq, SSD_GROUPS),
        in_specs=in_specs, out_specs=out_specs, out_shape=out_shape,
        scratch_shapes=[pltpu.VMEM((seq_len, width), F32),
                        pltpu.VMEM((width, SSD_STATE), F32), pltpu.VMEM((width, SSD_STATE), F32),
                        pltpu.VMEM((seq_len, width), F32), pltpu.VMEM((seq_len, width), F32),
                        pltpu.VMEM((seq_len, width), BF16), pltpu.VMEM((seq_len, width), BF16),
                        pltpu.VMEM((max(nc, SUBLANES), LANES), F32)],
        compiler_params=_cparams(2),
        name="ssd_scan_ctx" if want_state else "ssd_scan_latent",
    )(*args)


def _outproj_body(mrow_ref, al_ref, ac_ref, *refs, gated_norm):
    a = jnp.where(mrow_ref[pl.program_id(0)] > 0, al_ref[...], ac_ref[...])
    if gated_norm:
        z_ref, nw_ref, b_ref, w_ref, x_ref, g_ref, o_ref = refs
        y = a.astype(F32) * _silu(z_ref[...].astype(F32))
        ms = jnp.mean(y * y, axis=-1, keepdims=True)
        a = (y * lax.rsqrt(ms + NORM_EPS) * nw_ref[...]).astype(BF16)
    else:
        b_ref, w_ref, x_ref, g_ref, o_ref = refs
    d = a.shape[1]
    out = jnp.dot(a, w_ref[0:d, :], preferred_element_type=F32)
    out = out + jnp.dot(b_ref[...], w_ref[d:, :], preferred_element_type=F32)
    o_ref[...] = x_ref[...] + g_ref[0] * out


def _out_proj(a_lat, a_ctx, b, w, x, mrow, mods, gate_chunk, z_src=None, norm_w=None):
    t, d = x.shape
    nseg = t // SEG
    n_lat_seg = a_lat.shape[0] // SEG
    row = lambda i, mrow: (i, 0)
    in_specs = [pl.BlockSpec((SEG, d), lambda i, mrow: (jnp.minimum(i, n_lat_seg - 1), 0)),
                pl.BlockSpec((SEG, d), lambda i, mrow: (jnp.maximum(i - n_lat_seg, 0), 0))]
    args = [a_lat, a_ctx]
    if z_src is not None:
        in_specs += [pl.BlockSpec((SEG, d), row), pl.BlockSpec((1, d), lambda i, mrow: (0, 0))]
        args += [z_src, norm_w.reshape(1, d)]
    in_specs += [pl.BlockSpec((SEG, d), row), _resident(w.shape), pl.BlockSpec((SEG, d), row),
                 _mod_spec(d, gate_chunk)]
    args += [b, w, x, mods]
    return pl.pallas_call(
        functools.partial(_outproj_body, gated_norm=z_src is not None),
        grid_spec=pltpu.PrefetchScalarGridSpec(
            num_scalar_prefetch=1, grid=(nseg,), in_specs=in_specs,
            out_specs=pl.BlockSpec((SEG, d), row)),
        out_shape=jax.ShapeDtypeStruct((t, d), F32),
        compiler_params=_cparams(1),
        name="mixer_out_proj",
    )(mrow, *args)


def _qkprep_body(lat_ref, pos_ref, q_ref, kv_ref, qw_ref, kw_ref, cos_ref, sin_ref,
                 qo_ref, ko_ref, vo_ref, kn_ref, vn_ref):
    del pos_ref
    is_lat = lat_ref[pl.program_id(0)] > 0
    r = lax.broadcasted_iota(jnp.int32, (LANES, LANES), 0) // HEAD_DIM
    c = lax.broadcasted_iota(jnp.int32, (LANES, LANES), 1) // HEAD_DIM
    head_mean = jnp.where(r == c, 1.0 / HEAD_DIM, 0.0).astype(F32)
    lane = lax.broadcasted_iota(jnp.int32, (SEG, LANES), 1)
    first_half = (lane % HEAD_DIM) < (HEAD_DIM // 2)
    cos = cos_ref[...]
    sin = sin_ref[...]

    def norm_rope(xs, w):
        ms = jnp.dot(xs * xs, head_mean, precision=HIGHEST, preferred_element_type=F32)
        xn = xs * lax.rsqrt(ms + NORM_EPS) * w
        rot = jnp.where(first_half, pltpu.roll(xn, LANES - HEAD_DIM // 2, 1),
                        pltpu.roll(xn, HEAD_DIM // 2, 1))
        return xn, jnp.where(is_lat, xn * cos + rot * sin, xn)

    scale = HEAD_DIM ** -0.5 * LOG2_E
    for c0 in range(0, q_ref.shape[1], LANES):
        _, qr = norm_rope(q_ref[:, c0:c0 + LANES].astype(F32), qw_ref[...])
        qo_ref[:, c0:c0 + LANES] = (qr * scale).astype(qo_ref.dtype)
    kvw = ko_ref.shape[1]
    for c0 in range(0, kvw, LANES):
        kn, kr = norm_rope(kv_ref[:, c0:c0 + LANES], kw_ref[...])
        kn_ref[:, c0:c0 + LANES] = kn
        ko_ref[:, c0:c0 + LANES] = kr.astype(ko_ref.dtype)
    v = kv_ref[:, kvw:]
    vn_ref[...] = v
    vo_ref[...] = v.T.astype(vo_ref.dtype)


def _qk_prep(q, kv, meta, q_norm_w, k_norm_w, cos_t, sin_t):
    t, qd = q.shape
    kvd = kv.shape[1] // 2
    nseg = t // SEG
    row = lambda i, lat, pos: (i, 0)
    tab = lambda i, lat, pos: (pos[i] * lat[i], 0)
    tile2 = lambda w: jnp.tile(w, LANES // HEAD_DIM).reshape(1, LANES)
    per_chunk = ATT_TK // SEG
    return pl.pallas_call(
        _qkprep_body,
        grid_spec=pltpu.PrefetchScalarGridSpec(
            num_scalar_prefetch=2, grid=(nseg,),
            in_specs=[pl.BlockSpec((SEG, qd), row), pl.BlockSpec((SEG, 2 * kvd), row),
                      pl.BlockSpec((1, LANES), lambda i, lat, pos: (0, 0)),
                      pl.BlockSpec((1, LANES), lambda i, lat, pos: (0, 0)),
                      pl.BlockSpec((SEG, LANES), tab), pl.BlockSpec((SEG, LANES), tab)],
            out_specs=[pl.BlockSpec((SEG, qd), row), pl.BlockSpec((SEG, kvd), row),
                       pl.BlockSpec((None, kvd, SEG), lambda i, lat, pos: (i // per_chunk, 0, i % per_chunk)),
                       pl.BlockSpec((SEG, kvd), row), pl.BlockSpec((SEG, kvd), row)]),
        out_shape=[jax.ShapeDtypeStruct((t, qd), BF16), jax.ShapeDtypeStruct((t, kvd), BF16),
                   jax.ShapeDtypeStruct((t // ATT_TK, kvd, ATT_TK), BF16),
                   jax.ShapeDtypeStruct((t, kvd), F32), jax.ShapeDtypeStruct((t, kvd), F32)],
        compiler_params=_cparams(1),
        name="qk_norm_rope",
    )(meta["lat"], meta["pos"], q, kv, tile2(q_norm_w), tile2(k_norm_w), cos_t, sin_t)


def _rope_tables(seq_len):
    rows = seq_len // GRID_W
    rowp = jnp.repeat(jnp.arange(rows), GRID_W).astype(F32)
    colp = jnp.tile(jnp.arange(GRID_W), rows).astype(F32)
    axis = HEAD_DIM // 2
    inv = ROPE_THETA ** (-jnp.arange(0, axis, 2, dtype=F32) / axis)
    ang = jnp.concatenate([rowp[:, None] * inv, colp[:, None] * inv], axis=-1)
    cos, sin = jnp.cos(ang), jnp.sin(ang)
    cos_h = jnp.concatenate([cos, cos], axis=-1)
    sin_h = jnp.concatenate([-sin, sin], axis=-1)
    rep = LANES // HEAD_DIM
    return jnp.tile(cos_h, (1, rep)), jnp.tile(sin_h, (1, rep))


ONES_ROWS = 16


def _attn_body(*refs, chunked, tq):
    n_src = len(chunked)
    q_ref = refs[0]
    kv_refs = refs[1:1 + 2 * n_src]
    pos = 1 + 2 * n_src
    o_ref = refs[pos]
    qs_scr, acc_scr, m_scr = refs[pos + 1:]
    kvw = KV_HEADS * HEAD_DIM
    lane_g = lax.broadcasted_iota(jnp.int32, (tq, kvw), 1) // HEAD_DIM

    for g in range(KV_HEADS):
        for r in range(Q_PER_KV):
            qr = q_ref[:, kvw * r:kvw * (r + 1)]
            qs_scr[g, r * tq:(r + 1) * tq, :] = jnp.where(lane_g == g, qr, jnp.zeros_like(qr))
    m_scr[...] = jnp.full_like(m_scr, NEG_BIG)
    acc_scr[...] = jnp.zeros_like(acc_scr)

    units = [(g, c) for g in range(KV_HEADS) for c in range(Q_PER_KV * tq // ATT_STRIP)]

    def step(kc, vt):
        ones = jnp.ones((ONES_ROWS, kc.shape[0]), BF16)

        def scores(u):
            g, c = u
            return lax.dot_general(kc, qs_scr[g, c * ATT_STRIP:(c + 1) * ATT_STRIP, :], NT_DIMS,
                                   preferred_element_type=F32)

        def softmax(u, s):
            g, c = u
            cols = slice(c * ATT_STRIP, (c + 1) * ATT_STRIP)
            m_prev = m_scr[g, :, cols]
            m_new = jnp.maximum(m_prev, jnp.max(s, axis=0, keepdims=True))
            m_scr[g, :, cols] = m_new
            return jnp.exp2(s - m_new).astype(BF16), jnp.exp2(m_prev - m_new)

        def accumulate(u, p, alpha):
            g, c = u
            cols = slice(c * ATT_STRIP, (c + 1) * ATT_STRIP)
            lhs = jnp.concatenate([vt[HEAD_DIM * g:HEAD_DIM * (g + 1), :], ones], axis=0)
            pv = jnp.dot(lhs, p, preferred_element_type=F32)
            acc_scr[g, :, cols] = acc_scr[g, :, cols] * alpha + pv

        s_cur = scores(units[0])
        pending = None
        for i, u in enumerate(units):
            s_next = scores(units[i + 1]) if i + 1 < len(units) else None
            if pending is not None:
                accumulate(*pending)
            pending = (u,) + softmax(u, s_cur)
            s_cur = s_next
        accumulate(*pending)

    for si, is_chunked in enumerate(chunked):
        k_ref, vt_ref = kv_refs[2 * si], kv_refs[2 * si + 1]
        if not is_chunked:
            step(k_ref[...], vt_ref[...])
        else:
            tk = vt_ref.shape[2]

            def body(j, carry, k_ref=k_ref, vt_ref=vt_ref, tk=tk):
                rows = pl.ds(pl.multiple_of(j * tk, tk), tk)
                step(k_ref[rows, :], vt_ref[j])
                return carry
            lax.fori_loop(0, vt_ref.shape[0], body, 0)

    for r in range(Q_PER_KV):
        cols = slice(r * tq, (r + 1) * tq)
        parts = [acc_scr[g, 0:HEAD_DIM, cols] / acc_scr[g, HEAD_DIM:HEAD_DIM + 1, cols]
                 for g in range(KV_HEADS)]
        o_ref[:, kvw * r:kvw * (r + 1)] = jnp.concatenate(parts, axis=0).T.astype(o_ref.dtype)


def _attn_call(q, srcs, q_blk0, n_seq, q_len, name):
    qd = q.shape[1]
    tq = SEG
    kvw = KV_HEADS * HEAD_DIM
    nq = q_len // tq
    in_specs = [pl.BlockSpec((tq, qd), lambda s, j: (q_blk0 + s * nq + j, 0))]
    args = [q]
    for k, k_spec, vt, vt_spec, _ in srcs:
        in_specs += [k_spec, vt_spec]
        args += [k, vt]
    rows4 = Q_PER_KV * tq
    return pl.pallas_call(
        functools.partial(_attn_body, chunked=tuple(s[4] for s in srcs), tq=tq),
        grid=(n_seq, nq),
        in_specs=in_specs,
        out_specs=pl.BlockSpec((tq, qd), lambda s, j: (s * nq + j, 0)),
        out_shape=jax.ShapeDtypeStruct((n_seq * q_len, qd), BF16),
        scratch_shapes=[pltpu.VMEM((KV_HEADS, rows4, kvw), BF16),
                        pltpu.VMEM((KV_HEADS, HEAD_DIM + ONES_ROWS, rows4), F32),
                        pltpu.VMEM((KV_HEADS, 1, rows4), F32)],
        compiler_params=_cparams(2),
        name=name,
    )(*args)


def _ffnpre_body(mrow_ref, x_ref, nw_ref, sh_ref, sc_ref, rw_ref, rb_ref,
                 h_ref, ti_ref, gt_ref, rk_ref, cnt_ref, cnt_scr):
    del mrow_ref

    @pl.when(pl.program_id(0) == 0)
    def _():
        cnt_scr[...] = jnp.zeros_like(cnt_scr)

    h = _normed(x_ref[...], nw_ref, sh_ref, sc_ref)
    hb = h.astype(BF16)
    h_ref[...] = hb
    logits = jnp.dot(hb, rw_ref[...], preferred_element_type=F32) + rb_ref[...]
    lane = lax.broadcasted_iota(jnp.int32, logits.shape, 1)
    work = logits
    vals, idxs = [], []
    for _ in range(TOP_K):
        m = jnp.max(work, axis=1, keepdims=True)
        idx = jnp.min(jnp.where(work == m, lane, LANES), axis=1, keepdims=True)
        vals.append(m)
        idxs.append(idx)
        work = jnp.where(lane == idx, NEG_BIG, work)
    es = [jnp.exp(v - vals[0]) for v in vals]
    den = es[0]
    for e in es[1:]:
        den = den + e
    ti = jnp.zeros(logits.shape, jnp.int32)
    gt = jnp.zeros(logits.shape, F32)
    for k in range(TOP_K):
        ti = jnp.where(lane == k, idxs[k], ti)
        gt = jnp.where(lane == k, es[k] / den, gt)
    ti_ref[...] = ti
    gt_ref[...] = gt

    onehot = jnp.zeros(logits.shape, F32)
    for k in range(TOP_K):
        onehot = onehot + (lane == idxs[k]).astype(F32)
    n = logits.shape[0]
    earlier = (lax.broadcasted_iota(jnp.int32, (n, n), 0) > lax.broadcasted_iota(jnp.int32, (n, n), 1))
    before = jnp.dot(earlier.astype(BF16), onehot.astype(BF16), preferred_element_type=F32) + cnt_scr[...]
    rk = jnp.zeros(logits.shape, jnp.int32)
    for k in range(TOP_K):
        r_k = jnp.sum(jnp.where(lane == idxs[k], before, 0.0), axis=1, keepdims=True)
        rk = jnp.where(lane == k, r_k.astype(jnp.int32), rk)
    rk_ref[...] = rk
    cnt_scr[...] = cnt_scr[...] + jnp.sum(onehot, axis=0, keepdims=True)
    cnt_ref[...] = cnt_scr[...]


def _ffn_pre(x, mrow, norm_w, mods, router_w, router_b):
    t, d = x.shape
    nseg = t // SEG
    ne = router_w.shape[1]
    rw = jnp.zeros((d, LANES), F32).at[:, :ne].set(router_w).astype(BF16)
    rb = jnp.full((1, LANES), NEG_BIG, F32).at[0, :ne].set(router_b)
    row = lambda i, mrow: (i, 0)
    const = lambda i, mrow: (0, 0)
    return pl.pallas_call(
        _ffnpre_body,
        grid_spec=pltpu.PrefetchScalarGridSpec(
            num_scalar_prefetch=1, grid=(nseg,),
            in_specs=[pl.BlockSpec((SEG, d), row), pl.BlockSpec((1, d), const),
                      _mod_spec(d, 3), _mod_spec(d, 4),
                      pl.BlockSpec((d, LANES), const), pl.BlockSpec((1, LANES), const)],
            out_specs=[pl.BlockSpec((SEG, d), row), pl.BlockSpec((SEG, LANES), row),
                       pl.BlockSpec((SEG, LANES), row), pl.BlockSpec((SEG, LANES), row),
                       pl.BlockSpec((1, LANES), const)],
            scratch_shapes=[pltpu.VMEM((1, LANES), F32)]),
        out_shape=[jax.ShapeDtypeStruct((t, d), BF16), jax.ShapeDtypeStruct((t, LANES), jnp.int32),
                   jax.ShapeDtypeStruct((t, LANES), F32), jax.ShapeDtypeStruct((t, LANES), jnp.int32),
                   jax.ShapeDtypeStruct((1, LANES), F32)],
        compiler_params=_cparams(1),
        name="ffn_norm_router_topk",
    )(mrow, x, norm_w.reshape(1, d), mods, mods, rw, rb)


def _experts_body(be_ref, nu_ref, x_ref, wgu_ref, bgu_ref, wdn_ref, bdn_ref, o_ref, wgu_bf, wdn_bf):
    i = pl.program_id(0)
    live = i < nu_ref[0]
    new_expert = jnp.logical_or(i == 0, be_ref[i] != be_ref[jnp.maximum(i - 1, 0)])

    @pl.when(jnp.logical_and(live, new_expert))
    def _():
        wgu_bf[...] = wgu_ref[...].astype(BF16)
        wdn_bf[...] = wdn_ref[...].astype(BF16)

    @pl.when(live)
    def _():
        hg = jnp.dot(x_ref[...], wgu_bf[...], preferred_element_type=F32) + bgu_ref[...]
        dff = hg.shape[1] // 2
        g = jnp.minimum(hg[:, :dff], SWIGLU_LIMIT)
        u = jnp.clip(hg[:, dff:], -SWIGLU_LIMIT, SWIGLU_LIMIT)
        act = g * jax.nn.sigmoid(SWIGLU_ALPHA * g) * (u + 1.0)
        y = jnp.dot(act.astype(BF16), wdn_bf[...], preferred_element_type=F32) + bdn_ref[...]
        o_ref[...] = y.astype(o_ref.dtype)

    @pl.when(i >= nu_ref[0])
    def _():
        o_ref[...] = jnp.zeros_like(o_ref)


def _experts(xg, block_e, n_used, layer, w_gu, b_gu, w_dn, b_dn):
    n_slots, d = xg.shape
    n_blocks = n_slots // MOE_ROWS
    depth, ne, _, two_f = w_gu.shape
    dff = two_f // 2
    b_gu = b_gu.reshape(depth, ne, 1, two_f)
    b_dn = b_dn.reshape(depth, ne, 1, d)
    return pl.pallas_call(
        _experts_body,
        grid_spec=pltpu.PrefetchScalarGridSpec(
            num_scalar_prefetch=2, grid=(n_blocks,),
            in_specs=[pl.BlockSpec((MOE_ROWS, d), lambda i, be, nu: (i, 0)),
                      pl.BlockSpec((None, None, d, two_f), lambda i, be, nu: (layer, be[i], 0, 0)),
                      pl.BlockSpec((None, None, 1, two_f), lambda i, be, nu: (layer, be[i], 0, 0)),
                      pl.BlockSpec((None, None, dff, d), lambda i, be, nu: (layer, be[i], 0, 0)),
                      pl.BlockSpec((None, None, 1, d), lambda i, be, nu: (layer, be[i], 0, 0))],
            out_specs=pl.BlockSpec((MOE_ROWS, d), lambda i, be, nu: (i, 0)),
            scratch_shapes=[pltpu.VMEM((d, two_f), BF16), pltpu.VMEM((dff, d), BF16)]),
        out_shape=jax.ShapeDtypeStruct((n_slots, d), BF16),
        compiler_params=_cparams(1),
        name="expert_swiglu",
    )(block_e, n_used, xg, w_gu, b_gu, w_dn, b_dn)


def _combine_body(mrow_ref, x_ref, y0_ref, y1_ref, y2_ref, y3_ref, gt_ref, g2_ref, o_ref):
    del mrow_ref
    gt = gt_ref[...]
    f = None
    for k, y_ref in enumerate((y0_ref, y1_ref, y2_ref, y3_ref)):
        term = y_ref[...].astype(F32) * gt[:, k:k + 1]
        f = term if f is None else f + term
    o_ref[...] = x_ref[...] + g2_ref[0] * f


def _combine(x, yg, gates, mrow, mods):
    t, d = x.shape
    nseg = t // SEG
    row = lambda i, mrow: (i, 0)
    assert TOP_K == 4
    in_specs = [pl.BlockSpec((SEG, d), row)]
    in_specs += [pl.BlockSpec((SEG, d), lambda i, mrow, k=k: (k * nseg + i, 0)) for k in range(TOP_K)]
    in_specs += [pl.BlockSpec((SEG, LANES), row), _mod_spec(d, 5)]
    return pl.pallas_call(
        _combine_body,
        grid_spec=pltpu.PrefetchScalarGridSpec(
            num_scalar_prefetch=1, grid=(nseg,), in_specs=in_specs,
            out_specs=pl.BlockSpec((SEG, d), row)),
        out_shape=jax.ShapeDtypeStruct((t, d), F32),
        compiler_params=_cparams(1),
        name="moe_combine_residual",
    )(mrow, x, yg, yg, yg, yg, gates, mods)


def _moe(x, h2, top_i, gates, rank, counts, mrow, mods, layer, w_gu, b_gu, w_dn, b_dn):
    t, d = x.shape
    tk = t * TOP_K
    bm = MOE_ROWS
    counts = counts.astype(jnp.int32)
    padded = (counts + bm - 1) // bm * bm
    pad_end = jnp.cumsum(padded)
    pad_start = pad_end - padded
    start = jnp.cumsum(counts) - counts
    experts = jnp.arange(N_EXPERTS, dtype=jnp.int32)
    ti4 = top_i[:, :TOP_K]
    dest = rank[:, :TOP_K] + jnp.sum(jnp.where(ti4[:, :, None] == experts, pad_start, 0), axis=-1)
    n_blocks = (tk + N_EXPERTS * (bm - 1) + bm - 1) // bm
    blk_start = jnp.arange(n_blocks, dtype=jnp.int32) * bm
    block_e = jnp.minimum(jnp.sum(pad_end[None, :] <= blk_start[:, None], axis=1),
                          N_EXPERTS - 1).astype(jnp.int32)
    n_used = (pad_end[-1:] // bm).astype(jnp.int32)
    order = jnp.argsort(ti4.reshape(tk), stable=True).astype(jnp.int32)
    off = (blk_start - pad_start[block_e])[:, None] + jnp.arange(bm, dtype=jnp.int32)
    src = jnp.where(off < counts[block_e][:, None], start[block_e][:, None] + off, 0)
    slot_tok = order[src.reshape(n_blocks * bm)] // TOP_K
    xg = h2[slot_tok]
    yb = _experts(xg, block_e, n_used, layer, w_gu, b_gu, w_dn, b_dn)
    yg = yb[dest.T.reshape(tk)]
    return _combine(x, yg, gates, mrow, mods)


def _final_body(x_ref, w_ref, o_ref):
    x = x_ref[...]
    ms = jnp.mean(x * x, axis=-1, keepdims=True)
    o_ref[...] = x * lax.rsqrt(ms + NORM_EPS) * w_ref[...]


def _final_norm(x, w):
    t, d = x.shape
    return pl.pallas_call(
        _final_body,
        grid=(t // SEG,),
        in_specs=[pl.BlockSpec((SEG, d), lambda i: (i, 0)), pl.BlockSpec((1, d), lambda i: (0, 0))],
        out_specs=pl.BlockSpec((SEG, d), lambda i: (i, 0)),
        out_shape=jax.ShapeDtypeStruct((t, d), F32),
        compiler_params=_cparams(1),
        name="final_rmsnorm",
    )(x, w.reshape(1, d))


def _segment_meta(n_lat, ss, lat0, n_ctx, sp):
    mrow, first, last, pos, lat = [], [], [], [], []
    for n_seq, length, is_lat in ((n_lat, ss, 1), (n_ctx, sp, 0)):
        per = length // SEG
        for s in range(n_seq):
            for j in range(per):
                mrow.append(1 + lat0 + s if is_lat else 0)
                first.append(int(j == 0))
                last.append(int(j == per - 1))
                pos.append(j)
                lat.append(is_lat)
    as_arr = lambda v: jnp.asarray(np.asarray(v, np.int32))
    return dict(mrow=as_arr(mrow), first=as_arr(first), last=as_arr(last), pos=as_arr(pos),
                lat=as_arr(lat), mrow_np=np.asarray(mrow, np.int32))


def _run_stream(lat0, bs, ctx0, bp, mods_all, cos_t, sin_t, x_prompt, x_sample, state_ssd, cache_k, cache_v, norm_mix_w, norm_ffn_w, ev_w_in, ev_conv_w, ev_conv_b, ev_dt_bias, ev_a_log, ev_d_skip, ev_norm_w, ev_sconv_w, ev_w_out, od_w_in, od_q_norm_w, od_k_norm_w, od_dw_w, od_dw_b, od_ln_w, od_ln_b, od_w_out, router_w, router_b, w_gu, b_gu, w_dn, b_dn, final_norm_w):
    _, sp, d = x_prompt.shape
    bs_all, ss, _ = x_sample.shape
    depth, n_rows = mods_all.shape[0], mods_all.shape[1]
    n_lat_tok = bs * ss
    n_ctx_tok = bp * sp
    past = cache_k.shape[2]
    kvw = KV_HEADS * HEAD_DIM
    assert sp % SEG == 0 and ss % SEG == 0 and past % SEG == 0 and sp % SSD_CHUNK == 0
    assert ss % ATT_TK == 0 and ATT_TK % sp == 0 and (n_lat_tok + n_ctx_tok) % ATT_TK == 0
    assert d == SSD_GROUPS * SSD_HEADS_PER_GROUP * SSD_HEAD_DIM

    meta = _segment_meta(bs, ss, lat0, bp, sp)
    mrow = meta["mrow"]
    x = jnp.concatenate([x_sample[lat0:lat0 + bs].reshape(n_lat_tok, d),
                         x_prompt[ctx0:ctx0 + bp].reshape(n_ctx_tok, d)], axis=0)
    ctx_blk0 = n_lat_tok // sp

    n_heads = SSD_GROUPS * SSD_HEADS_PER_GROUP
    conv_dim = d + 2 * SSD_GROUPS * SSD_STATE
    states, ctx_k, ctx_v = [], [], []
    for l in range(depth):
        mods = mods_all[l].reshape(n_rows, 1, 6 * d)
        i = l // 2
        if l % 2 == 0:
            w_in = ev_w_in[i]
            o_dt = d + conv_dim
            w_main = jnp.concatenate([w_in[:, :o_dt], w_in[:, o_dt + 2 * n_heads:]], axis=1).astype(BF16)
            regroup = lambda v: v.reshape(v.shape[:-1] + (2, SSD_GROUPS, SSD_HEADS_PER_GROUP)).swapaxes(-3, -2)
            pad_lanes = lambda v: jnp.zeros(v.shape[:-3] + (SSD_GROUPS, LANES), F32).at[..., :2 * SSD_HEADS_PER_GROUP].set(
                v.reshape(v.shape[:-3] + (SSD_GROUPS, 2 * SSD_HEADS_PER_GROUP)))
            w_dt = pad_lanes(regroup(w_in[:, o_dt:o_dt + 2 * n_heads])).reshape(d, SSD_GROUPS * LANES).astype(BF16)
            bias = pad_lanes(regroup(ev_dt_bias[i].reshape(2 * n_heads))).reshape(SSD_GROUPS, 1, LANES)
            alog = pad_lanes(regroup(ev_a_log[i].reshape(2 * n_heads))).reshape(SSD_GROUPS, 1, LANES)
            dsk = jnp.repeat(ev_d_skip[i], SSD_HEAD_DIM).reshape(1, d)

            main, dtp = _fused_proj(x, mrow, norm_mix_w[l], mods, 0, [w_main, w_dt], [BF16, F32])
            xbc = _ssd_conv(main, meta, ev_conv_w[i], ev_conv_b[i], d, conv_dim)
            (y_lat,) = _ssd_call(xbc, dtp, bias, alog, dsk, ss, bs, 0, state_ssd, i, lat0)
            y_ctx, st = _ssd_call(xbc, dtp, bias, alog, dsk, sp, bp, ctx_blk0, None, i, 0)
            states.append(st)
            yc = _short_conv(main, meta, ev_sconv_w[i], o_dt, o_dt + d, o_dt + 2 * d, d)
            x = _out_proj(y_lat, y_ctx, yc, ev_w_out[i].astype(BF16), x, mrow, mods, 2,
                          z_src=main, norm_w=ev_norm_w[i])
        else:
            w_in = od_w_in[i]
            wq = w_in[:, :d].reshape(d, KV_HEADS, Q_PER_KV, HEAD_DIM).swapaxes(1, 2).reshape(d, d).astype(BF16)
            wkv = w_in[:, d:d + 2 * kvw].astype(BF16)
            wglu = w_in[:, d + 2 * kvw:].astype(BF16)
            w_out = od_w_out[i]
            w_att = w_out[:d].reshape(KV_HEADS, Q_PER_KV, HEAD_DIM, d).swapaxes(0, 1).reshape(d, d)
            w_out_p = jnp.concatenate([w_att, w_out[d:]], axis=0).astype(BF16)

            q, kv, glu = _fused_proj(x, mrow, norm_mix_w[l], mods, 0, [wq, wkv, wglu], [BF16, F32, BF16])
            qn, k_att, vt_att, k_n, v_n = _qk_prep(q, kv, meta, od_q_norm_w[i], od_k_norm_w[i], cos_t, sin_t)
            ctx_k.append(k_n[n_lat_tok:].reshape(bp, sp, KV_HEADS, HEAD_DIM))
            ctx_v.append(v_n[n_lat_tok:].reshape(bp, sp, KV_HEADS, HEAD_DIM))
            ck = cache_k[:, i].reshape(bs_all * past, kvw).astype(BF16)
            cvt = cache_v[:, i].reshape(bs_all, past, kvw).swapaxes(1, 2).astype(BF16)
            per = ATT_TK // sp
            ctx_src = (k_att, pl.BlockSpec((sp, kvw), lambda s, j: (ctx_blk0 + s, 0)),
                       vt_att, pl.BlockSpec((None, kvw, sp),
                                            lambda s, j: ((ctx_blk0 + s) // per, 0, (ctx_blk0 + s) % per)), False)
            cache_src = (ck, pl.BlockSpec((past, kvw), lambda s, j: (lat0 + s, 0)),
                         cvt, pl.BlockSpec((None, kvw, past), lambda s, j: (lat0 + s, 0, 0)), False)
            lat_src = (k_att, pl.BlockSpec((ss, kvw), lambda s, j: (s, 0)),
                       vt_att, pl.BlockSpec((ss // ATT_TK, kvw, ATT_TK), lambda s, j: (s, 0, 0)), True)
            att_lat = _attn_call(qn, [cache_src, lat_src], 0, bs, ss, "attention_latent")
            att_ctx = _attn_call(qn, [ctx_src], n_lat_tok // SEG, bp, sp, "attention_ctx")
            u = _conformer_conv(glu, meta, od_dw_w[i], od_dw_b[i], od_ln_w[i], od_ln_b[i])
            x = _out_proj(att_lat, att_ctx, u, w_out_p, x, mrow, mods, 2)

        h2, ti, gt, rk, cnt = _ffn_pre(x, mrow, norm_ffn_w[l], mods, router_w[l], router_b[l])
        x = _moe(x, h2, ti, gt, rk, cnt[0, :N_EXPERTS], mrow, mods, l, w_gu, b_gu, w_dn, b_dn)

    y = _final_norm(x, final_norm_w)
    y_sample = y[:n_lat_tok].reshape(bs, ss, d)
    y_prompt = y[n_lat_tok:].reshape(bp, sp, d)
    return (y_prompt, y_sample, jnp.stack(states, axis=1), jnp.stack(ctx_k, axis=1), jnp.stack(ctx_v, axis=1))


N_STREAMS = 1


def kernel(x_prompt, x_sample, state_ssd, cache_k, cache_v, c, c_ctx, w_mod, b_mod, norm_mix_w, norm_ffn_w, ev_w_in, ev_conv_w, ev_conv_b, ev_dt_bias, ev_a_log, ev_d_skip, ev_norm_w, ev_sconv_w, ev_w_out, od_w_in, od_q_norm_w, od_k_norm_w, od_dw_w, od_dw_b, od_ln_w, od_ln_b, od_w_out, router_w, router_b, w_gu, b_gu, w_dn, b_dn, final_norm_w):
    bp, _, d = x_prompt.shape
    bs, ss, _ = x_sample.shape
    n_rows = 16
    assert 1 + bs <= n_rows
    cvec = jnp.zeros((n_rows, d), F32).at[0].set(c_ctx).at[1:1 + bs].set(c)
    mods_all = _modulation(cvec, w_mod, b_mod)
    cos_t, sin_t = _rope_tables(ss)
    n_streams = N_STREAMS if (bp % N_STREAMS == 0 and bs % N_STREAMS == 0) else 1
    outs = [_run_stream(s * (bs // n_streams), bs // n_streams, s * (bp // n_streams), bp // n_streams,
                        mods_all, cos_t, sin_t, x_prompt, x_sample, state_ssd, cache_k, cache_v,
                        norm_mix_w, norm_ffn_w, ev_w_in, ev_conv_w, ev_conv_b, ev_dt_bias, ev_a_log, ev_d_skip,
                        ev_norm_w, ev_sconv_w, ev_w_out, od_w_in, od_q_norm_w, od_k_norm_w, od_dw_w, od_dw_b,
                        od_ln_w, od_ln_b, od_w_out, router_w, router_b, w_gu, b_gu, w_dn, b_dn, final_norm_w)
            for s in range(n_streams)]
    return tuple(jnp.concatenate(parts, axis=0) for parts in zip(*outs))
```

```python
import functools

import numpy as np
import jax
import jax.numpy as jnp
from jax import lax
from jax.experimental import pallas as pl
from jax.experimental.pallas import tpu as pltpu

F32 = jnp.float32
BF16 = jnp.bfloat16
HIGHEST = lax.Precision.HIGHEST

NORM_EPS = 1e-6
GRID_W = 64
ROPE_THETA = 10000.0
HEAD_DIM = 64
KV_HEADS = 4
Q_PER_KV = 4
SSD_HEAD_DIM = 64
SSD_GROUPS = 4
SSD_HEADS_PER_GROUP = 4
SSD_STATE = 128
SSD_CHUNK = 128
SSD_CONV = 4
SCONV_K = 3
CONF_K = 31
N_EXPERTS = 32
TOP_K = 4
SWIGLU_LIMIT = 7.0
SWIGLU_ALPHA = 1.702

SEG = 256
HALO = 16
LANES = 128
SUBLANES = 8
MOE_ROWS = 256
ATT_TK = 1024
ATT_STRIP = 512
NEG_BIG = -1e30
LOG2_E = 1.4426950408889634
VMEM_LIMIT = 56 * 1024 * 1024

NT_DIMS = (((1,), (1,)), ((), ()))
TN_DIMS = (((0,), (0,)), ((), ()))


def _cparams(n_grid):
    return pltpu.CompilerParams(dimension_semantics=("arbitrary",) * n_grid,
                                vmem_limit_bytes=VMEM_LIMIT)


def _resident(shape):
    nd = len(shape)
    return pl.BlockSpec(shape, lambda *_: (0,) * nd, pipeline_mode=pl.Buffered(1))


def _silu(x):
    return x * jax.nn.sigmoid(x)


def _mod_body(c_ref, w_ref, b_ref, o_ref):
    c = c_ref[...]
    s = _silu(c).astype(BF16)
    o_ref[0] = jnp.dot(s, w_ref[0].astype(BF16), preferred_element_type=F32) + b_ref[0]


def _modulation(cvec, w_mod, b_mod):
    depth, d, n = w_mod.shape
    rows = cvec.shape[0]
    tn = 1536
    assert n % tn == 0
    return pl.pallas_call(
        _mod_body,
        grid=(depth, n // tn),
        in_specs=[pl.BlockSpec((rows, d), lambda l, j: (0, 0)),
                  pl.BlockSpec((1, d, tn), lambda l, j: (l, 0, j)),
                  pl.BlockSpec((1, 1, tn), lambda l, j: (l, 0, j))],
        out_specs=pl.BlockSpec((1, rows, tn), lambda l, j: (l, 0, j)),
        out_shape=jax.ShapeDtypeStruct((depth, rows, n), F32),
        compiler_params=_cparams(2),
        name="modulation",
    )(cvec, w_mod, b_mod.reshape(depth, 1, n))


def _normed(x, nw_ref, sh_ref, sc_ref):
    ms = jnp.mean(x * x, axis=-1, keepdims=True)
    h = x * lax.rsqrt(ms + NORM_EPS) * nw_ref[...]
    return h * (1.0 + sc_ref[0]) + sh_ref[0]


def _proj_body(mrow_ref, x_ref, nw_ref, sh_ref, sc_ref, *rest, n_out, col_chunk):
    del mrow_ref
    w_refs, o_refs = rest[:n_out], rest[n_out:]
    hb = _normed(x_ref[...], nw_ref, sh_ref, sc_ref).astype(BF16)
    for w_ref, o_ref in zip(w_refs, o_refs):
        n = w_ref.shape[1]
        for c0 in range(0, n, col_chunk):
            c1 = min(n, c0 + col_chunk)
            o_ref[:, c0:c1] = jnp.dot(hb, w_ref[:, c0:c1],
                                      preferred_element_type=F32).astype(o_ref.dtype)


def _mod_spec(d, chunk):
    return pl.BlockSpec((1, 1, d), lambda i, mrow: (mrow[i], 0, chunk))


def _fused_proj(x, mrow, norm_w, mods, shift_chunk, weights, out_dtypes):
    t, d = x.shape
    nseg = t // SEG
    n_out = len(weights)
    in_specs = [pl.BlockSpec((SEG, d), lambda i, mrow: (i, 0)),
                pl.BlockSpec((1, d), lambda i, mrow: (0, 0)),
                _mod_spec(d, shift_chunk), _mod_spec(d, shift_chunk + 1)]
    in_specs += [_resident(w.shape) for w in weights]
    out_specs = [pl.BlockSpec((SEG, w.shape[1]), lambda i, mrow: (i, 0)) for w in weights]
    out_shape = [jax.ShapeDtypeStruct((t, w.shape[1]), dt) for w, dt in zip(weights, out_dtypes)]
    return pl.pallas_call(
        functools.partial(_proj_body, n_out=n_out, col_chunk=512),
        grid_spec=pltpu.PrefetchScalarGridSpec(
            num_scalar_prefetch=1, grid=(nseg,), in_specs=in_specs, out_specs=out_specs),
        out_shape=out_shape,
        compiler_params=_cparams(1),
        name="norm_mod_proj",
    )(mrow, x, norm_w.reshape(1, d), mods, mods, *weights)


def _fill(scr, prev, cur, nxt, keep_prev, keep_next):
    scr[0:HALO, :] = prev * keep_prev
    scr[HALO:HALO + SEG, :] = cur
    scr[HALO + SEG:HALO + SEG + HALO, :] = nxt * keep_next


def _taps(scr, w_ref, c0, k_taps, left):
    acc = None
    for k in range(k_taps):
        term = scr[pl.ds(HALO - left + k, SEG), c0:c0 + LANES] * w_ref[k:k + 1, c0:c0 + LANES]
        acc = term if acc is None else acc + term
    return acc


def _keep(first_ref, last_ref):
    i = pl.program_id(0)
    return (1 - first_ref[i]).astype(F32), (1 - last_ref[i]).astype(F32)


def _ssdconv_body(first_ref, last_ref, p_ref, c_ref, n_ref, w_ref, b_ref, o_ref, scr):
    kp, kn = _keep(first_ref, last_ref)
    _fill(scr, p_ref[...].astype(F32), c_ref[...].astype(F32), n_ref[...].astype(F32), kp, kn)
    for c0 in range(0, o_ref.shape[1], LANES):
        y = _taps(scr, w_ref, c0, SSD_CONV, SSD_CONV // 2) + b_ref[:, c0:c0 + LANES]
        o_ref[:, c0:c0 + LANES] = _silu(y).astype(o_ref.dtype)


def _sconv_body(first_ref, last_ref, gp_ref, gc_ref, gn_ref, xp_ref, xc_ref, xn_ref, gb_ref,
                w_ref, o_ref, scr):
    kp, kn = _keep(first_ref, last_ref)
    f = lambda a, b: a[...].astype(F32) * b[...].astype(F32)
    _fill(scr, f(gp_ref, xp_ref), f(gc_ref, xc_ref), f(gn_ref, xn_ref), kp, kn)
    for c0 in range(0, o_ref.shape[1], LANES):
        y = _taps(scr, w_ref, c0, SCONV_K, SCONV_K // 2)
        o_ref[:, c0:c0 + LANES] = (gb_ref[:, c0:c0 + LANES].astype(F32) * y).astype(o_ref.dtype)


def _conf_body(first_ref, last_ref, ap_ref, ac_ref, an_ref, gp_ref, gc_ref, gn_ref,
               w_ref, b_ref, lnw_ref, lnb_ref, o_ref, scr, u_scr, sh_scr):
    kp, kn = _keep(first_ref, last_ref)
    f = lambda a, g: a[...].astype(F32) * jax.nn.sigmoid(g[...].astype(F32))
    _fill(scr, f(ap_ref, gp_ref), f(ac_ref, gc_ref), f(an_ref, gn_ref), kp, kn)
    span = SEG + 2 * HALO - SUBLANES
    for r in range(SUBLANES):
        sh_scr[r] = scr[pl.ds(r, span), :]
    left = CONF_K // 2
    rows = SEG // 4
    for c0 in range(0, o_ref.shape[1], LANES):
        for r0 in range(0, SEG, rows):
            acc = None
            for k in range(CONF_K):
                off = HALO - left + k
                term = (sh_scr[off % SUBLANES, pl.ds(r0 + off - off % SUBLANES, rows), c0:c0 + LANES]
                        * w_ref[k:k + 1, c0:c0 + LANES])
                acc = term if acc is None else acc + term
            u_scr[r0:r0 + rows, c0:c0 + LANES] = acc + b_ref[:, c0:c0 + LANES]
    u = u_scr[...]
    mu = jnp.mean(u, axis=-1, keepdims=True)
    uc = u - mu
    var = jnp.mean(uc * uc, axis=-1, keepdims=True)
    y = uc * lax.rsqrt(var + NORM_EPS) * lnw_ref[...] + lnb_ref[...]
    o_ref[...] = _silu(y).astype(o_ref.dtype)


def _halo_specs(t, cw, col_off):
    per = SEG // HALO
    last_blk = t // HALO - 1
    return [
        pl.BlockSpec((HALO, cw), lambda i, j, f, l: (jnp.maximum(i * per - 1, 0), col_off + j)),
        pl.BlockSpec((SEG, cw), lambda i, j, f, l: (i, col_off + j)),
        pl.BlockSpec((HALO, cw), lambda i, j, f, l: (jnp.minimum((i + 1) * per, last_blk), col_off + j)),
    ]


def _conv_call(body, name, meta, t, cw, n_col, in_specs, args, out_cols, scratch):
    nseg = t // SEG
    return pl.pallas_call(
        body,
        grid_spec=pltpu.PrefetchScalarGridSpec(
            num_scalar_prefetch=2, grid=(nseg, n_col), in_specs=in_specs,
            out_specs=pl.BlockSpec((SEG, cw), lambda i, j, f, l: (i, j)),
            scratch_shapes=scratch),
        out_shape=jax.ShapeDtypeStruct((t, out_cols), BF16),
        compiler_params=_cparams(2),
        name=name,
    )(meta["first"], meta["last"], *args)


def _colvec_spec(rows, cw):
    return pl.BlockSpec((rows, cw), lambda i, j, f, l: (0, j))


def _ssd_conv(main, meta, conv_w, conv_b, col0, width):
    t = main.shape[0]
    cw = 512
    in_specs = _halo_specs(t, cw, col0 // cw) + [_colvec_spec(SSD_CONV, cw), _colvec_spec(1, cw)]
    return _conv_call(_ssdconv_body, "ssd_conv_silu", meta, t, cw, width // cw, in_specs,
                      (main, main, main, conv_w, conv_b.reshape(1, width)), width,
                      [pltpu.VMEM((SEG + 2 * HALO, cw), F32)])


def _short_conv(main, meta, sconv_w, col_gb, col_gc, col_xc, width):
    t = main.shape[0]
    cw = 512
    in_specs = (_halo_specs(t, cw, col_gc // cw) + _halo_specs(t, cw, col_xc // cw)
                + [pl.BlockSpec((SEG, cw), lambda i, j, f, l: (i, col_gb // cw + j)),
                   _colvec_spec(SCONV_K, cw)])
    return _conv_call(_sconv_body, "short_gated_conv", meta, t, cw, width // cw, in_specs,
                      (main,) * 7 + (sconv_w,), width, [pltpu.VMEM((SEG + 2 * HALO, cw), F32)])


def _conformer_conv(glu, meta, dw_w, dw_b, ln_w, ln_b):
    t, two_w = glu.shape
    w = two_w // 2
    in_specs = (_halo_specs(t, w, 0) + _halo_specs(t, w, 1)
                + [_colvec_spec(CONF_K, w)] + [_colvec_spec(1, w)] * 3)
    return _conv_call(_conf_body, "conformer_conv", meta, t, w, 1, in_specs,
                      (glu,) * 6 + (dw_w, dw_b.reshape(1, w), ln_w.reshape(1, w), ln_b.reshape(1, w)),
                      w, [pltpu.VMEM((SEG + 2 * HALO, w), F32), pltpu.VMEM((SEG, w), F32),
                       pltpu.VMEM((SUBLANES, SEG + 2 * HALO - SUBLANES, w), F32)])


def _softplus(x):
    return jnp.maximum(x, 0.0) + jnp.log1p(jnp.exp(-jnp.abs(x)))


def _ssd_body(*refs, nc, has_h0, want_state):
    it = iter(refs)
    x_ref, b_ref, c_ref, dt_ref, bias_ref, alog_ref, dsk_ref = (next(it) for _ in range(7))
    h0_ref = next(it) if has_h0 else None
    y_ref = next(it)
    st_ref = next(it) if want_state else None
    ybuf, sf_scr, sb_scr, decf_scr, decb_scr, xsf_scr, xsb_scr, tot_scr = (next(it) for _ in range(8))

    q = SSD_CHUNK
    hp = SSD_HEADS_PER_GROUP
    width = hp * SSD_HEAD_DIM
    row = lax.broadcasted_iota(jnp.int32, (q, q), 0)
    col = lax.broadcasted_iota(jnp.int32, (q, q), 1)
    lower = row >= col
    upper = col >= row
    lower_f = lower.astype(F32)
    lane_w = lax.broadcasted_iota(jnp.int32, (q, width), 1) // SSD_HEAD_DIM
    sub_w = lax.broadcasted_iota(jnp.int32, (width, 1), 0) // SSD_HEAD_DIM

    def per_head_cols(mat, base):
        out = mat[:, base + hp - 1:base + hp]
        for r in range(hp - 2, -1, -1):
            out = jnp.where(lane_w == r, mat[:, base + r:base + r + 1], out)
        return out

    def per_head_rows(rowvec, base):
        out = rowvec[:, base + hp - 1:base + hp]
        for r in range(hp - 2, -1, -1):
            out = jnp.where(sub_w == r, rowvec[:, base + r:base + r + 1], out)
        return out

    a_row = -jnp.exp(alog_ref[0])
    bias_row = bias_ref[0]

    dsk = dsk_ref[...]

    def intra(c, carry):
        rows = pl.ds(pl.multiple_of(c * q, q), q)
        dt = _softplus(dt_ref[rows, :] + bias_row)
        a = dt * a_row
        cs = jnp.dot(lower_f, a, precision=HIGHEST, preferred_element_type=F32)
        ex = cs - a
        tot = cs[q - 1:q, :]
        xb = x_ref[rows, :]
        x = xb.astype(F32)
        cb = lax.dot_general(c_ref[rows, :], b_ref[rows, :], NT_DIMS, preferred_element_type=F32)
        cs_t, ex_t, dt_t = cs.T, ex.T, dt.T
        y = x * dsk
        for r in range(hp):
            dec_f = jnp.exp(jnp.where(lower, cs[:, r:r + 1] - cs_t[r:r + 1, :], NEG_BIG))
            dec_b = jnp.exp(jnp.where(upper, ex_t[hp + r:hp + r + 1, :] - ex[:, hp + r:hp + r + 1], NEG_BIG))
            wm = cb * (dec_f * dt_t[r:r + 1, :] + dec_b * dt_t[hp + r:hp + r + 1, :])
            yd = jnp.dot(wm.astype(BF16), xb, preferred_element_type=F32)
            y = y + jnp.where(lane_w == r, yd, 0.0)
        ybuf[rows, :] = y
        decf_scr[rows, :] = per_head_cols(jnp.exp(cs), 0)
        decb_scr[rows, :] = per_head_cols(jnp.exp(tot - ex), hp)
        xsf_scr[rows, :] = (x * per_head_cols(jnp.exp(tot - cs) * dt, 0)).astype(BF16)
        xsb_scr[rows, :] = (x * per_head_cols(jnp.exp(ex) * dt, hp)).astype(BF16)
        tot_scr[pl.ds(c, 1), :] = jnp.exp(tot)
        return carry

    lax.fori_loop(0, nc, intra, 0, unroll=2)

    if has_h0:
        sf_scr[...] = h0_ref[0].reshape(width, SSD_STATE)
        sb_scr[...] = h0_ref[1].reshape(width, SSD_STATE)
    else:
        sf_scr[...] = jnp.zeros_like(sf_scr)
        sb_scr[...] = jnp.zeros_like(sb_scr)

    def one_direction(c, s_scr, dec_scr, xs_scr, base):
        rows = pl.ds(pl.multiple_of(c * q, q), q)
        s_prev = s_scr[...]
        y_off = lax.dot_general(c_ref[rows, :], s_prev.astype(BF16), NT_DIMS, preferred_element_type=F32)
        ybuf[rows, :] = ybuf[rows, :] + y_off * dec_scr[rows, :]
        contrib = lax.dot_general(xs_scr[rows, :], b_ref[rows, :], TN_DIMS, preferred_element_type=F32)
        s_scr[...] = s_prev * per_head_rows(tot_scr[pl.ds(c, 1), :], base) + contrib

    def recur(i, carry):
        one_direction(i, sf_scr, decf_scr, xsf_scr, 0)
        one_direction(nc - 1 - i, sb_scr, decb_scr, xsb_scr, hp)
        return carry

    lax.fori_loop(0, nc, recur, 0, unroll=2)

    y_ref[...] = ybuf[...].astype(y_ref.dtype)
    if want_state:
        st_ref[0] = sf_scr[...].reshape(hp, SSD_HEAD_DIM, SSD_STATE)
        st_ref[1] = sb_scr[...].reshape(hp, SSD_HEAD_DIM, SSD_STATE)


def _ssd_call(xbc, dtp, bias, alog, dsk, seq_len, n_seq, row_blk0, h0, layer_idx, seq0):
    inner = SSD_GROUPS * SSD_HEADS_PER_GROUP * SSD_HEAD_DIM
    width = SSD_HEADS_PER_GROUP * SSD_HEAD_DIM
    nb = inner // SSD_STATE
    nc = seq_len // SSD_CHUNK
    want_state = h0 is None
    in_specs = [
        pl.BlockSpec((seq_len, width), lambda s, g: (row_blk0 + s, g)),
        pl.BlockSpec((seq_len, SSD_STATE), lambda s, g: (row_blk0 + s, nb + g)),
        pl.BlockSpec((seq_len, SSD_STATE), lambda s, g: (row_blk0 + s, nb + SSD_GROUPS + g)),
        pl.BlockSpec((seq_len, LANES), lambda s, g: (row_blk0 + s, g)),
        pl.BlockSpec((1, 1, LANES), lambda s, g: (g, 0, 0)),
        pl.BlockSpec((1, 1, LANES), lambda s, g: (g, 0, 0)),
        pl.BlockSpec((1, width), lambda s, g: (0, g)),
    ]
    args = [xbc, xbc, xbc, dtp, bias, alog, dsk]
    if not want_state:
        in_specs.append(pl.BlockSpec((None, None, 2, SSD_HEADS_PER_GROUP, SSD_HEAD_DIM, SSD_STATE),
                                     lambda s, g: (seq0 + s, layer_idx, 0, g, 0, 0)))
        args.append(h0)
    out_specs = [pl.BlockSpec((seq_len, width), lambda s, g: (s, g))]
    out_shape = [jax.ShapeDtypeStruct((n_seq * seq_len, inner), BF16)]
    if want_state:
        out_specs.append(pl.BlockSpec((None, 2, SSD_HEADS_PER_GROUP, SSD_HEAD_DIM, SSD_STATE),
                                      lambda s, g: (s, 0, g, 0, 0)))
        out_shape.append(jax.ShapeDtypeStruct(
            (n_seq, 2, SSD_GROUPS * SSD_HEADS_PER_GROUP, SSD_HEAD_DIM, SSD_STATE), F32))
    return pl.pallas_call(
        functools.partial(_ssd_body, nc=nc, has_h0=not want_state, want_state=want_state),
        grid=(n_seq, SSD_GROUPS),
        in_specs=in_specs, out_specs=out_specs, out_shape=out_shape,
        scratch_shapes=[pltpu.VMEM((seq_len, width), F32),
                        pltpu.VMEM((width, SSD_STATE), F32), pltpu.VMEM((width, SSD_STATE), F32),
                        pltpu.VMEM((seq_len, width), F32), pltpu.VMEM((seq_len, width), F32),
                        pltpu.VMEM((seq_len, width), BF16), pltpu.VMEM((seq_len, width), BF16),
                        pltpu.VMEM((max(nc, SUBLANES), LANES), F32)],
        compiler_params=_cparams(2),
        name="ssd_scan_ctx" if want_state else "ssd_scan_latent",
    )(*args)


def _outproj_body(mrow_ref, al_ref, ac_ref, *refs, gated_norm):
    a = jnp.where(mrow_ref[pl.program_id(0)] > 0, al_ref[...], ac_ref[...])
    if gated_norm:
        z_ref, nw_ref, b_ref, w_ref, x_ref, g_ref, o_ref = refs
        y = a.astype(F32) * _silu(z_ref[...].astype(F32))
        ms = jnp.mean(y * y, axis=-1, keepdims=True)
        a = (y * lax.rsqrt(ms + NORM_EPS) * nw_ref[...]).astype(BF16)
    else:
        b_ref, w_ref, x_ref, g_ref, o_ref = refs
    d = a.shape[1]
    out = jnp.dot(a, w_ref[0:d, :], preferred_element_type=F32)
    out = out + jnp.dot(b_ref[...], w_ref[d:, :], preferred_element_type=F32)
    o_ref[...] = x_ref[...] + g_ref[0] * out


def _out_proj(a_lat, a_ctx, b, w, x, mrow, mods, gate_chunk, z_src=None, norm_w=None):
    t, d = x.shape
    nseg = t // SEG
    n_lat_seg = a_lat.shape[0] // SEG
    row = lambda i, mrow: (i, 0)
    in_specs = [pl.BlockSpec((SEG, d), lambda i, mrow: (jnp.minimum(i, n_lat_seg - 1), 0)),
                pl.BlockSpec((SEG, d), lambda i, mrow: (jnp.maximum(i - n_lat_seg, 0), 0))]
    args = [a_lat, a_ctx]
    if z_src is not None:
        in_specs += [pl.BlockSpec((SEG, d), row), pl.BlockSpec((1, d), lambda i, mrow: (0, 0))]
        args += [z_src, norm_w.reshape(1, d)]
    in_specs += [pl.BlockSpec((SEG, d), row), _resident(w.shape), pl.BlockSpec((SEG, d), row),
                 _mod_spec(d, gate_chunk)]
    args += [b, w, x, mods]
    return pl.pallas_call(
        functools.partial(_outproj_body, gated_norm=z_src is not None),
        grid_spec=pltpu.PrefetchScalarGridSpec(
            num_scalar_prefetch=1, grid=(nseg,), in_specs=in_specs,
            out_specs=pl.BlockSpec((SEG, d), row)),
        out_shape=jax.ShapeDtypeStruct((t, d), F32),
        compiler_params=_cparams(1),
        name="mixer_out_proj",
    )(mrow, *args)


def _qkprep_body(lat_ref, pos_ref, q_ref, kv_ref, qw_ref, kw_ref, cos_ref, sin_ref,
                 qo_ref, ko_ref, vo_ref, kn_ref, vn_ref):
    del pos_ref
    is_lat = lat_ref[pl.program_id(0)] > 0
    r = lax.broadcasted_iota(jnp.int32, (LANES, LANES), 0) // HEAD_DIM
    c = lax.broadcasted_iota(jnp.int32, (LANES, LANES), 1) // HEAD_DIM
    head_mean = jnp.where(r == c, 1.0 / HEAD_DIM, 0.0).astype(F32)
    lane = lax.broadcasted_iota(jnp.int32, (SEG, LANES), 1)
    first_half = (lane % HEAD_DIM) < (HEAD_DIM // 2)
    cos = cos_ref[...]
    sin = sin_ref[...]

    def norm_rope(xs, w):
        ms = jnp.dot(xs * xs, head_mean, precision=HIGHEST, preferred_element_type=F32)
        xn = xs * lax.rsqrt(ms + NORM_EPS) * w
        rot = jnp.where(first_half, pltpu.roll(xn, LANES - HEAD_DIM // 2, 1),
                        pltpu.roll(xn, HEAD_DIM // 2, 1))
        return xn, jnp.where(is_lat, xn * cos + rot * sin, xn)

    scale = HEAD_DIM ** -0.5 * LOG2_E
    for c0 in range(0, q_ref.shape[1], LANES):
        _, qr = norm_rope(q_ref[:, c0:c0 + LANES].astype(F32), qw_ref[...])
        qo_ref[:, c0:c0 + LANES] = (qr * scale).astype(qo_ref.dtype)
    kvw = ko_ref.shape[1]
    for c0 in range(0, kvw, LANES):
        kn, kr = norm_rope(kv_ref[:, c0:c0 + LANES], kw_ref[...])
        kn_ref[:, c0:c0 + LANES] = kn
        ko_ref[:, c0:c0 + LANES] = kr.astype(ko_ref.dtype)
    v = kv_ref[:, kvw:]
    vn_ref[...] = v
    vo_ref[...] = v.T.astype(vo_ref.dtype)


def _qk_prep(q, kv, meta, q_norm_w, k_norm_w, cos_t, sin_t):
    t, qd = q.shape
    kvd = kv.shape[1] // 2
    nseg = t // SEG
    row = lambda i, lat, pos: (i, 0)
    tab = lambda i, lat, pos: (pos[i] * lat[i], 0)
    tile2 = lambda w: jnp.tile(w, LANES // HEAD_DIM).reshape(1, LANES)
    per_chunk = ATT_TK // SEG
    return pl.pallas_call(
        _qkprep_body,
        grid_spec=pltpu.PrefetchScalarGridSpec(
            num_scalar_prefetch=2, grid=(nseg,),
            in_specs=[pl.BlockSpec((SEG, qd), row), pl.BlockSpec((SEG, 2 * kvd), row),
                      pl.BlockSpec((1, LANES), lambda i, lat, pos: (0, 0)),
                      pl.BlockSpec((1, LANES), lambda i, lat, pos: (0, 0)),
                      pl.BlockSpec((SEG, LANES), tab), pl.BlockSpec((SEG, LANES), tab)],
            out_specs=[pl.BlockSpec((SEG, qd), row), pl.BlockSpec((SEG, kvd), row),
                       pl.BlockSpec((None, kvd, SEG), lambda i, lat, pos: (i // per_chunk, 0, i % per_chunk)),
                       pl.BlockSpec((SEG, kvd), row), pl.BlockSpec((SEG, kvd), row)]),
        out_shape=[jax.ShapeDtypeStruct((t, qd), BF16), jax.ShapeDtypeStruct((t, kvd), BF16),
                   jax.ShapeDtypeStruct((t // ATT_TK, kvd, ATT_TK), BF16),
                   jax.ShapeDtypeStruct((t, kvd), F32), jax.ShapeDtypeStruct((t, kvd), F32)],
        compiler_params=_cparams(1),
        name="qk_norm_rope",
    )(meta["lat"], meta["pos"], q, kv, tile2(q_norm_w), tile2(k_norm_w), cos_t, sin_t)


def _rope_tables(seq_len):
    rows = seq_len // GRID_W
    rowp = jnp.repeat(jnp.arange(rows), GRID_W).astype(F32)
    colp = jnp.tile(jnp.arange(GRID_W), rows).astype(F32)
    axis = HEAD_DIM // 2
    inv = ROPE_THETA ** (-jnp.arange(0, axis, 2, dtype=F32) / axis)
    ang = jnp.concatenate([rowp[:, None] * inv, colp[:, None] * inv], axis=-1)
    cos, sin = jnp.cos(ang), jnp.sin(ang)
    cos_h = jnp.concatenate([cos, cos], axis=-1)
    sin_h = jnp.concatenate([-sin, sin], axis=-1)
    rep = LANES // HEAD_DIM
    return jnp.tile(cos_h, (1, rep)), jnp.tile(sin_h, (1, rep))


ONES_ROWS = 16


def _attn_body(*refs, chunked, tq):
    n_src = len(chunked)
    q_ref = refs[0]
    kv_refs = refs[1:1 + 2 * n_src]
    pos = 1 + 2 * n_src
    o_ref = refs[pos]
    qs_scr, acc_scr, m_scr = refs[pos + 1:]
    kvw = KV_HEADS * HEAD_DIM
    lane_g = lax.broadcasted_iota(jnp.int32, (tq, kvw), 1) // HEAD_DIM

    for g in range(KV_HEADS):
        for r in range(Q_PER_KV):
            qr = q_ref[:, kvw * r:kvw * (r + 1)]
            qs_scr[g, r * tq:(r + 1) * tq, :] = jnp.where(lane_g == g, qr, jnp.zeros_like(qr))
    m_scr[...] = jnp.full_like(m_scr, NEG_BIG)
    acc_scr[...] = jnp.zeros_like(acc_scr)

    units = [(g, c) for g in range(KV_HEADS) for c in range(Q_PER_KV * tq // ATT_STRIP)]

    def step(kc, vt):
        ones = jnp.ones((ONES_ROWS, kc.shape[0]), BF16)

        def scores(u):
            g, c = u
            return lax.dot_general(kc, qs_scr[g, c * ATT_STRIP:(c + 1) * ATT_STRIP, :], NT_DIMS,
                                   preferred_element_type=F32)

        def softmax(u, s):
            g, c = u
            cols = slice(c * ATT_STRIP, (c + 1) * ATT_STRIP)
            m_prev = m_scr[g, :, cols]
            m_new = jnp.maximum(m_prev, jnp.max(s, axis=0, keepdims=True))
            m_scr[g, :, cols] = m_new
            return jnp.exp2(s - m_new).astype(BF16), jnp.exp2(m_prev - m_new)

        def accumulate(u, p, alpha):
            g, c = u
            cols = slice(c * ATT_STRIP, (c + 1) * ATT_STRIP)
            lhs = jnp.concatenate([vt[HEAD_DIM * g:HEAD_DIM * (g + 1), :], ones], axis=0)
            pv = jnp.dot(lhs, p, preferred_element_type=F32)
            acc_scr[g, :, cols] = acc_scr[g, :, cols] * alpha + pv

        s_cur = scores(units[0])
        pending = None
        for i, u in enumerate(units):
            s_next = scores(units[i + 1]) if i + 1 < len(units) else None
            if pending is not None:
                accumulate(*pending)
            pending = (u,) + softmax(u, s_cur)
            s_cur = s_next
        accumulate(*pending)

    for si, is_chunked in enumerate(chunked):
        k_ref, vt_ref = kv_refs[2 * si], kv_refs[2 * si + 1]
        if not is_chunked:
            step(k_ref[...], vt_ref[...])
        else:
            tk = vt_ref.shape[2]

            def body(j, carry, k_ref=k_ref, vt_ref=vt_ref, tk=tk):
                rows = pl.ds(pl.multiple_of(j * tk, tk), tk)
                step(k_ref[rows, :], vt_ref[j])
                return carry
            lax.fori_loop(0, vt_ref.shape[0], body, 0)

    for r in range(Q_PER_KV):
        cols = slice(r * tq, (r + 1) * tq)
        parts = [acc_scr[g, 0:HEAD_DIM, cols] / acc_scr[g, HEAD_DIM:HEAD_DIM + 1, cols]
                 for g in range(KV_HEADS)]
        o_ref[:, kvw * r:kvw * (r + 1)] = jnp.concatenate(parts, axis=0).T.astype(o_ref.dtype)


def _attn_call(q, srcs, q_blk0, n_seq, q_len, name):
    qd = q.shape[1]
    tq = SEG
    kvw = KV_HEADS * HEAD_DIM
    nq = q_len // tq
    in_specs = [pl.BlockSpec((tq, qd), lambda s, j: (q_blk0 + s * nq + j, 0))]
    args = [q]
    for k, k_spec, vt, vt_spec, _ in srcs:
        in_specs += [k_spec, vt_spec]
        args += [k, vt]
    rows4 = Q_PER_KV * tq
    return pl.pallas_call(
        functools.partial(_attn_body, chunked=tuple(s[4] for s in srcs), tq=tq),
        grid=(n_seq, nq),
        in_specs=in_specs,
        out_specs=pl.BlockSpec((tq, qd), lambda s, j: (s * nq + j, 0)),
        out_shape=jax.ShapeDtypeStruct((n_seq * q_len, qd), BF16),
        scratch_shapes=[pltpu.VMEM((KV_HEADS, rows4, kvw), BF16),
                        pltpu.VMEM((KV_HEADS, HEAD_DIM + ONES_ROWS, rows4), F32),
                        pltpu.VMEM((KV_HEADS, 1, rows4), F32)],
        compiler_params=_cparams(2),
        name=name,
    )(*args)


def _ffnpre_body(mrow_ref, x_ref, nw_ref, sh_ref, sc_ref, rw_ref, rb_ref,
                 h_ref, ti_ref, gt_ref, rk_ref, cnt_ref, cnt_scr):
    del mrow_ref

    @pl.when(pl.program_id(0) == 0)
    def _():
        cnt_scr[...] = jnp.zeros_like(cnt_scr)

    h = _normed(x_ref[...], nw_ref, sh_ref, sc_ref)
    hb = h.astype(BF16)
    h_ref[...] = hb
    logits = jnp.dot(hb, rw_ref[...], preferred_element_type=F32) + rb_ref[...]
    lane = lax.broadcasted_iota(jnp.int32, logits.shape, 1)
    work = logits
    vals, idxs = [], []
    for _ in range(TOP_K):
        m = jnp.max(work, axis=1, keepdims=True)
        idx = jnp.min(jnp.where(work == m, lane, LANES), axis=1, keepdims=True)
        vals.append(m)
        idxs.append(idx)
        work = jnp.where(lane == idx, NEG_BIG, work)
    es = [jnp.exp(v - vals[0]) for v in vals]
    den = es[0]
    for e in es[1:]:
        den = den + e
    ti = jnp.zeros(logits.shape, jnp.int32)
    gt = jnp.zeros(logits.shape, F32)
    for k in range(TOP_K):
        ti = jnp.where(lane == k, idxs[k], ti)
        gt = jnp.where(lane == k, es[k] / den, gt)
    ti_ref[...] = ti
    gt_ref[...] = gt

    onehot = jnp.zeros(logits.shape, F32)
    for k in range(TOP_K):
        onehot = onehot + (lane == idxs[k]).astype(F32)
    n = logits.shape[0]
    earlier = (lax.broadcasted_iota(jnp.int32, (n, n), 0) > lax.broadcasted_iota(jnp.int32, (n, n), 1))
    before = jnp.dot(earlier.astype(BF16), onehot.astype(BF16), preferred_element_type=F32) + cnt_scr[...]
    rk = jnp.zeros(logits.shape, jnp.int32)
    for k in range(TOP_K):
        r_k = jnp.sum(jnp.where(lane == idxs[k], before, 0.0), axis=1, keepdims=True)
        rk = jnp.where(lane == k, r_k.astype(jnp.int32), rk)
    rk_ref[...] = rk
    cnt_scr[...] = cnt_scr[...] + jnp.sum(onehot, axis=0, keepdims=True)
    cnt_ref[...] = cnt_scr[...]


def _ffn_pre(x, mrow, norm_w, mods, router_w, router_b):
    t, d = x.shape
    nseg = t // SEG
    ne = router_w.shape[1]
    rw = jnp.zeros((d, LANES), F32).at[:, :ne].set(router_w).astype(BF16)
    rb = jnp.full((1, LANES), NEG_BIG, F32).at[0, :ne].set(router_b)
    row = lambda i, mrow: (i, 0)
    const = lambda i, mrow: (0, 0)
    return pl.pallas_call(
        _ffnpre_body,
        grid_spec=pltpu.PrefetchScalarGridSpec(
            num_scalar_prefetch=1, grid=(nseg,),
            in_specs=[pl.BlockSpec((SEG, d), row), pl.BlockSpec((1, d), const),
                      _mod_spec(d, 3), _mod_spec(d, 4),
                      pl.BlockSpec((d, LANES), const), pl.BlockSpec((1, LANES), const)],
            out_specs=[pl.BlockSpec((SEG, d), row), pl.BlockSpec((SEG, LANES), row),
                       pl.BlockSpec((SEG, LANES), row), pl.BlockSpec((SEG, LANES), row),
                       pl.BlockSpec((1, LANES), const)],
            scratch_shapes=[pltpu.VMEM((1, LANES), F32)]),
        out_shape=[jax.ShapeDtypeStruct((t, d), BF16), jax.ShapeDtypeStruct((t, LANES), jnp.int32),
                   jax.ShapeDtypeStruct((t, LANES), F32), jax.ShapeDtypeStruct((t, LANES), jnp.int32),
                   jax.ShapeDtypeStruct((1, LANES), F32)],
        compiler_params=_cparams(1),
        name="ffn_norm_router_topk",
    )(mrow, x, norm_w.reshape(1, d), mods, mods, rw, rb)


def _experts_body(be_ref, nu_ref, x_ref, wgu_ref, bgu_ref, wdn_ref, bdn_ref, o_ref, wgu_bf, wdn_bf):
    i = pl.program_id(0)
    live = i < nu_ref[0]
    new_expert = jnp.logical_or(i == 0, be_ref[i] != be_ref[jnp.maximum(i - 1, 0)])

    @pl.when(jnp.logical_and(live, new_expert))
    def _():
        wgu_bf[...] = wgu_ref[...].astype(BF16)
        wdn_bf[...] = wdn_ref[...].astype(BF16)

    @pl.when(live)
    def _():
        hg = jnp.dot(x_ref[...], wgu_bf[...], preferred_element_type=F32) + bgu_ref[...]
        dff = hg.shape[1] // 2
        g = jnp.minimum(hg[:, :dff], SWIGLU_LIMIT)
        u = jnp.clip(hg[:, dff:], -SWIGLU_LIMIT, SWIGLU_LIMIT)
        act = g * jax.nn.sigmoid(SWIGLU_ALPHA * g) * (u + 1.0)
        y = jnp.dot(act.astype(BF16), wdn_bf[...], preferred_element_type=F32) + bdn_ref[...]
        o_ref[...] = y.astype(o_ref.dtype)

    @pl.when(i >= nu_ref[0])
    def _():
        o_ref[...] = jnp.zeros_like(o_ref)


def _experts(xg, block_e, n_used, layer, w_gu, b_gu, w_dn, b_dn):
    n_slots, d = xg.shape
    n_blocks = n_slots // MOE_ROWS
    depth, ne, _, two_f = w_gu.shape
    dff = two_f // 2
    b_gu = b_gu.reshape(depth, ne, 1, two_f)
    b_dn = b_dn.reshape(depth, ne, 1, d)
    return pl.pallas_call(
        _experts_body,
        grid_spec=pltpu.PrefetchScalarGridSpec(
            num_scalar_prefetch=2, grid=(n_blocks,),
            in_specs=[pl.BlockSpec((MOE_ROWS, d), lambda i, be, nu: (i, 0)),
                      pl.BlockSpec((None, None, d, two_f), lambda i, be, nu: (layer, be[i], 0, 0)),
                      pl.BlockSpec((None, None, 1, two_f), lambda i, be, nu: (layer, be[i], 0, 0)),
                      pl.BlockSpec((None, None, dff, d), lambda i, be, nu: (layer, be[i], 0, 0)),
                      pl.BlockSpec((None, None, 1, d), lambda i, be, nu: (layer, be[i], 0, 0))],
            out_specs=pl.BlockSpec((MOE_ROWS, d), lambda i, be, nu: (i, 0)),
            scratch_shapes=[pltpu.VMEM((d, two_f), BF16), pltpu.VMEM((dff, d), BF16)]),
        out_shape=jax.ShapeDtypeStruct((n_slots, d), BF16),
        compiler_params=_cparams(1),
        name="expert_swiglu",
    )(block_e, n_used, xg, w_gu, b_gu, w_dn, b_dn)


def _combine_body(mrow_ref, x_ref, y0_ref, y1_ref, y2_ref, y3_ref, gt_ref, g2_ref, *rest, final_norm):
    del mrow_ref
    gt = gt_ref[...]
    f = None
    for k, y_ref in enumerate((y0_ref, y1_ref, y2_ref, y3_ref)):
        term = y_ref[...].astype(F32) * gt[:, k:k + 1]
        f = term if f is None else f + term
    xn = x_ref[...] + g2_ref[0] * f
    if final_norm:
        fw_ref, o_ref = rest
        ms = jnp.mean(xn * xn, axis=-1, keepdims=True)
        o_ref[...] = xn * lax.rsqrt(ms + NORM_EPS) * fw_ref[...]
    else:
        (o_ref,) = rest
        o_ref[...] = xn


def _combine(x, yg, gates, mrow, mods, final_w=None):
    t, d = x.shape
    nseg = t // SEG
    row = lambda i, mrow: (i, 0)
    assert TOP_K == 4
    in_specs = [pl.BlockSpec((SEG, d), row)]
    in_specs += [pl.BlockSpec((SEG, d), lambda i, mrow, k=k: (k * nseg + i, 0)) for k in range(TOP_K)]
    in_specs += [pl.BlockSpec((SEG, LANES), row), _mod_spec(d, 5)]
    args = [x, yg, yg, yg, yg, gates, mods]
    if final_w is not None:
        in_specs.append(pl.BlockSpec((1, d), lambda i, mrow: (0, 0)))
        args.append(final_w.reshape(1, d))
    return pl.pallas_call(
        functools.partial(_combine_body, final_norm=final_w is not None),
        grid_spec=pltpu.PrefetchScalarGridSpec(
            num_scalar_prefetch=1, grid=(nseg,), in_specs=in_specs,
            out_specs=pl.BlockSpec((SEG, d), row)),
        out_shape=jax.ShapeDtypeStruct((t, d), F32),
        compiler_params=_cparams(1),
        name="moe_combine_residual",
    )(mrow, *args)


def _moe(x, h2, top_i, gates, rank, counts, mrow, mods, layer, w_gu, b_gu, w_dn, b_dn, final_w=None):
    t, d = x.shape
    tk = t * TOP_K
    bm = MOE_ROWS
    counts = counts.astype(jnp.int32)
    padded = (counts + bm - 1) // bm * bm
    pad_end = jnp.cumsum(padded)
    pad_start = pad_end - padded
    start = jnp.cumsum(counts) - counts
    experts = jnp.arange(N_EXPERTS, dtype=jnp.int32)
    ti4 = top_i[:, :TOP_K]
    dest = rank[:, :TOP_K] + jnp.sum(jnp.where(ti4[:, :, None] == experts, pad_start, 0), axis=-1)
    n_blocks = (tk + N_EXPERTS * (bm - 1) + bm - 1) // bm
    blk_start = jnp.arange(n_blocks, dtype=jnp.int32) * bm
    block_e = jnp.minimum(jnp.sum(pad_end[None, :] <= blk_start[:, None], axis=1),
                          N_EXPERTS - 1).astype(jnp.int32)
    n_used = (pad_end[-1:] // bm).astype(jnp.int32)
    order = jnp.argsort(ti4.reshape(tk), stable=True).astype(jnp.int32)
    off = (blk_start - pad_start[block_e])[:, None] + jnp.arange(bm, dtype=jnp.int32)
    src = jnp.where(off < counts[block_e][:, None], start[block_e][:, None] + off, 0)
    slot_tok = order[src.reshape(n_blocks * bm)] // TOP_K
    xg = h2[slot_tok]
    yb = _experts(xg, block_e, n_used, layer, w_gu, b_gu, w_dn, b_dn)
    yg = yb[dest.T.reshape(tk)]
    return _combine(x, yg, gates, mrow, mods, final_w)


def _final_body(x_ref, w_ref, o_ref):
    x = x_ref[...]
    ms = jnp.mean(x * x, axis=-1, keepdims=True)
    o_ref[...] = x * lax.rsqrt(ms + NORM_EPS) * w_ref[...]


def _final_norm(x, w):
    t, d = x.shape
    return pl.pallas_call(
        _final_body,
        grid=(t // SEG,),
        in_specs=[pl.BlockSpec((SEG, d), lambda i: (i, 0)), pl.BlockSpec((1, d), lambda i: (0, 0))],
        out_specs=pl.BlockSpec((SEG, d), lambda i: (i, 0)),
        out_shape=jax.ShapeDtypeStruct((t, d), F32),
        compiler_params=_cparams(1),
        name="final_rmsnorm",
    )(x, w.reshape(1, d))


def _segment_meta(n_lat, ss, lat0, n_ctx, sp):
    mrow, first, last, pos, lat = [], [], [], [], []
    for n_seq, length, is_lat in ((n_lat, ss, 1), (n_ctx, sp, 0)):
        per = length // SEG
        for s in range(n_seq):
            for j in range(per):
                mrow.append(1 + lat0 + s if is_lat else 0)
                first.append(int(j == 0))
                last.append(int(j == per - 1))
                pos.append(j)
                lat.append(is_lat)
    as_arr = lambda v: jnp.asarray(np.asarray(v, np.int32))
    return dict(mrow=as_arr(mrow), first=as_arr(first), last=as_arr(last), pos=as_arr(pos),
                lat=as_arr(lat), mrow_np=np.asarray(mrow, np.int32))


def _run_stream(lat0, bs, ctx0, bp, mods_all, cos_t, sin_t, x_prompt, x_sample, state_ssd, cache_k, cache_v, norm_mix_w, norm_ffn_w, ev_w_in, ev_conv_w, ev_conv_b, ev_dt_bias, ev_a_log, ev_d_skip, ev_norm_w, ev_sconv_w, ev_w_out, od_w_in, od_q_norm_w, od_k_norm_w, od_dw_w, od_dw_b, od_ln_w, od_ln_b, od_w_out, router_w, router_b, w_gu, b_gu, w_dn, b_dn, final_norm_w):
    _, sp, d = x_prompt.shape
    bs_all, ss, _ = x_sample.shape
    depth, n_rows = mods_all.shape[0], mods_all.shape[1]
    n_lat_tok = bs * ss
    n_ctx_tok = bp * sp
    past = cache_k.shape[2]
    kvw = KV_HEADS * HEAD_DIM
    assert sp % SEG == 0 and ss % SEG == 0 and past % SEG == 0 and sp % SSD_CHUNK == 0
    assert ss % ATT_TK == 0 and ATT_TK % sp == 0 and (n_lat_tok + n_ctx_tok) % ATT_TK == 0
    assert d == SSD_GROUPS * SSD_HEADS_PER_GROUP * SSD_HEAD_DIM

    meta = _segment_meta(bs, ss, lat0, bp, sp)
    mrow = meta["mrow"]
    x = jnp.concatenate([x_sample[lat0:lat0 + bs].reshape(n_lat_tok, d),
                         x_prompt[ctx0:ctx0 + bp].reshape(n_ctx_tok, d)], axis=0)
    ctx_blk0 = n_lat_tok // sp

    n_heads = SSD_GROUPS * SSD_HEADS_PER_GROUP
    conv_dim = d + 2 * SSD_GROUPS * SSD_STATE
    states, ctx_k, ctx_v = [], [], []
    for l in range(depth):
        mods = mods_all[l].reshape(n_rows, 1, 6 * d)
        i = l // 2
        if l % 2 == 0:
            w_in = ev_w_in[i]
            o_dt = d + conv_dim
            w_main = jnp.concatenate([w_in[:, :o_dt], w_in[:, o_dt + 2 * n_heads:]], axis=1).astype(BF16)
            regroup = lambda v: v.reshape(v.shape[:-1] + (2, SSD_GROUPS, SSD_HEADS_PER_GROUP)).swapaxes(-3, -2)
            pad_lanes = lambda v: jnp.zeros(v.shape[:-3] + (SSD_GROUPS, LANES), F32).at[..., :2 * SSD_HEADS_PER_GROUP].set(
                v.reshape(v.shape[:-3] + (SSD_GROUPS, 2 * SSD_HEADS_PER_GROUP)))
            w_dt = pad_lanes(regroup(w_in[:, o_dt:o_dt + 2 * n_heads])).reshape(d, SSD_GROUPS * LANES).astype(BF16)
            bias = pad_lanes(regroup(ev_dt_bias[i].reshape(2 * n_heads))).reshape(SSD_GROUPS, 1, LANES)
            alog = pad_lanes(regroup(ev_a_log[i].reshape(2 * n_heads))).reshape(SSD_GROUPS, 1, LANES)
            dsk = jnp.repeat(ev_d_skip[i], SSD_HEAD_DIM).reshape(1, d)

            main, dtp = _fused_proj(x, mrow, norm_mix_w[l], mods, 0, [w_main, w_dt], [BF16, F32])
            xbc = _ssd_conv(main, meta, ev_conv_w[i], ev_conv_b[i], d, conv_dim)
            (y_lat,) = _ssd_call(xbc, dtp, bias, alog, dsk, ss, bs, 0, state_ssd, i, lat0)
            y_ctx, st = _ssd_call(xbc, dtp, bias, alog, dsk, sp, bp, ctx_blk0, None, i, 0)
            states.append(st)
            yc = _short_conv(main, meta, ev_sconv_w[i], o_dt, o_dt + d, o_dt + 2 * d, d)
            x = _out_proj(y_lat, y_ctx, yc, ev_w_out[i].astype(BF16), x, mrow, mods, 2,
                          z_src=main, norm_w=ev_norm_w[i])
        else:
            w_in = od_w_in[i]
            wq = w_in[:, :d].reshape(d, KV_HEADS, Q_PER_KV, HEAD_DIM).swapaxes(1, 2).reshape(d, d).astype(BF16)
            wkv = w_in[:, d:d + 2 * kvw].astype(BF16)
            wglu = w_in[:, d + 2 * kvw:].astype(BF16)
            w_out = od_w_out[i]
            w_att = w_out[:d].reshape(KV_HEADS, Q_PER_KV, HEAD_DIM, d).swapaxes(0, 1).reshape(d, d)
            w_out_p = jnp.concatenate([w_att, w_out[d:]], axis=0).astype(BF16)

            q, kv, glu = _fused_proj(x, mrow, norm_mix_w[l], mods, 0, [wq, wkv, wglu], [BF16, F32, BF16])
            qn, k_att, vt_att, k_n, v_n = _qk_prep(q, kv, meta, od_q_norm_w[i], od_k_norm_w[i], cos_t, sin_t)
            ctx_k.append(k_n[n_lat_tok:].reshape(bp, sp, KV_HEADS, HEAD_DIM))
            ctx_v.append(v_n[n_lat_tok:].reshape(bp, sp, KV_HEADS, HEAD_DIM))
            ck = cache_k[:, i].reshape(bs_all * past, kvw).astype(BF16)
            cvt = cache_v[:, i].reshape(bs_all, past, kvw).swapaxes(1, 2).astype(BF16)
            per = ATT_TK // sp
            ctx_src = (k_att, pl.BlockSpec((sp, kvw), lambda s, j: (ctx_blk0 + s, 0)),
                       vt_att, pl.BlockSpec((None, kvw, sp),
                                            lambda s, j: ((ctx_blk0 + s) // per, 0, (ctx_blk0 + s) % per)), False)
            cache_src = (ck, pl.BlockSpec((past, kvw), lambda s, j: (lat0 + s, 0)),
                         cvt, pl.BlockSpec((None, kvw, past), lambda s, j: (lat0 + s, 0, 0)), False)
            lat_src = (k_att, pl.BlockSpec((ss, kvw), lambda s, j: (s, 0)),
                       vt_att, pl.BlockSpec((ss // ATT_TK, kvw, ATT_TK), lambda s, j: (s, 0, 0)), True)
            att_lat = _attn_call(qn, [cache_src, lat_src], 0, bs, ss, "attention_latent")
            att_ctx = _attn_call(qn, [ctx_src], n_lat_tok // SEG, bp, sp, "attention_ctx")
            u = _conformer_conv(glu, meta, od_dw_w[i], od_dw_b[i], od_ln_w[i], od_ln_b[i])
            x = _out_proj(att_lat, att_ctx, u, w_out_p, x, mrow, mods, 2)

        h2, ti, gt, rk, cnt = _ffn_pre(x, mrow, norm_ffn_w[l], mods, router_w[l], router_b[l])
        x = _moe(x, h2, ti, gt, rk, cnt[0, :N_EXPERTS], mrow, mods, l, w_gu, b_gu, w_dn, b_dn,
                 final_w=final_norm_w if l == depth - 1 else None)

    y = x
    y_sample = y[:n_lat_tok].reshape(bs, ss, d)
    y_prompt = y[n_lat_tok:].reshape(bp, sp, d)
    return (y_prompt, y_sample, jnp.stack(states, axis=1), jnp.stack(ctx_k, axis=1), jnp.stack(ctx_v, axis=1))


N_STREAMS = 1


def kernel(x_prompt, x_sample, state_ssd, cache_k, cache_v, c, c_ctx, w_mod, b_mod, norm_mix_w, norm_ffn_w, ev_w_in, ev_conv_w, ev_conv_b, ev_dt_bias, ev_a_log, ev_d_skip, ev_norm_w, ev_sconv_w, ev_w_out, od_w_in, od_q_norm_w, od_k_norm_w, od_dw_w, od_dw_b, od_ln_w, od_ln_b, od_w_out, router_w, router_b, w_gu, b_gu, w_dn, b_dn, final_norm_w):
    bp, _, d = x_prompt.shape
    bs, ss, _ = x_sample.shape
    n_rows = 16
    assert 1 + bs <= n_rows
    cvec = jnp.zeros((n_rows, d), F32).at[0].set(c_ctx).at[1:1 + bs].set(c)
    mods_all = _modulation(cvec, w_mod, b_mod)
    cos_t, sin_t = _rope_tables(ss)
    n_streams = N_STREAMS if (bp % N_STREAMS == 0 and bs % N_STREAMS == 0) else 1
    outs = [_run_stream(s * (bs // n_streams), bs // n_streams, s * (bp // n_streams), bp // n_streams,
                        mods_all, cos_t, sin_t, x_prompt, x_sample, state_ssd, cache_k, cache_v,
                        norm_mix_w, norm_ffn_w, ev_w_in, ev_conv_w, ev_conv_b, ev_dt_bias, ev_a_log, ev_d_skip,
                        ev_norm_w, ev_sconv_w, ev_w_out, od_w_in, od_q_norm_w, od_k_norm_w, od_dw_w, od_dw_b,
                        od_ln_w, od_ln_b, od_w_out, router_w, router_b, w_gu, b_gu, w_dn, b_dn, final_norm_w)
            for s in range(n_streams)]
    return tuple(jnp.concatenate(parts, axis=0) for parts in zip(*outs))
```
